```python
import jax, jax.numpy as jnp
from jax import lax
import numpy as np

D_MODEL = 1024
BATCH = 8
SEQ = 8192
DEPTH = 1
DEC_BATCH = 16
DEC_SEQ = 32
PAST_LEN = 4096

CHUNK = 64
ROPE_THETA = 10000.0
NORM_EPS = 1e-6
NEG = -1e30
MIX_WIDTH = D_MODEL
RET_WIDTH = MIX_WIDTH // 2
ATT_WIDTH = MIX_WIDTH - RET_WIDTH
RET_HEADS = 4
RET_DK = RET_WIDTH // RET_HEADS
RET_DV = RET_WIDTH // RET_HEADS
ATT_HEADS = 8
ATT_DH = ATT_WIDTH // ATT_HEADS
IDX_HEADS = 8
IDX_DH = 64
TOPK_MAX = 256
D_FF = 4 * D_MODEL
Q_BLOCK = 128
SPLIT_SIZES = (RET_WIDTH, RET_WIDTH, RET_WIDTH, RET_WIDTH, ATT_WIDTH, ATT_WIDTH, ATT_WIDTH, IDX_HEADS * IDX_DH, IDX_DH, IDX_HEADS)
SPLIT_POINTS = tuple(int(v) for v in np.cumsum(SPLIT_SIZES)[:-1])
IN_WIDTH = sum(SPLIT_SIZES)

kernel_name = 'retention_dsa_hybrid_stream_step'


def rms_norm(x, g):
    xf = x.astype(jnp.float32)
    y = xf * lax.rsqrt(jnp.mean(xf * xf, axis=-1, keepdims=True) + NORM_EPS)
    return (y * g.astype(jnp.float32)).astype(x.dtype)


def rope(x, pos):
    d = x.shape[-1]
    half = d // 2
    inv = ROPE_THETA ** (-jnp.arange(half, dtype=jnp.float32) * 2.0 / d)
    ang = pos.astype(jnp.float32)[:, None] * inv[None, :]
    cos = jnp.cos(ang)[None, :, None, :]
    sin = jnp.sin(ang)[None, :, None, :]
    xf = x.astype(jnp.float32)
    x1, x2 = xf[..., :half], xf[..., half:]
    return jnp.concatenate([x1 * cos - x2 * sin, x2 * cos + x1 * sin], axis=-1).astype(x.dtype)


def mixer_inputs(h, w_in, pos):
    B, T, _ = h.shape
    z = jnp.einsum('btd,de->bte', h, w_in)
    p = jnp.split(z, SPLIT_POINTS, axis=-1)
    q_r = rope(p[0].reshape(B, T, RET_HEADS, RET_DK), pos)
    k_r = rope(p[1].reshape(B, T, RET_HEADS, RET_DK), pos)
    v_r = p[2].reshape(B, T, RET_HEADS, RET_DV)
    g_r = p[3]
    q_a = rope(p[4].reshape(B, T, ATT_HEADS, ATT_DH), pos)
    k_a = rope(p[5].reshape(B, T, ATT_HEADS, ATT_DH), pos)
    v_a = p[6].reshape(B, T, ATT_HEADS, ATT_DH)
    q_i = rope(p[7].reshape(B, T, IDX_HEADS, IDX_DH), pos)
    k_i = rope(p[8][:, :, None, :], pos)[:, :, 0, :]
    w_i = p[9]
    return q_r, k_r, v_r, g_r, q_a, k_a, v_a, q_i, k_i, w_i


def retention_log_decay():
    return jnp.log1p(-jnp.exp2(-5.0 - jnp.arange(RET_HEADS, dtype=jnp.float32)))


def retention_chunkwise(q, k, v, s0, block):
    B, T, H, dk = q.shape
    n = T // block
    lg = retention_log_decay()
    qf = q.astype(jnp.float32).reshape(B, n, block, H, dk)
    kf = (k.astype(jnp.float32) * dk ** -0.5).reshape(B, n, block, H, dk)
    vf = v.astype(jnp.float32).reshape(B, n, block, H, -1)
    i = jnp.arange(block, dtype=jnp.float32)
    diff = i[:, None] - i[None, :]
    dmat = jnp.where(diff[None] >= 0, jnp.exp(lg[:, None, None] * jnp.maximum(diff, 0.0)[None]), 0.0)
    scores = jnp.einsum('bnihd,bnjhd->bnhij', qf, kf) * dmat[None, None]
    o_intra = jnp.einsum('bnhij,bnjhe->bnihe', scores, vf)
    w_end = jnp.exp(lg[None, :] * (block - 1.0 - i)[:, None])
    kv = jnp.einsum('bnjhd,jh,bnjhe->nbhde', kf, w_end, vf)
    g_block = jnp.exp(lg * block)[None, :, None, None]

    def step(s, kv_n):
        return g_block * s + kv_n, s

    s_final, s_starts = lax.scan(step, s0, kv)
    w_q = jnp.exp(lg[None, :] * (i + 1.0)[:, None])
    o_cross = jnp.einsum('bnihd,ih,nbhde->bnihe', qf, w_q, s_starts)
    return (o_intra + o_cross).reshape(B, T, H, -1), s_final


def retention_output(o, g, gn_gain, dtype):
    B, T = o.shape[:2]
    mu = jnp.mean(o, axis=-1, keepdims=True)
    var = jnp.mean(jnp.square(o - mu), axis=-1, keepdims=True)
    on = ((o - mu) * lax.rsqrt(var + NORM_EPS)).reshape(B, T, RET_WIDTH) * gn_gain.astype(jnp.float32)
    return (jax.nn.silu(g.astype(jnp.float32)) * on).astype(dtype)


def indexer_scores(qi, ki, wi):
    dots = jnp.einsum('bqhd,bld->bqhl', qi.astype(jnp.float32), ki.astype(jnp.float32)) * IDX_DH ** -0.5
    return jnp.einsum('bqhl,bqh->bql', jax.nn.relu(dots), wi.astype(jnp.float32) * IDX_HEADS ** -0.5)


def sparse_attend(q, k, v, qi, ki, wi, q_pos, topk):
    L = k.shape[1]
    key_pos = jnp.arange(L, dtype=jnp.int32)
    admissible = (key_pos[None, :] // CHUNK) <= (q_pos[:, None] // CHUNK)
    scores = jnp.where(admissible[None], indexer_scores(qi, ki, wi), NEG)
    _, idx = lax.top_k(scores, topk)
    valid = (idx // CHUNK) <= (q_pos[None, :, None] // CHUNK)
    gather = jax.vmap(lambda a, ii: a[ii])
    k_sel = gather(k, idx)
    v_sel = gather(v, idx)
    logits = jnp.einsum('bqhd,bqkhd->bqhk', q.astype(jnp.float32), k_sel.astype(jnp.float32)) * ATT_DH ** -0.5
    logits = jnp.where(valid[:, :, None, :], logits, NEG)
    p = jax.nn.softmax(logits, axis=-1)
    out = jnp.einsum('bqhk,bqkhd->bqhd', p, v_sel.astype(jnp.float32))
    return out.astype(q.dtype)


def prompt_sparse_attention(q, k, v, qi, ki, wi):
    B, T, H, dh = q.shape
    n = T // Q_BLOCK
    topk = min(TOPK_MAX, T // 4)

    def blocks(a):
        return jnp.swapaxes(a.reshape((B, n, Q_BLOCK) + a.shape[2:]), 0, 1)

    pos = jnp.arange(T, dtype=jnp.int32).reshape(n, Q_BLOCK)

    def one(args):
        qb, qib, wib, pb = args
        return sparse_attend(qb, k, v, qib, ki, wib, pb, topk)

    out = lax.map(one, (blocks(q), blocks(qi), blocks(wi), pos))
    return jnp.swapaxes(out, 0, 1).reshape(B, T, H, dh)


def sq_relu_mlp(h, w_up, w_down):
    u = jnp.einsum('btd,df->btf', h, w_up)
    return jnp.einsum('btf,fd->btd', jnp.square(jax.nn.relu(u)), w_down)


def layer_forward(x, pos, past_k, past_v, past_ki, s0, ret_block, norm1, w_in, gn_gain, w_out, norm2, w_up, w_down):
    B, T, _ = x.shape
    h = rms_norm(x, norm1)
    q_r, k_r, v_r, g_r, q_a, k_a, v_a, q_i, k_i, w_i = mixer_inputs(h, w_in, pos)
    o_r, s_new = retention_chunkwise(q_r, k_r, v_r, s0, ret_block)
    y_r = retention_output(o_r, g_r, gn_gain, x.dtype)
    if past_k is None:
        y_a = prompt_sparse_attention(q_a, k_a, v_a, q_i, k_i, w_i)
    else:
        k_all = jnp.concatenate([past_k.astype(k_a.dtype), k_a], axis=1)
        v_all = jnp.concatenate([past_v.astype(v_a.dtype), v_a], axis=1)
        ki_all = jnp.concatenate([past_ki.astype(k_i.dtype), k_i], axis=1)
        topk = min(TOPK_MAX, k_all.shape[1] // 4)
        y_a = sparse_attend(q_a, k_all, v_all, q_i, ki_all, w_i, pos, topk)
    mix = jnp.concatenate([y_r, y_a.reshape(B, T, ATT_WIDTH)], axis=-1)
    x = x + jnp.einsum('btm,md->btd', mix, w_out)
    x = x + sq_relu_mlp(rms_norm(x, norm2), w_up, w_down)
    return x, (k_a, v_a, k_i, s_new.astype(x.dtype))


def setup_inputs(seed: int = 0) -> dict:
    key = jax.random.key(seed)
    ks = jax.random.split(key, 14)
    f32 = jnp.float32

    def nrm(k, shape, s):
        return jax.random.normal(k, shape, f32) * s

    return {
        'x_prompt': nrm(ks[0], (BATCH, SEQ, D_MODEL), 1.0),
        'x_sample': nrm(ks[1], (DEC_BATCH, DEC_SEQ, D_MODEL), 1.0),
        'cache_k_att': nrm(ks[2], (DEPTH, DEC_BATCH, PAST_LEN, ATT_HEADS, ATT_DH), 1.0),
        'cache_v_att': nrm(ks[3], (DEPTH, DEC_BATCH, PAST_LEN, ATT_HEADS, ATT_DH), 1.0),
        'cache_k_idx': nrm(ks[4], (DEPTH, DEC_BATCH, PAST_LEN, IDX_DH), 1.0),
        'state_ret': nrm(ks[5], (DEPTH, DEC_BATCH, RET_HEADS, RET_DK, RET_DV), 0.1),
        'norm1': 1.0 + nrm(ks[6], (DEPTH, D_MODEL), 0.02),
        'w_in': nrm(ks[7], (DEPTH, D_MODEL, IN_WIDTH), D_MODEL ** -0.5),
        'gn_gain': 1.0 + nrm(ks[8], (DEPTH, RET_WIDTH), 0.02),
        'w_out': nrm(ks[9], (DEPTH, MIX_WIDTH, D_MODEL), MIX_WIDTH ** -0.5),
        'norm2': 1.0 + nrm(ks[10], (DEPTH, D_MODEL), 0.02),
        'w_up': nrm(ks[11], (DEPTH, D_MODEL, D_FF), D_MODEL ** -0.5),
        'w_down': nrm(ks[12], (DEPTH, D_FF, D_MODEL), D_FF ** -0.5),
        'norm_final': 1.0 + nrm(ks[13], (D_MODEL,), 0.02),
    }


def reference(x_prompt, x_sample, cache_k_att, cache_v_att, cache_k_idx, state_ret, norm1, w_in, gn_gain, w_out, norm2, w_up, w_down, norm_final):
    xp, xs = x_prompt, x_sample
    past = cache_k_att.shape[2]
    pos_p = jnp.arange(xp.shape[1], dtype=jnp.int32)
    pos_s = past + jnp.arange(xs.shape[1], dtype=jnp.int32)
    s0_p = jnp.zeros((xp.shape[0], RET_HEADS, RET_DK, RET_DV), jnp.float32)
    new_p, new_s = [], []
    for l in range(DEPTH):
        lw = (norm1[l], w_in[l], gn_gain[l], w_out[l], norm2[l], w_up[l], w_down[l])
        xp, st_p = layer_forward(xp, pos_p, None, None, None, s0_p, CHUNK, *lw)
        xs, st_s = layer_forward(xs, pos_s, cache_k_att[l], cache_v_att[l], cache_k_idx[l], state_ret[l].astype(jnp.float32), xs.shape[1], *lw)
        new_p.append(st_p)
        new_s.append(st_s)
    y_prompt = rms_norm(xp, norm_final)
    y_sample = rms_norm(xs, norm_final)
    k_att_prompt = jnp.stack([s[0] for s in new_p])
    v_att_prompt = jnp.stack([s[1] for s in new_p])
    k_idx_prompt = jnp.stack([s[2] for s in new_p])
    s_ret_prompt = jnp.stack([s[3] for s in new_p])
    k_att_sample = jnp.stack([s[0] for s in new_s])
    v_att_sample = jnp.stack([s[1] for s in new_s])
    k_idx_sample = jnp.stack([s[2] for s in new_s])
    s_ret_sample = jnp.stack([s[3] for s in new_s])
    return (y_prompt, y_sample, k_att_prompt, v_att_prompt, k_idx_prompt, s_ret_prompt, k_att_sample, v_att_sample, k_idx_sample, s_ret_sample)
```

```python
import functools

import jax
import jax.numpy as jnp
import numpy as np
from jax import lax
from jax.experimental import pallas as pl
from jax.experimental.pallas import tpu as pltpu

D_MODEL = 1024
CHUNK = 64
ROPE_THETA = 10000.0
NORM_EPS = 1e-6
NEG = -1e30
RET_WIDTH = 512
ATT_WIDTH = 512
RET_HEADS = 4
RET_DK = 128
RET_DV = 128
ATT_HEADS = 8
ATT_DH = 64
IDX_HEADS = 8
IDX_DH = 64
TOPK_MAX = 256
D_FF = 4 * D_MODEL
IN_WIDTH = 4 * RET_WIDTH + 3 * ATT_WIDTH + IDX_HEADS * IDX_DH + IDX_DH + IDX_HEADS

LANES = 128
IN_WIDTH_PADDED = ((IN_WIDTH + LANES - 1) // LANES) * LANES
VMEM_LIMIT_BYTES = 56 * 1024 * 1024

F32 = jnp.float32
BF16 = jnp.bfloat16

_NT = (((1,), (1,)), ((), ()))


def _compiler_params(semantics):
    return pltpu.CompilerParams(dimension_semantics=semantics, vmem_limit_bytes=VMEM_LIMIT_BYTES)


def _resident(block_shape, index_map):
    return pl.BlockSpec(block_shape, index_map, pipeline_mode=pl.Buffered(1))


def _rope_tables(pos):
    posf = pos.astype(F32)[:, None]

    def angles(d):
        half = d // 2
        inv = ROPE_THETA ** (-jnp.arange(half, dtype=F32) * 2.0 / d)
        ang = posf * inv[None, :]
        return jnp.cos(ang), jnp.sin(ang)

    c128, s128 = angles(128)
    zero64 = jnp.zeros_like(s128)
    cos128 = jnp.concatenate([c128, c128], axis=1)
    sin128 = jnp.concatenate([-s128, s128], axis=1)
    c64, s64 = angles(64)
    zero32 = jnp.zeros_like(s64)
    cos64 = jnp.concatenate([c64, c64, c64, c64], axis=1)
    sin64_up = jnp.concatenate([zero32, s64, zero32, s64], axis=1)
    sin64_dn = jnp.concatenate([-s64, zero32, -s64, zero32], axis=1)
    one64 = jnp.ones_like(zero64)
    cosk = jnp.concatenate([c64, c64, one64], axis=1)
    sink_up = jnp.concatenate([zero32, s64, zero64], axis=1)
    sink_dn = jnp.concatenate([-s64, zero32, zero64], axis=1)
    return jnp.stack([cos128, sin128, cos64, sin64_up, sin64_dn, cosk, sink_up, sink_dn])


def _proj_kernel(x_ref, g_ref, w_ref, tab_ref, ret_ref, qa_ref, ka_ref, va_ref, qi_ref, kw_ref):
    x = x_ref[...]
    ms = jnp.mean(x * x, axis=-1, keepdims=True)
    h = ((x * lax.rsqrt(ms + NORM_EPS)) * g_ref[...]).astype(BF16)

    def mm(c0, n):
        return jnp.dot(h, w_ref[:, c0:c0 + n], preferred_element_type=F32)

    def rope128(z):
        return z * tab_ref[0] + pltpu.roll(z, 64, 1) * tab_ref[1]

    def rope64(z):
        return z * tab_ref[2] + pltpu.roll(z, 32, 1) * tab_ref[3] + pltpu.roll(z, 96, 1) * tab_ref[4]

    def ropek(z):
        return z * tab_ref[5] + pltpu.roll(z, 32, 1) * tab_ref[6] + pltpu.roll(z, 96, 1) * tab_ref[7]

    def emit(out_ref, out_c0, w_c0, rope):
        z = mm(w_c0, 512)
        for t in range(4):
            zt = z[:, t * LANES:(t + 1) * LANES]
            if rope is not None:
                zt = rope(zt)
            out_ref[:, out_c0 + t * LANES:out_c0 + (t + 1) * LANES] = zt.astype(out_ref.dtype)

    emit(ret_ref, 0, 0, rope128)
    emit(ret_ref, 512, 512, rope128)
    emit(ret_ref, 1024, 1024, None)
    emit(ret_ref, 1536, 1536, None)
    emit(qa_ref, 0, 2048, rope64)
    emit(ka_ref, 0, 2560, rope64)
    emit(va_ref, 0, 3072, None)
    emit(qi_ref, 0, 3584, rope64)
    kw_ref[...] = ropek(mm(4096, LANES))


def _proj(x2d, norm_g, w_in_b, tabs, tm):
    m = x2d.shape[0]
    n_tab = tabs.shape[1] // tm
    row = lambda i: (i, 0)
    outs = (
        jax.ShapeDtypeStruct((m, 4 * RET_WIDTH), F32),
        jax.ShapeDtypeStruct((m, ATT_WIDTH), F32),
        jax.ShapeDtypeStruct((m, ATT_WIDTH), F32),
        jax.ShapeDtypeStruct((m, ATT_WIDTH), F32),
        jax.ShapeDtypeStruct((m, IDX_HEADS * IDX_DH), F32),
        jax.ShapeDtypeStruct((m, LANES), F32),
    )
    return pl.pallas_call(
        _proj_kernel,
        grid=(m // tm,),
        in_specs=[
            pl.BlockSpec((tm, D_MODEL), row),
            _resident((1, D_MODEL), lambda i: (0, 0)),
            _resident((D_MODEL, IN_WIDTH_PADDED), lambda i: (0, 0)),
            pl.BlockSpec((8, tm, LANES), lambda i: (0, i % n_tab, 0)),
        ],
        out_specs=[
            pl.BlockSpec((tm, 4 * RET_WIDTH), row),
            pl.BlockSpec((tm, ATT_WIDTH), row),
            pl.BlockSpec((tm, ATT_WIDTH), row),
            pl.BlockSpec((tm, ATT_WIDTH), row),
            pl.BlockSpec((tm, IDX_HEADS * IDX_DH), row),
            pl.BlockSpec((tm, LANES), row),
        ],
        out_shape=outs,
        compiler_params=_compiler_params(("parallel",)),
        name="proj",
    )(x2d, norm_g, w_in_b, tabs)


def _retention_tables(blk):
    lg = jnp.log1p(-jnp.exp2(-5.0 - jnp.arange(RET_HEADS, dtype=F32)))
    i = jnp.arange(blk, dtype=F32)
    diff = i[:, None] - i[None, :]
    dmat = jnp.where(diff[None] >= 0, jnp.exp(lg[:, None, None] * jnp.maximum(diff, 0.0)[None]), 0.0)
    w_end = jnp.exp(lg[:, None] * (blk - 1.0 - i)[None, :])
    w_q = jnp.exp(lg[:, None] * (i + 1.0)[None, :])
    g_block = jnp.exp(lg * blk)
    lane = lambda a: jnp.broadcast_to(a[:, :, None], (RET_HEADS, blk, LANES))
    return dmat, lane(w_q), lane(w_end), g_block


def _retention_kernel(gblk_ref, q_ref, k_ref, v_ref, g_ref, s0_ref, dmat_ref, wq_ref, wend_ref, gain_ref,
                      y_ref, sfin_ref, s_scr):
    t = pl.program_id(1)

    @pl.when(t == 0)
    def _():
        s_scr[...] = s0_ref[0]

    for hd in range(RET_HEADS):
        cols = slice(hd * RET_DK, (hd + 1) * RET_DK)
        q = q_ref[:, cols]
        k = k_ref[:, cols] * RET_DK ** -0.5
        vb = v_ref[:, cols].astype(BF16)
        state = s_scr[hd]
        scores = lax.dot_general(q.astype(BF16), k.astype(BF16), _NT, preferred_element_type=F32) * dmat_ref[hd]
        o = jnp.dot(scores.astype(BF16), vb, preferred_element_type=F32)
        o = o + jnp.dot((q * wq_ref[hd]).astype(BF16), state.astype(BF16), preferred_element_type=F32)
        k_end_t = (k * wend_ref[hd]).T.astype(BF16)
        s_scr[hd] = gblk_ref[hd] * state + jnp.dot(k_end_t, vb, preferred_element_type=F32)
        mu = jnp.mean(o, axis=-1, keepdims=True)
        var = jnp.mean(jnp.square(o - mu), axis=-1, keepdims=True)
        on = ((o - mu) * lax.rsqrt(var + NORM_EPS)) * gain_ref[:, cols]
        g = g_ref[:, cols]
        y_ref[:, cols] = ((g * jax.nn.sigmoid(g)) * on).astype(y_ref.dtype)

    @pl.when(t == pl.num_programs(1) - 1)
    def _():
        sfin_ref[0] = s_scr[...]


def _retention(ret_in, s0, gn_gain, batch, seq, blk):
    nt = seq // blk
    dmat, w_q, w_end, g_block = _retention_tables(blk)

    def col(c):
        return pl.BlockSpec((blk, RET_WIDTH), lambda b, t: (b * nt + t, c))

    const3 = lambda b, t: (0, 0, 0)
    return pl.pallas_call(
        _retention_kernel,
        grid=(batch, nt),
        in_specs=[
            pl.BlockSpec(memory_space=pltpu.SMEM),
            col(0), col(1), col(2), col(3),
            pl.BlockSpec((1, RET_HEADS, RET_DK, RET_DV), lambda b, t: (b, 0, 0, 0)),
            pl.BlockSpec((RET_HEADS, blk, blk), const3),
            pl.BlockSpec((RET_HEADS, blk, LANES), const3),
            pl.BlockSpec((RET_HEADS, blk, LANES), const3),
            pl.BlockSpec((1, RET_WIDTH), lambda b, t: (0, 0)),
        ],
        out_specs=[
            pl.BlockSpec((blk, RET_WIDTH), lambda b, t: (b * nt + t, 0)),
            pl.BlockSpec((1, RET_HEADS, RET_DK, RET_DV), lambda b, t: (b, 0, 0, 0)),
        ],
        out_shape=(
            jax.ShapeDtypeStruct((batch * seq, RET_WIDTH), BF16),
            jax.ShapeDtypeStruct((batch, RET_HEADS, RET_DK, RET_DV), F32),
        ),
        scratch_shapes=[pltpu.VMEM((RET_HEADS, RET_DK, RET_DV), F32)],
        compiler_params=_compiler_params(("parallel", "arbitrary")),
        name="retention",
    )(g_block, ret_in, ret_in, ret_in, ret_in, s0, dmat, w_q, w_end, gn_gain)


COUNT_ROWS = 32
MASKED = 2.0 * NEG


def _ordered_bits_to_float(u):
    bits = jnp.where(u < 0, u ^ jnp.int32(-2 ** 31), ~u)
    return lax.bitcast_convert_type(bits, F32)


def _attention_kernel(lim_ref, qa_ref, qi_ref, wt_ref, ki_ref, ka_ref, vt_ref, o_ref,
                      s_scr, bias_scr, acc_scr, m_scr, l_scr,
                      *, qb, tk, n_keys, topk, causal, n_valid_q, index_bits):
    n_kt = pl.program_id(1) * (qb // tk) + (qb // tk) if causal else ki_ref.shape[0] // tk
    lim = lim_ref[0]
    lane = lax.broadcasted_iota(jnp.int32, (qb, LANES), 1)
    key_iota = lax.broadcasted_iota(jnp.int32, (tk, qb), 0)

    def head_operand(ref, hd, scale):
        pair = ref[:, (hd // 2) * LANES:(hd // 2 + 1) * LANES] * scale
        return pair, (hd % 2)

    qi_heads = []
    for hd in range(IDX_HEADS):
        pair, odd = head_operand(qi_ref, hd, 1.0)
        if odd:
            pair = pltpu.roll(pair, 64, 1)
        qi_heads.append(jnp.where(lane < IDX_DH, pair, 0.0).astype(BF16))
    wt = wt_ref[0]

    def score_body(kt, carry):
        r0 = pl.multiple_of(kt * tk, tk)
        keys = ki_ref[pl.ds(r0, tk), :]
        acc = jnp.zeros((tk, qb), F32)
        for hd in range(IDX_HEADS):
            d = lax.dot_general(keys, qi_heads[hd], _NT, preferred_element_type=F32)
            acc = acc + jnp.maximum(d, 0.0) * wt[hd:hd + 1, :]
        s_scr[pl.ds(r0, tk), :] = jnp.where(key_iota + r0 < lim, acc, -jnp.inf)
        return carry

    lax.fori_loop(0, n_kt, score_body, 0)

    n_steps = n_kt * (tk // COUNT_ROWS)

    def count(indicator):
        def body(c, acc):
            r0 = pl.multiple_of(c * COUNT_ROWS, COUNT_ROWS)
            return acc + indicator(s_scr[pl.ds(r0, COUNT_ROWS), :], r0)
        acc = lax.fori_loop(0, n_steps, body, jnp.zeros((COUNT_ROWS, qb), F32))
        return jnp.sum(acc, axis=0, keepdims=True)

    n_masked = (n_keys - lim).astype(F32)
    kf = float(topk)

    def bisect_body(it, carry):
        cur, cnt = carry
        cand = cur | lax.shift_left(jnp.int32(1), 31 - it)
        theta = _ordered_bits_to_float(cand)
        c = count(lambda s, r0: jnp.where(s >= theta, 1.0, 0.0))
        ok =c + jnp.where(theta <= NEG, n_masked, 0.0) >= kf
        return jnp.where(ok, cand, cur), jnp.where(ok, c, cnt)

    cur, cnt_ge = lax.fori_loop(0, 32, bisect_body,
                                (jnp.zeros((1, qb), jnp.int32), jnp.zeros((1, qb), F32)))
    theta = _ordered_bits_to_float(cur)

    q_lane = lax.broadcasted_iota(jnp.int32, (1, qb), 1)
    overflow = jnp.where((cnt_ge > kf) & (q_lane < n_valid_q), 1.0, 0.0)
    all_idx = jnp.full((1, qb), 2 ** index_bits - 1, jnp.int32)

    def tie_cut():
        need = kf - count(lambda s, r0: jnp.where(s > theta, 1.0, 0.0))
        rows = lax.broadcasted_iota(jnp.int32, (COUNT_ROWS, qb), 0)

        def body(it, cut):
            cand = cut | lax.shift_left(jnp.int32(1), index_bits - 1 - it)
            c = count(lambda s, r0: jnp.where(s == theta, jnp.where(rows + r0 < cand, 1.0, 0.0), 0.0))
            return jnp.where(c <= need, cand, cut)

        return lax.fori_loop(0, index_bits, body, jnp.zeros((1, qb), jnp.int32))

    idx_cut = lax.cond(jnp.max(overflow) > 0.0, tie_cut, lambda: all_idx)

    qa_heads = []
    for hd in range(ATT_HEADS):
        pair, odd = head_operand(qa_ref, hd, ATT_DH ** -0.5)
        keep = (lane >= ATT_DH) if odd else (lane < ATT_DH)
        qa_heads.append(jnp.where(keep, pair, 0.0).astype(BF16))
    m_scr[...] = jnp.full(m_scr.shape, NEG, F32)
    l_scr[...] = jnp.zeros(l_scr.shape, F32)
    acc_scr[...] = jnp.zeros(acc_scr.shape, F32)

    def attend_body(kt, carry):
        r0 = pl.multiple_of(kt * tk, tk)
        s = s_scr[pl.ds(r0, tk), :]
        tie = jnp.where(key_iota + r0 < idx_cut, 0.0, MASKED)
        bias_scr[...] = jnp.where(s > theta, 0.0, jnp.where(s == theta, tie, MASKED))
        for hd in range(ATT_HEADS):
            keys = ka_ref[pl.ds(r0, tk), (hd // 2) * LANES:(hd // 2 + 1) * LANES]
            logits = lax.dot_general(keys, qa_heads[hd], _NT, preferred_element_type=F32) + bias_scr[...]
            m_old = m_scr[hd:hd + 1, :]
            m_new = jnp.maximum(m_old, jnp.max(logits, axis=0, keepdims=True))
            alpha = jnp.exp(m_old - m_new)
            p = jnp.exp(logits - m_new)
            l_scr[hd:hd + 1, :] = alpha * l_scr[hd:hd + 1, :] + jnp.sum(p, axis=0, keepdims=True)
            m_scr[hd:hd + 1, :] = m_new
            rows = slice(hd * ATT_DH, (hd + 1) * ATT_DH)
            pv = jnp.dot(vt_ref[kt, rows, :], p.astype(BF16), preferred_element_type=F32)
            acc_scr[rows, :] = alpha * acc_scr[rows, :] + pv
        return carry

    lax.fori_loop(0, n_kt, attend_body, 0)

    for hd in range(ATT_HEADS):
        rows = slice(hd * ATT_DH, (hd + 1) * ATT_DH)
        acc_scr[rows, :] = acc_scr[rows, :] / l_scr[hd:hd + 1, :]
    o_ref[...] = acc_scr[...].T.astype(o_ref.dtype)


def _attention(q_a, q_i, w_t, lim, k_i, k_a, v_t, *, qb, tk, n_keys, topk, causal, n_valid_q):
    batch, tq, _ = q_a.shape
    lp = k_i.shape[1]
    nq = tq // qb
    index_bits = int(np.ceil(np.log2(lp + 1)))
    kernel = functools.partial(_attention_kernel, qb=qb, tk=tk, n_keys=n_keys, topk=topk, causal=causal,
                               n_valid_q=n_valid_q, index_bits=index_bits)
    return pl.pallas_call(
        kernel,
        grid=(batch, nq),
        in_specs=[
            pl.BlockSpec((1, 1, qb), lambda b, i: (i, 0, 0)),
            pl.BlockSpec((None, qb, ATT_WIDTH), lambda b, i: (b, i, 0)),
            pl.BlockSpec((None, qb, IDX_HEADS * IDX_DH), lambda b, i: (b, i, 0)),
            pl.BlockSpec((1, IDX_HEADS, qb), lambda b, i: (b, 0, i)),
            _resident((None, lp, LANES), lambda b, i: (b, 0, 0)),
            _resident((None, lp, ATT_WIDTH), lambda b, i: (b, 0, 0)),
            _resident((None, lp // tk, ATT_WIDTH, tk), lambda b, i: (b, 0, 0, 0)),
        ],
        out_specs=pl.BlockSpec((None, qb, ATT_WIDTH), lambda b, i: (b, i, 0)),
        out_shape=jax.ShapeDtypeStruct((batch, tq, ATT_WIDTH), BF16),
        scratch_shapes=[
            pltpu.VMEM((lp, qb), F32),
            pltpu.VMEM((tk, qb), F32),
            pltpu.VMEM((ATT_WIDTH, qb), F32),
            pltpu.VMEM((ATT_HEADS, qb), F32),
            pltpu.VMEM((ATT_HEADS, qb), F32),
        ],
        compiler_params=_compiler_params(("parallel", "arbitrary")),
        name="attention",
    )(lim, q_a, q_i, w_t, k_i, k_a, v_t)


FF_CHUNK = 512


def _mlp_kernel(x_ref, yr_ref, ya_ref, wo_ref, g2_ref, wup_ref, wdn_ref, gf_ref, y_ref, x_scr, h_scr, a_scr):
    x = x_ref[...] + jnp.dot(yr_ref[...], wo_ref[:RET_WIDTH, :], preferred_element_type=F32)
    x = x + jnp.dot(ya_ref[...], wo_ref[RET_WIDTH:, :], preferred_element_type=F32)
    x_scr[...] = x
    ms = jnp.mean(x * x, axis=-1, keepdims=True)
    h_scr[...] = ((x * lax.rsqrt(ms + NORM_EPS)) * g2_ref[...]).astype(h_scr.dtype)
    for c in range(D_FF // FF_CHUNK):
        cols = slice(c * FF_CHUNK, (c + 1) * FF_CHUNK)
        u = jnp.dot(h_scr[...], wup_ref[:, cols], preferred_element_type=F32)
        a_scr[:, cols] = jnp.square(jnp.maximum(u, 0.0)).astype(a_scr.dtype)
    x = x_scr[...] + jnp.dot(a_scr[...], wdn_ref[...], preferred_element_type=F32)
    ms = jnp.mean(x * x, axis=-1, keepdims=True)
    y_ref[...] = (x * lax.rsqrt(ms + NORM_EPS)) * gf_ref[...]


def _mlp(x2d, y_r, y_a, w_out_b, norm2, w_up_b, w_down_b, norm_final, tm):
    m = x2d.shape[0]
    row = lambda i: (i, 0)
    const = lambda i: (0, 0)
    return pl.pallas_call(
        _mlp_kernel,
        grid=(m // tm,),
        in_specs=[
            pl.BlockSpec((tm, D_MODEL), row),
            pl.BlockSpec((tm, RET_WIDTH), row),
            pl.BlockSpec((tm, ATT_WIDTH), row),
            _resident((D_MODEL, D_MODEL), const),
            _resident((1, D_MODEL), const),
            _resident((D_MODEL, D_FF), const),
            _resident((D_FF, D_MODEL), const),
            _resident((1, D_MODEL), const),
        ],
        out_specs=pl.BlockSpec((tm, D_MODEL), row),
        out_shape=jax.ShapeDtypeStruct((m, D_MODEL), F32),
        scratch_shapes=[pltpu.VMEM((tm, D_MODEL), F32), pltpu.VMEM((tm, D_MODEL), BF16),
                        pltpu.VMEM((tm, D_FF), BF16)],
        compiler_params=_compiler_params(("parallel",)),
        name="mlp",
    )(x2d, y_r, y_a, w_out_b, norm2, w_up_b, w_down_b, norm_final)


def _pick_tile(n, target):
    t = min(n, target)
    while n % t:
        t //= 2
    return t


def _layer(x, pos, past, s0, ret_blk, weights, *, attn_qb, attn_tk):
    norm1, w_in_b, gn_gain, w_out_b, norm2, w_up_b, w_down_b, norm_final = weights
    batch, seq, _ = x.shape
    m = batch * seq
    x2d = x.reshape(m, D_MODEL)
    tm = _pick_tile(m, 512)

    tabs = _rope_tables(pos)
    if seq < tm:
        tabs = jnp.tile(tabs, (1, tm // seq, 1))
    ret_in, q_a, k_a, v_a, q_i, kw = _proj(x2d, norm1, w_in_b, tabs, tm)

    y_r, s_new = _retention(ret_in, s0, gn_gain, batch, seq, ret_blk)

    three = lambda a: a.reshape(batch, seq, a.shape[-1])
    k_i = three(kw[:, :IDX_DH])
    w_t = jnp.swapaxes(three(kw[:, IDX_DH:IDX_DH + IDX_HEADS]) * IDX_HEADS ** -0.5 * IDX_DH ** -0.5, 1, 2)
    q_a3, q_i3 = three(q_a), three(q_i)
    if past is None:
        n_keys = seq
        keys_i, keys_a, vals = three(kw), three(k_a), three(v_a)
        qpos = pos
        n_valid_q = attn_qb
    else:
        past_k, past_v, past_ki = past
        n_keys = past_k.shape[1] + seq
        pad_i = jnp.pad(past_ki, ((0, 0), (0, 0), (0, LANES - IDX_DH)))
        keys_i = jnp.concatenate([pad_i, three(kw)], axis=1)
        keys_a = jnp.concatenate([past_k.reshape(batch, -1, ATT_WIDTH), three(k_a)], axis=1)
        vals = jnp.concatenate([past_v.reshape(batch, -1, ATT_WIDTH), three(v_a)], axis=1)
        qpos = pos
        n_valid_q = seq
    topk = min(TOPK_MAX, n_keys // 4)
    tq = -(-seq // attn_qb) * attn_qb
    lp = -(-n_keys // attn_tk) * attn_tk
    padq = lambda a: jnp.pad(a, ((0, 0), (0, tq - seq), (0, 0)))
    padk = lambda a: jnp.pad(a, ((0, 0), (0, lp - n_keys), (0, 0)))
    lim = jnp.minimum((qpos // CHUNK + 1) * CHUNK, n_keys).astype(jnp.int32)
    lim = jnp.pad(lim, (0, tq - seq), constant_values=n_keys).reshape(tq // attn_qb, 1, attn_qb)
    v_t = jnp.swapaxes(padk(vals).astype(BF16).reshape(batch, lp // attn_tk, attn_tk, ATT_WIDTH), 2, 3)
    y_a = _attention(
        padq(q_a3), padq(q_i3), jnp.pad(w_t, ((0, 0), (0, 0), (0, tq - seq))), lim,
        padk(keys_i).astype(BF16), padk(keys_a).astype(BF16), v_t,
        qb=attn_qb, tk=attn_tk, n_keys=n_keys, topk=topk, causal=past is None, n_valid_q=n_valid_q)
    y_a = y_a[:, :seq].reshape(m, ATT_WIDTH)

    y = _mlp(x2d, y_r, y_a, w_out_b, norm2, w_up_b, w_down_b, norm_final, tm)
    return (y.reshape(batch, seq, D_MODEL),
            k_a.reshape(batch, seq, ATT_HEADS, ATT_DH),
            v_a.reshape(batch, seq, ATT_HEADS, ATT_DH),
            k_i, s_new)


def kernel(x_prompt, x_sample, cache_k_att, cache_v_att, cache_k_idx, state_ret, norm1, w_in, gn_gain, w_out,
           norm2, w_up, w_down, norm_final):
    depth = norm1.shape[0]
    assert depth == 1, "the final norm is fused into the layer's last kernel"
    past_len = cache_k_att.shape[2]
    pos_p = jnp.arange(x_prompt.shape[1], dtype=jnp.int32)
    pos_s = past_len + jnp.arange(x_sample.shape[1], dtype=jnp.int32)
    l = 0
    w_in_b = jnp.pad(w_in[l], ((0, 0), (0, IN_WIDTH_PADDED - IN_WIDTH))).astype(BF16)
    weights = (norm1[l][None], w_in_b, gn_gain[l][None], w_out[l].astype(BF16), norm2[l][None],
               w_up[l].astype(BF16), w_down[l].astype(BF16), norm_final[None])
    s0_p = jnp.zeros((x_prompt.shape[0], RET_HEADS, RET_DK, RET_DV), F32)
    yp, kap, vap, kip, sp = _layer(x_prompt, pos_p, None, s0_p, 256, weights, attn_qb=256, attn_tk=256)
    past = (cache_k_att[l], cache_v_att[l], cache_k_idx[l])
    ys, kas, vas, kis, ss = _layer(x_sample, pos_s, past, state_ret[l].astype(F32), x_sample.shape[1], weights,
                                   attn_qb=128, attn_tk=256)
    return (yp, ys, kap[None], vap[None], kip[None], sp[None], kas[None], vas[None], kis[None], ss[None])
```

```python
import functools

import jax
import jax.numpy as jnp
import numpy as np
from jax import lax
from jax.experimental import pallas as pl
from jax.experimental.pallas import tpu as pltpu

D_MODEL = 1024
CHUNK = 64
ROPE_THETA = 10000.0
NORM_EPS = 1e-6
NEG = -1e30
RET_WIDTH = 512
ATT_WIDTH = 512
RET_HEADS = 4
RET_DK = 128
RET_DV = 128
ATT_HEADS = 8
ATT_DH = 64
IDX_HEADS = 8
IDX_DH = 64
TOPK_MAX = 256
D_FF = 4 * D_MODEL
IN_WIDTH = 4 * RET_WIDTH + 3 * ATT_WIDTH + IDX_HEADS * IDX_DH + IDX_DH + IDX_HEADS

LANES = 128
IN_WIDTH_PADDED = ((IN_WIDTH + LANES - 1) // LANES) * LANES
VMEM_LIMIT_BYTES = 56 * 1024 * 1024

F32 = jnp.float32
BF16 = jnp.bfloat16

_NT = (((1,), (1,)), ((), ()))


def _compiler_params(semantics):
    return pltpu.CompilerParams(dimension_semantics=semantics, vmem_limit_bytes=VMEM_LIMIT_BYTES)


def _resident(block_shape, index_map):
    return pl.BlockSpec(block_shape, index_map, pipeline_mode=pl.Buffered(1))


def _rope_tables(pos):
    posf = pos.astype(F32)[:, None]

    def angles(d):
        half = d // 2
        inv = ROPE_THETA ** (-jnp.arange(half, dtype=F32) * 2.0 / d)
        ang = posf * inv[None, :]
        return jnp.cos(ang), jnp.sin(ang)

    c128, s128 = angles(128)
    zero64 = jnp.zeros_like(s128)
    cos128 = jnp.concatenate([c128, c128], axis=1)
    sin128 = jnp.concatenate([-s128, s128], axis=1)
    c64, s64 = angles(64)
    zero32 = jnp.zeros_like(s64)
    cos64 = jnp.concatenate([c64, c64, c64, c64], axis=1)
    sin64_up = jnp.concatenate([zero32, s64, zero32, s64], axis=1)
    sin64_dn = jnp.concatenate([-s64, zero32, -s64, zero32], axis=1)
    one64 = jnp.ones_like(zero64)
    cosk = jnp.concatenate([c64, c64, one64], axis=1)
    sink_up = jnp.concatenate([zero32, s64, zero64], axis=1)
    sink_dn = jnp.concatenate([-s64, zero32, zero64], axis=1)
    return jnp.stack([cos128, sin128, cos64, sin64_up, sin64_dn, cosk, sink_up, sink_dn])


def _proj_kernel(x_ref, g_ref, w_ref, tab_ref, ret_ref, qa_ref, ka_ref, va_ref, qi_ref, kw_ref):
    x = x_ref[...]
    ms = jnp.mean(x * x, axis=-1, keepdims=True)
    h = ((x * lax.rsqrt(ms + NORM_EPS)) * g_ref[...]).astype(BF16)

    def mm(c0, n):
        return jnp.dot(h, w_ref[:, c0:c0 + n], preferred_element_type=F32)

    def rope128(z):
        return z * tab_ref[0] + pltpu.roll(z, 64, 1) * tab_ref[1]

    def rope64(z):
        return z * tab_ref[2] + pltpu.roll(z, 32, 1) * tab_ref[3] + pltpu.roll(z, 96, 1) * tab_ref[4]

    def ropek(z):
        return z * tab_ref[5] + pltpu.roll(z, 32, 1) * tab_ref[6] + pltpu.roll(z, 96, 1) * tab_ref[7]

    def emit(out_ref, out_c0, w_c0, rope):
        z = mm(w_c0, 512)
        for t in range(4):
            zt = z[:, t * LANES:(t + 1) * LANES]
            if rope is not None:
                zt = rope(zt)
            out_ref[:, out_c0 + t * LANES:out_c0 + (t + 1) * LANES] = zt.astype(out_ref.dtype)

    emit(ret_ref, 0, 0, rope128)
    emit(ret_ref, 512, 512, rope128)
    emit(ret_ref, 1024, 1024, None)
    emit(ret_ref, 1536, 1536, None)
    emit(qa_ref, 0, 2048, rope64)
    emit(ka_ref, 0, 2560, rope64)
    emit(va_ref, 0, 3072, None)
    emit(qi_ref, 0, 3584, rope64)
    kw_ref[...] = ropek(mm(4096, LANES))


def _proj(x2d, norm_g, w_in_b, tabs, tm):
    m = x2d.shape[0]
    n_tab = tabs.shape[1] // tm
    row = lambda i: (i, 0)
    outs = (
        jax.ShapeDtypeStruct((m, 4 * RET_WIDTH), F32),
        jax.ShapeDtypeStruct((m, ATT_WIDTH), F32),
        jax.ShapeDtypeStruct((m, ATT_WIDTH), F32),
        jax.ShapeDtypeStruct((m, ATT_WIDTH), F32),
        jax.ShapeDtypeStruct((m, IDX_HEADS * IDX_DH), F32),
        jax.ShapeDtypeStruct((m, LANES), F32),
    )
    return pl.pallas_call(
        _proj_kernel,
        grid=(m // tm,),
        in_specs=[
            pl.BlockSpec((tm, D_MODEL), row),
            _resident((1, D_MODEL), lambda i: (0, 0)),
            _resident((D_MODEL, IN_WIDTH_PADDED), lambda i: (0, 0)),
            pl.BlockSpec((8, tm, LANES), lambda i: (0, i % n_tab, 0)),
        ],
        out_specs=[
            pl.BlockSpec((tm, 4 * RET_WIDTH), row),
            pl.BlockSpec((tm, ATT_WIDTH), row),
            pl.BlockSpec((tm, ATT_WIDTH), row),
            pl.BlockSpec((tm, ATT_WIDTH), row),
            pl.BlockSpec((tm, IDX_HEADS * IDX_DH), row),
            pl.BlockSpec((tm, LANES), row),
        ],
        out_shape=outs,
        compiler_params=_compiler_params(("parallel",)),
        name="proj",
    )(x2d, norm_g, w_in_b, tabs)


def _retention_tables(blk):
    lg = jnp.log1p(-jnp.exp2(-5.0 - jnp.arange(RET_HEADS, dtype=F32)))
    i = jnp.arange(blk, dtype=F32)
    diff = i[:, None] - i[None, :]
    dmat = jnp.where(diff[None] >= 0, jnp.exp(lg[:, None, None] * jnp.maximum(diff, 0.0)[None]), 0.0)
    w_end = jnp.exp(lg[:, None] * (blk - 1.0 - i)[None, :])
    w_q = jnp.exp(lg[:, None] * (i + 1.0)[None, :])
    g_block = jnp.exp(lg * blk)
    lane = lambda a: jnp.broadcast_to(a[:, :, None], (RET_HEADS, blk, LANES))
    return dmat, lane(w_q), lane(w_end), g_block


def _retention_kernel(gblk_ref, q_ref, k_ref, v_ref, g_ref, s0_ref, dmat_ref, wq_ref, wend_ref, gain_ref,
                      y_ref, sfin_ref, s_scr):
    t = pl.program_id(1)

    @pl.when(t == 0)
    def _():
        s_scr[...] = s0_ref[0]

    for hd in range(RET_HEADS):
        cols = slice(hd * RET_DK, (hd + 1) * RET_DK)
        q = q_ref[:, cols]
        k = k_ref[:, cols] * RET_DK ** -0.5
        vb = v_ref[:, cols].astype(BF16)
        state = s_scr[hd]
        scores = lax.dot_general(q.astype(BF16), k.astype(BF16), _NT, preferred_element_type=F32) * dmat_ref[hd]
        o = jnp.dot(scores.astype(BF16), vb, preferred_element_type=F32)
        o = o + jnp.dot((q * wq_ref[hd]).astype(BF16), state.astype(BF16), preferred_element_type=F32)
        k_end_t = (k * wend_ref[hd]).T.astype(BF16)
        s_scr[hd] = gblk_ref[hd] * state + jnp.dot(k_end_t, vb, preferred_element_type=F32)
        mu = jnp.mean(o, axis=-1, keepdims=True)
        var = jnp.mean(jnp.square(o - mu), axis=-1, keepdims=True)
        on = ((o - mu) * lax.rsqrt(var + NORM_EPS)) * gain_ref[:, cols]
        g = g_ref[:, cols]
        y_ref[:, cols] = ((g * jax.nn.sigmoid(g)) * on).astype(y_ref.dtype)

    @pl.when(t == pl.num_programs(1) - 1)
    def _():
        sfin_ref[0] = s_scr[...]


def _retention(ret_in, s0, gn_gain, batch, seq, blk):
    nt = seq // blk
    dmat, w_q, w_end, g_block = _retention_tables(blk)

    def col(c):
        return pl.BlockSpec((blk, RET_WIDTH), lambda b, t: (b * nt + t, c))

    const3 = lambda b, t: (0, 0, 0)
    return pl.pallas_call(
        _retention_kernel,
        grid=(batch, nt),
        in_specs=[
            pl.BlockSpec(memory_space=pltpu.SMEM),
            col(0), col(1), col(2), col(3),
            pl.BlockSpec((1, RET_HEADS, RET_DK, RET_DV), lambda b, t: (b, 0, 0, 0)),
            pl.BlockSpec((RET_HEADS, blk, blk), const3),
            pl.BlockSpec((RET_HEADS, blk, LANES), const3),
            pl.BlockSpec((RET_HEADS, blk, LANES), const3),
            pl.BlockSpec((1, RET_WIDTH), lambda b, t: (0, 0)),
        ],
        out_specs=[
            pl.BlockSpec((blk, RET_WIDTH), lambda b, t: (b * nt + t, 0)),
            pl.BlockSpec((1, RET_HEADS, RET_DK, RET_DV), lambda b, t: (b, 0, 0, 0)),
        ],
        out_shape=(
            jax.ShapeDtypeStruct((batch * seq, RET_WIDTH), BF16),
            jax.ShapeDtypeStruct((batch, RET_HEADS, RET_DK, RET_DV), F32),
        ),
        scratch_shapes=[pltpu.VMEM((RET_HEADS, RET_DK, RET_DV), F32)],
        compiler_params=_compiler_params(("parallel", "arbitrary")),
        name="retention",
    )(g_block, ret_in, ret_in, ret_in, ret_in, s0, dmat, w_q, w_end, gn_gain)


COUNT_ROWS = 32
MASKED = 2.0 * NEG
LOG2_E = 1.4426950408889634


def _ordered_bits_to_float(u):
    bits = jnp.where(u < 0, u ^ jnp.int32(-2 ** 31), ~u)
    return lax.bitcast_convert_type(bits, F32)


def _attention_kernel(lim_ref, qa_ref, qi_ref, wt_ref, ki_ref, ka_ref, vt_ref, o_ref,
                      s_scr, x_scr, acc_scr,
                      *, qb, tk, n_keys, topk, causal, n_valid_q, index_bits):
    n_kt = pl.program_id(1) * (qb // tk) + (qb // tk) if causal else ki_ref.shape[0] // tk
    lim = lim_ref[0]
    lane = lax.broadcasted_iota(jnp.int32, (qb, LANES), 1)
    key_iota = lax.broadcasted_iota(jnp.int32, (tk, qb), 0)

    def head_operand(ref, hd, scale, shift_to_low):
        pair = ref[:, (hd // 2) * LANES:(hd // 2 + 1) * LANES] * scale
        if shift_to_low and hd % 2:
            pair = pltpu.roll(pair, 64, 1)
        keep = (lane < 64) if (shift_to_low or hd % 2 == 0) else (lane >= 64)
        return jnp.where(keep, pair, 0.0).T.astype(BF16)

    qi_heads = [head_operand(qi_ref, hd, 1.0, True) for hd in range(IDX_HEADS)]
    wt = wt_ref[0]

    def score_body(kt, carry):
        r0 = pl.multiple_of(kt * tk, tk)
        keys = ki_ref[pl.ds(r0, tk), :]
        acc = jnp.zeros((tk, qb), F32)
        for hd in range(IDX_HEADS):
            d = jnp.dot(keys, qi_heads[hd], preferred_element_type=F32)
            acc = acc + jnp.maximum(d, 0.0) * wt[hd:hd + 1, :]
        s_scr[pl.ds(r0, tk), :] = jnp.where(key_iota + r0 < lim, acc, -jnp.inf)
        return carry

    lax.fori_loop(0, n_kt, score_body, 0)

    def count(indicator):
        def body(kt, acc):
            for part in range(tk // COUNT_ROWS):
                r0 = pl.multiple_of(kt * tk + part * COUNT_ROWS, COUNT_ROWS)
                acc = acc + indicator(s_scr[pl.ds(r0, COUNT_ROWS), :], r0)
            return acc
        acc = lax.fori_loop(0, n_kt, body, jnp.zeros((COUNT_ROWS, qb), F32))
        return jnp.sum(acc, axis=0, keepdims=True)

    n_masked = (n_keys - lim).astype(F32)
    kf = float(topk)

    def bisect_body(it, carry):
        cur, cnt = carry
        cand = cur | lax.shift_left(jnp.int32(1), 31 - it)
        theta = _ordered_bits_to_float(cand)
        c = count(lambda s, r0: jnp.where(s >= theta, 1.0, 0.0))
        ok = c + jnp.where(theta <= NEG, n_masked, 0.0) >= kf
        return jnp.where(ok, cand, cur), jnp.where(ok, c, cnt)

    cur, cnt_ge = lax.fori_loop(0, 32, bisect_body,
                                (jnp.zeros((1, qb), jnp.int32), jnp.zeros((1, qb), F32)))
    theta = _ordered_bits_to_float(cur)

    q_lane = lax.broadcasted_iota(jnp.int32, (1, qb), 1)
    overflow = jnp.where((cnt_ge > kf) & (q_lane < n_valid_q), 1.0, 0.0)
    all_idx = jnp.full((1, qb), 2 ** index_bits - 1, jnp.int32)

    def tie_cut():
        need = kf - count(lambda s, r0: jnp.where(s > theta, 1.0, 0.0))
        rows = lax.broadcasted_iota(jnp.int32, (COUNT_ROWS, qb), 0)

        def body(it, cut):
            cand = cut | lax.shift_left(jnp.int32(1), index_bits - 1 - it)
            c = count(lambda s, r0: jnp.where(s == theta, jnp.where(rows + r0 < cand, 1.0, 0.0), 0.0))
            return jnp.where(c <= need, cand, cut)

        return lax.fori_loop(0, index_bits, body, jnp.zeros((1, qb), jnp.int32))

    idx_cut = lax.cond(jnp.max(overflow) > 0.0, tie_cut, lambda: all_idx)

    def bias_body(kt, carry):
        r0 = pl.multiple_of(kt * tk, tk)
        s = s_scr[pl.ds(r0, tk), :]
        tie = jnp.where(key_iota + r0 < idx_cut, 0.0, MASKED)
        s_scr[pl.ds(r0, tk), :] = jnp.where(s > theta, 0.0, jnp.where(s == theta, tie, MASKED))
        return carry

    lax.fori_loop(0, n_kt, bias_body, 0)

    qa_heads = [head_operand(qa_ref, hd, ATT_DH ** -0.5 * LOG2_E, False) for hd in range(ATT_HEADS)]
    acc_scr[...] = jnp.zeros(acc_scr.shape, F32)

    def logits_stage(kt, slot):
        r0 = pl.multiple_of(kt * tk, tk)
        tops = []
        for hd in range(ATT_HEADS):
            keys = ka_ref[pl.ds(r0, tk), (hd // 2) * LANES:(hd // 2 + 1) * LANES]
            x = jnp.dot(keys, qa_heads[hd], preferred_element_type=F32) + s_scr[pl.ds(r0, tk), :]
            x_scr[slot, hd] = x
            tops.append(jnp.max(x, axis=0, keepdims=True))
        return jnp.concatenate(tops, axis=0)

    def attend_body(kt, carry):
        tile_max, m_run, l_run = carry
        slot = kt % 2
        next_max = logits_stage(jnp.minimum(kt + 1, n_kt - 1), 1 - slot)
        m_new = jnp.maximum(m_run, tile_max)
        alpha = jnp.exp2(m_run - m_new)
        sums = []
        for hd in range(ATT_HEADS):
            p = jnp.exp2(x_scr[slot, hd] - m_new[hd:hd + 1, :])
            sums.append(jnp.sum(p, axis=0, keepdims=True))
            rows = slice(hd * ATT_DH, (hd + 1) * ATT_DH)
            pv = jnp.dot(vt_ref[kt, rows, :], p.astype(BF16), preferred_element_type=F32)
            acc_scr[rows, :] = alpha[hd:hd + 1, :] * acc_scr[rows, :] + pv
        return next_max, m_new, alpha * l_run + jnp.concatenate(sums, axis=0)

    first_max = logits_stage(0, 0)
    _, _, l_fin = lax.fori_loop(
        0, n_kt, attend_body,
        (first_max, jnp.full((ATT_HEADS, qb), NEG, F32), jnp.zeros((ATT_HEADS, qb), F32)))

    for hd in range(ATT_HEADS):
        rows = slice(hd * ATT_DH, (hd + 1) * ATT_DH)
        acc_scr[rows, :] = acc_scr[rows, :] / l_fin[hd:hd + 1, :]
    o_ref[...] = acc_scr[...].T.astype(o_ref.dtype)


def _attention(q_a, q_i, w_t, lim, k_i, k_a, v_t, *, qb, tk, n_keys, topk, causal, n_valid_q):
    batch, tq, _ = q_a.shape
    lp = k_i.shape[1]
    nq = tq // qb
    index_bits = int(np.ceil(np.log2(lp + 1)))
    kernel = functools.partial(_attention_kernel, qb=qb, tk=tk, n_keys=n_keys, topk=topk, causal=causal,
                               n_valid_q=n_valid_q, index_bits=index_bits)
    return pl.pallas_call(
        kernel,
        grid=(batch, nq),
        in_specs=[
            pl.BlockSpec((1, 1, qb), lambda b, i: (i, 0, 0)),
            pl.BlockSpec((None, qb, ATT_WIDTH), lambda b, i: (b, i, 0)),
            pl.BlockSpec((None, qb, IDX_HEADS * IDX_DH), lambda b, i: (b, i, 0)),
            pl.BlockSpec((1, IDX_HEADS, qb), lambda b, i: (b, 0, i)),
            _resident((None, lp, LANES), lambda b, i: (b, 0, 0)),
            _resident((None, lp, ATT_WIDTH), lambda b, i: (b, 0, 0)),
            _resident((None, lp // tk, ATT_WIDTH, tk), lambda b, i: (b, 0, 0, 0)),
        ],
        out_specs=pl.BlockSpec((None, qb, ATT_WIDTH), lambda b, i: (b, i, 0)),
        out_shape=jax.ShapeDtypeStruct((batch, tq, ATT_WIDTH), BF16),
        scratch_shapes=[
            pltpu.VMEM((lp, qb), F32),
            pltpu.VMEM((2, ATT_HEADS, tk, qb), F32),
            pltpu.VMEM((ATT_WIDTH, qb), F32),
        ],
        compiler_params=_compiler_params(("parallel", "arbitrary")),
        name="attention",
    )(lim, q_a, q_i, w_t, k_i, k_a, v_t)


FF_CHUNK = 512


def _mlp_kernel(x_ref, yr_ref, ya_ref, wo_ref, g2_ref, wup_ref, wdn_ref, gf_ref, y_ref, x_scr, h_scr, a_scr):
    x = x_ref[...] + jnp.dot(yr_ref[...], wo_ref[:RET_WIDTH, :], preferred_element_type=F32)
    x = x + jnp.dot(ya_ref[...], wo_ref[RET_WIDTH:, :], preferred_element_type=F32)
    x_scr[...] = x
    ms = jnp.mean(x * x, axis=-1, keepdims=True)
    h_scr[...] = ((x * lax.rsqrt(ms + NORM_EPS)) * g2_ref[...]).astype(h_scr.dtype)
    for c in range(D_FF // FF_CHUNK):
        cols = slice(c * FF_CHUNK, (c + 1) * FF_CHUNK)
        u = jnp.dot(h_scr[...], wup_ref[:, cols], preferred_element_type=F32)
        a_scr[:, cols] = jnp.square(jnp.maximum(u, 0.0)).astype(a_scr.dtype)
    x = x_scr[...] + jnp.dot(a_scr[...], wdn_ref[...], preferred_element_type=F32)
    ms = jnp.mean(x * x, axis=-1, keepdims=True)
    y_ref[...] = (x * lax.rsqrt(ms + NORM_EPS)) * gf_ref[...]


def _mlp(x2d, y_r, y_a, w_out_b, norm2, w_up_b, w_down_b, norm_final, tm):
    m = x2d.shape[0]
    row = lambda i: (i, 0)
    const = lambda i: (0, 0)
    return pl.pallas_call(
        _mlp_kernel,
        grid=(m // tm,),
        in_specs=[
            pl.BlockSpec((tm, D_MODEL), row),
            pl.BlockSpec((tm, RET_WIDTH), row),
            pl.BlockSpec((tm, ATT_WIDTH), row),
            _resident((D_MODEL, D_MODEL), const),
            _resident((1, D_MODEL), const),
            _resident((D_MODEL, D_FF), const),
            _resident((D_FF, D_MODEL), const),
            _resident((1, D_MODEL), const),
        ],
        out_specs=pl.BlockSpec((tm, D_MODEL), row),
        out_shape=jax.ShapeDtypeStruct((m, D_MODEL), F32),
        scratch_shapes=[pltpu.VMEM((tm, D_MODEL), F32), pltpu.VMEM((tm, D_MODEL), BF16),
                        pltpu.VMEM((tm, D_FF), BF16)],
        compiler_params=_compiler_params(("parallel",)),
        name="mlp",
    )(x2d, y_r, y_a, w_out_b, norm2, w_up_b, w_down_b, norm_final)


def _pick_tile(n, target):
    t = min(n, target)
    while n % t:
        t //= 2
    return t


def _layer(x, pos, past, s0, ret_blk, weights, *, attn_qb, attn_tk):
    norm1, w_in_b, gn_gain, w_out_b, norm2, w_up_b, w_down_b, norm_final = weights
    batch, seq, _ = x.shape
    m = batch * seq
    x2d = x.reshape(m, D_MODEL)
    tm = _pick_tile(m, 512)

    tabs = _rope_tables(pos)
    if seq < tm:
        tabs = jnp.tile(tabs, (1, tm // seq, 1))
    ret_in, q_a, k_a, v_a, q_i, kw = _proj(x2d, norm1, w_in_b, tabs, tm)

    y_r, s_new = _retention(ret_in, s0, gn_gain, batch, seq, ret_blk)

    three = lambda a: a.reshape(batch, seq, a.shape[-1])
    k_i = three(kw[:, :IDX_DH])
    w_t = jnp.swapaxes(three(kw[:, IDX_DH:IDX_DH + IDX_HEADS]) * IDX_HEADS ** -0.5 * IDX_DH ** -0.5, 1, 2)
    q_a3, q_i3 = three(q_a), three(q_i)
    if past is None:
        n_keys = seq
        keys_i, keys_a, vals = three(kw), three(k_a), three(v_a)
        qpos = pos
        n_valid_q = attn_qb
    else:
        past_k, past_v, past_ki = past
        n_keys = past_k.shape[1] + seq
        pad_i = jnp.pad(past_ki, ((0, 0), (0, 0), (0, LANES - IDX_DH)))
        keys_i = jnp.concatenate([pad_i, three(kw)], axis=1)
        keys_a = jnp.concatenate([past_k.reshape(batch, -1, ATT_WIDTH), three(k_a)], axis=1)
        vals = jnp.concatenate([past_v.reshape(batch, -1, ATT_WIDTH), three(v_a)], axis=1)
        qpos = pos
        n_valid_q = seq
    topk = min(TOPK_MAX, n_keys // 4)
    tq = -(-seq // attn_qb) * attn_qb
    lp = -(-n_keys // attn_tk) * attn_tk
    padq = lambda a: jnp.pad(a, ((0, 0), (0, tq - seq), (0, 0)))
    padk = lambda a: jnp.pad(a, ((0, 0), (0, lp - n_keys), (0, 0)))
    lim = jnp.minimum((qpos // CHUNK + 1) * CHUNK, n_keys).astype(jnp.int32)
    lim = jnp.pad(lim, (0, tq - seq), constant_values=n_keys).reshape(tq // attn_qb, 1, attn_qb)
    v_t = jnp.swapaxes(padk(vals).astype(BF16).reshape(batch, lp // attn_tk, attn_tk, ATT_WIDTH), 2, 3)
    y_a = _attention(
        padq(q_a3), padq(q_i3), jnp.pad(w_t, ((0, 0), (0, 0), (0, tq - seq))), lim,
        padk(keys_i).astype(BF16), padk(keys_a).astype(BF16), v_t,
        qb=attn_qb, tk=attn_tk, n_keys=n_keys, topk=topk, causal=past is None, n_valid_q=n_valid_q)
    y_a = y_a[:, :seq].reshape(m, ATT_WIDTH)

    y = _mlp(x2d, y_r, y_a, w_out_b, norm2, w_up_b, w_down_b, norm_final, tm)
    return (y.reshape(batch, seq, D_MODEL),
            k_a.reshape(batch, seq, ATT_HEADS, ATT_DH),
            v_a.reshape(batch, seq, ATT_HEADS, ATT_DH),
            k_i, s_new)


def kernel(x_prompt, x_sample, cache_k_att, cache_v_att, cache_k_idx, state_ret, norm1, w_in, gn_gain, w_out,
           norm2, w_up, w_down, norm_final):
    depth = norm1.shape[0]
    assert depth == 1, "the final norm is fused into the layer's last kernel"
    past_len = cache_k_att.shape[2]
    pos_p = jnp.arange(x_prompt.shape[1], dtype=jnp.int32)
    pos_s = past_len + jnp.arange(x_sample.shape[1], dtype=jnp.int32)
    l = 0
    w_in_b = jnp.pad(w_in[l], ((0, 0), (0, IN_WIDTH_PADDED - IN_WIDTH))).astype(BF16)
    weights = (norm1[l][None], w_in_b, gn_gain[l][None], w_out[l].astype(BF16), norm2[l][None],
               w_up[l].astype(BF16), w_down[l].astype(BF16), norm_final[None])
    s0_p = jnp.zeros((x_prompt.shape[0], RET_HEADS, RET_DK, RET_DV), F32)
    yp, kap, vap, kip, sp = _layer(x_prompt, pos_p, None, s0_p, 256, weights, attn_qb=256, attn_tk=256)
    past = (cache_k_att[l], cache_v_att[l], cache_k_idx[l])
    ys, kas, vas, kis, ss = _layer(x_sample, pos_s, past, state_ret[l].astype(F32), x_sample.shape[1], weights,
                                   attn_qb=128, attn_tk=256)
    return (yp, ys, kap[None], vap[None], kip[None], sp[None], kas[None], vas[None], kis[None], ss[None])
```

```python
import functools

import jax
import jax.numpy as jnp
import numpy as np
from jax import lax
from jax.experimental import pallas as pl
from jax.experimental.pallas import tpu as pltpu

D_MODEL = 1024
CHUNK = 64
ROPE_THETA = 10000.0
NORM_EPS = 1e-6
NEG = -1e30
RET_WIDTH = 512
ATT_WIDTH = 512
RET_HEADS = 4
RET_DK = 128
RET_DV = 128
ATT_HEADS = 8
ATT_DH = 64
IDX_HEADS = 8
IDX_DH = 64
TOPK_MAX = 256
D_FF = 4 * D_MODEL
IN_WIDTH = 4 * RET_WIDTH + 3 * ATT_WIDTH + IDX_HEADS * IDX_DH + IDX_DH + IDX_HEADS

LANES = 128
IN_WIDTH_PADDED = ((IN_WIDTH + LANES - 1) // LANES) * LANES
VMEM_LIMIT_BYTES = 56 * 1024 * 1024

F32 = jnp.float32
BF16 = jnp.bfloat16

_NT = (((1,), (1,)), ((), ()))


def _compiler_params(semantics):
    return pltpu.CompilerParams(dimension_semantics=semantics, vmem_limit_bytes=VMEM_LIMIT_BYTES)


def _resident(block_shape, index_map):
    return pl.BlockSpec(block_shape, index_map, pipeline_mode=pl.Buffered(1))


def _rope_tables(pos):
    posf = pos.astype(F32)[:, None]

    def angles(d):
        half = d // 2
        inv = ROPE_THETA ** (-jnp.arange(half, dtype=F32) * 2.0 / d)
        ang = posf * inv[None, :]
        return jnp.cos(ang), jnp.sin(ang)

    c128, s128 = angles(128)
    zero64 = jnp.zeros_like(s128)
    cos128 = jnp.concatenate([c128, c128], axis=1)
    sin128 = jnp.concatenate([-s128, s128], axis=1)
    c64, s64 = angles(64)
    zero32 = jnp.zeros_like(s64)
    cos64 = jnp.concatenate([c64, c64, c64, c64], axis=1)
    sin64_up = jnp.concatenate([zero32, s64, zero32, s64], axis=1)
    sin64_dn = jnp.concatenate([-s64, zero32, -s64, zero32], axis=1)
    one64 = jnp.ones_like(zero64)
    cosk = jnp.concatenate([c64, c64, one64], axis=1)
    sink_up = jnp.concatenate([zero32, s64, zero64], axis=1)
    sink_dn = jnp.concatenate([-s64, zero32, zero64], axis=1)
    return jnp.stack([cos128, sin128, cos64, sin64_up, sin64_dn, cosk, sink_up, sink_dn])


def _proj_kernel(x_ref, g_ref, w_ref, tab_ref, ret_ref, qa_ref, ka_ref, va_ref, qi_ref, kw_ref):
    x = x_ref[...]
    ms = jnp.mean(x * x, axis=-1, keepdims=True)
    h = ((x * lax.rsqrt(ms + NORM_EPS)) * g_ref[...]).astype(BF16)

    def mm(c0, n):
        return jnp.dot(h, w_ref[:, c0:c0 + n], preferred_element_type=F32)

    def rope128(z):
        return z * tab_ref[0] + pltpu.roll(z, 64, 1) * tab_ref[1]

    def rope64(z):
        return z * tab_ref[2] + pltpu.roll(z, 32, 1) * tab_ref[3] + pltpu.roll(z, 96, 1) * tab_ref[4]

    def ropek(z):
        return z * tab_ref[5] + pltpu.roll(z, 32, 1) * tab_ref[6] + pltpu.roll(z, 96, 1) * tab_ref[7]

    def emit(out_ref, out_c0, w_c0, rope):
        z = mm(w_c0, 512)
        for t in range(4):
            zt = z[:, t * LANES:(t + 1) * LANES]
            if rope is not None:
                zt = rope(zt)
            out_ref[:, out_c0 + t * LANES:out_c0 + (t + 1) * LANES] = zt.astype(out_ref.dtype)

    emit(ret_ref, 0, 0, rope128)
    emit(ret_ref, 512, 512, rope128)
    emit(ret_ref, 1024, 1024, None)
    emit(ret_ref, 1536, 1536, None)
    emit(qa_ref, 0, 2048, rope64)
    emit(ka_ref, 0, 2560, rope64)
    emit(va_ref, 0, 3072, None)
    emit(qi_ref, 0, 3584, rope64)
    kw_ref[...] = ropek(mm(4096, LANES))


def _proj(x2d, norm_g, w_in_b, tabs, tm):
    m = x2d.shape[0]
    n_tab = tabs.shape[1] // tm
    row = lambda i: (i, 0)
    outs = (
        jax.ShapeDtypeStruct((m, 4 * RET_WIDTH), F32),
        jax.ShapeDtypeStruct((m, ATT_WIDTH), F32),
        jax.ShapeDtypeStruct((m, ATT_WIDTH), F32),
        jax.ShapeDtypeStruct((m, ATT_WIDTH), F32),
        jax.ShapeDtypeStruct((m, IDX_HEADS * IDX_DH), F32),
        jax.ShapeDtypeStruct((m, LANES), F32),
    )
    return pl.pallas_call(
        _proj_kernel,
        grid=(m // tm,),
        in_specs=[
            pl.BlockSpec((tm, D_MODEL), row),
            _resident((1, D_MODEL), lambda i: (0, 0)),
            _resident((D_MODEL, IN_WIDTH_PADDED), lambda i: (0, 0)),
            pl.BlockSpec((8, tm, LANES), lambda i: (0, i % n_tab, 0)),
        ],
        out_specs=[
            pl.BlockSpec((tm, 4 * RET_WIDTH), row),
            pl.BlockSpec((tm, ATT_WIDTH), row),
            pl.BlockSpec((tm, ATT_WIDTH), row),
            pl.BlockSpec((tm, ATT_WIDTH), row),
            pl.BlockSpec((tm, IDX_HEADS * IDX_DH), row),
            pl.BlockSpec((tm, LANES), row),
        ],
        out_shape=outs,
        compiler_params=_compiler_params(("parallel",)),
        name="proj",
    )(x2d, norm_g, w_in_b, tabs)


def _retention_tables(blk):
    lg = jnp.log1p(-jnp.exp2(-5.0 - jnp.arange(RET_HEADS, dtype=F32)))
    i = jnp.arange(blk, dtype=F32)
    diff = i[:, None] - i[None, :]
    dmat = jnp.where(diff[None] >= 0, jnp.exp(lg[:, None, None] * jnp.maximum(diff, 0.0)[None]), 0.0)
    w_end = jnp.exp(lg[:, None] * (blk - 1.0 - i)[None, :])
    w_q = jnp.exp(lg[:, None] * (i + 1.0)[None, :])
    g_block = jnp.exp(lg * blk)
    lane = lambda a: jnp.broadcast_to(a[:, :, None], (RET_HEADS, blk, LANES))
    return dmat, lane(w_q), lane(w_end), g_block


def _retention_kernel(gblk_ref, q_ref, k_ref, v_ref, g_ref, s0_ref, dmat_ref, wq_ref, wend_ref, gain_ref,
                      y_ref, sfin_ref, s_scr):
    t = pl.program_id(1)

    @pl.when(t == 0)
    def _():
        s_scr[...] = s0_ref[0]

    for hd in range(RET_HEADS):
        cols = slice(hd * RET_DK, (hd + 1) * RET_DK)
        q = q_ref[:, cols]
        k = k_ref[:, cols] * RET_DK ** -0.5
        vb = v_ref[:, cols].astype(BF16)
        state = s_scr[hd]
        scores = lax.dot_general(q.astype(BF16), k.astype(BF16), _NT, preferred_element_type=F32) * dmat_ref[hd]
        o = jnp.dot(scores.astype(BF16), vb, preferred_element_type=F32)
        o = o + jnp.dot((q * wq_ref[hd]).astype(BF16), state.astype(BF16), preferred_element_type=F32)
        k_end_t = (k * wend_ref[hd]).T.astype(BF16)
        s_scr[hd] = gblk_ref[hd] * state + jnp.dot(k_end_t, vb, preferred_element_type=F32)
        mu = jnp.mean(o, axis=-1, keepdims=True)
        var = jnp.mean(jnp.square(o - mu), axis=-1, keepdims=True)
        on = ((o - mu) * lax.rsqrt(var + NORM_EPS)) * gain_ref[:, cols]
        g = g_ref[:, cols]
        y_ref[:, cols] = ((g * jax.nn.sigmoid(g)) * on).astype(y_ref.dtype)

    @pl.when(t == pl.num_programs(1) - 1)
    def _():
        sfin_ref[0] = s_scr[...]


def _retention(ret_in, s0, gn_gain, batch, seq, blk):
    nt = seq // blk
    dmat, w_q, w_end, g_block = _retention_tables(blk)

    def col(c):
        return pl.BlockSpec((blk, RET_WIDTH), lambda b, t: (b * nt + t, c))

    const3 = lambda b, t: (0, 0, 0)
    return pl.pallas_call(
        _retention_kernel,
        grid=(batch, nt),
        in_specs=[
            pl.BlockSpec(memory_space=pltpu.SMEM),
            col(0), col(1), col(2), col(3),
            pl.BlockSpec((1, RET_HEADS, RET_DK, RET_DV), lambda b, t: (b, 0, 0, 0)),
            pl.BlockSpec((RET_HEADS, blk, blk), const3),
            pl.BlockSpec((RET_HEADS, blk, LANES), const3),
            pl.BlockSpec((RET_HEADS, blk, LANES), const3),
            pl.BlockSpec((1, RET_WIDTH), lambda b, t: (0, 0)),
        ],
        out_specs=[
            pl.BlockSpec((blk, RET_WIDTH), lambda b, t: (b * nt + t, 0)),
            pl.BlockSpec((1, RET_HEADS, RET_DK, RET_DV), lambda b, t: (b, 0, 0, 0)),
        ],
        out_shape=(
            jax.ShapeDtypeStruct((batch * seq, RET_WIDTH), BF16),
            jax.ShapeDtypeStruct((batch, RET_HEADS, RET_DK, RET_DV), F32),
        ),
        scratch_shapes=[pltpu.VMEM((RET_HEADS, RET_DK, RET_DV), F32)],
        compiler_params=_compiler_params(("parallel", "arbitrary")),
        name="retention",
    )(g_block, ret_in, ret_in, ret_in, ret_in, s0, dmat, w_q, w_end, gn_gain)


COUNT_ROWS = 32
MASKED = 2.0 * NEG
LOG2_E = 1.4426950408889634


def _ordered_bits_to_float(u):
    bits = jnp.where(u < 0, u ^ jnp.int32(-2 ** 31), ~u)
    return lax.bitcast_convert_type(bits, F32)


def _attention_kernel(lim_ref, qa_ref, qi_ref, wt_ref, ki_ref, ka_ref, vt_ref, o_ref,
                      s_scr, xa_scr, xb_scr, acc_scr,
                      *, qb, tk, n_keys, topk, causal, n_valid_q, index_bits):
    n_kt = pl.program_id(1) * (qb // tk) + (qb // tk) if causal else ki_ref.shape[0] // tk
    lim = lim_ref[0]
    lane = lax.broadcasted_iota(jnp.int32, (qb, LANES), 1)
    key_iota = lax.broadcasted_iota(jnp.int32, (tk, qb), 0)

    def head_operand(ref, hd, scale, shift_to_low):
        pair = ref[:, (hd // 2) * LANES:(hd // 2 + 1) * LANES] * scale
        if shift_to_low and hd % 2:
            pair = pltpu.roll(pair, 64, 1)
        keep = (lane < 64) if (shift_to_low or hd % 2 == 0) else (lane >= 64)
        return jnp.where(keep, pair, 0.0).T.astype(BF16)

    qi_heads = [head_operand(qi_ref, hd, 1.0, True) for hd in range(IDX_HEADS)]
    wt = wt_ref[0]

    def score_body(kt, carry):
        r0 = pl.multiple_of(kt * tk, tk)
        keys = ki_ref[pl.ds(r0, tk), :]
        acc = jnp.zeros((tk, qb), F32)
        for hd in range(IDX_HEADS):
            d = jnp.dot(keys, qi_heads[hd], preferred_element_type=F32)
            acc = acc + jnp.maximum(d, 0.0) * wt[hd:hd + 1, :]
        s_scr[pl.ds(r0, tk), :] = jnp.where(key_iota + r0 < lim, acc, -jnp.inf)
        return carry

    lax.fori_loop(0, n_kt, score_body, 0)

    def count(indicator):
        def body(kt, acc):
            for part in range(tk // COUNT_ROWS):
                r0 = pl.multiple_of(kt * tk + part * COUNT_ROWS, COUNT_ROWS)
                acc = acc + indicator(s_scr[pl.ds(r0, COUNT_ROWS), :], r0)
            return acc
        acc = lax.fori_loop(0, n_kt, body, jnp.zeros((COUNT_ROWS, qb), F32))
        return jnp.sum(acc, axis=0, keepdims=True)

    n_masked = (n_keys - lim).astype(F32)
    kf = float(topk)

    def bisect_body(it, carry):
        cur, cnt = carry
        cand = cur | lax.shift_left(jnp.int32(1), 31 - it)
        theta = _ordered_bits_to_float(cand)
        c = count(lambda s, r0: jnp.where(s >= theta, 1.0, 0.0))
        ok = c + jnp.where(theta <= NEG, n_masked, 0.0) >= kf
        return jnp.where(ok, cand, cur), jnp.where(ok, c, cnt)

    cur, cnt_ge = lax.fori_loop(0, 32, bisect_body,
                                (jnp.zeros((1, qb), jnp.int32), jnp.zeros((1, qb), F32)))
    theta = _ordered_bits_to_float(cur)

    q_lane = lax.broadcasted_iota(jnp.int32, (1, qb), 1)
    overflow = jnp.where((cnt_ge > kf) & (q_lane < n_valid_q), 1.0, 0.0)
    all_idx = jnp.full((1, qb), 2 ** index_bits - 1, jnp.int32)

    def tie_cut():
        need = kf - count(lambda s, r0: jnp.where(s > theta, 1.0, 0.0))
        rows = lax.broadcasted_iota(jnp.int32, (COUNT_ROWS, qb), 0)

        def body(it, cut):
            cand = cut | lax.shift_left(jnp.int32(1), index_bits - 1 - it)
            c = count(lambda s, r0: jnp.where(s == theta, jnp.where(rows + r0 < cand, 1.0, 0.0), 0.0))
            return jnp.where(c <= need, cand, cut)

        return lax.fori_loop(0, index_bits, body, jnp.zeros((1, qb), jnp.int32))

    idx_cut = lax.cond(jnp.max(overflow) > 0.0, tie_cut, lambda: all_idx)

    def bias_body(kt, carry):
        r0 = pl.multiple_of(kt * tk, tk)
        s = s_scr[pl.ds(r0, tk), :]
        tie = jnp.where(key_iota + r0 < idx_cut, 0.0, MASKED)
        s_scr[pl.ds(r0, tk), :] = jnp.where(s > theta, 0.0, jnp.where(s == theta, tie, MASKED))
        return carry

    lax.fori_loop(0, n_kt, bias_body, 0)

    lp = ki_ref.shape[0]
    s_scr[pl.ds(lp, tk), :] = jnp.full((tk, qb), MASKED, F32)
    qa_heads = [head_operand(qa_ref, hd, ATT_DH ** -0.5 * LOG2_E, False) for hd in range(ATT_HEADS)]
    acc_scr[...] = jnp.zeros(acc_scr.shape, F32)

    def logits_stage(kt, x_ref):
        kt_in = jnp.minimum(kt, n_kt - 1)
        r0 = pl.multiple_of(kt_in * tk, tk)
        b0 = pl.multiple_of(jnp.where(kt < n_kt, kt_in * tk, lp), tk)
        tops = []
        for hd in range(ATT_HEADS):
            keys = ka_ref[pl.ds(r0, tk), (hd // 2) * LANES:(hd // 2 + 1) * LANES]
            x = jnp.dot(keys, qa_heads[hd], preferred_element_type=F32) + s_scr[pl.ds(b0, tk), :]
            x_ref[hd] = x
            tops.append(jnp.max(x, axis=0, keepdims=True))
        return jnp.concatenate(tops, axis=0)

    def accumulate(kt, x_ref, tile_max, m_run, l_run):
        kt_in = jnp.minimum(kt, n_kt - 1)
        m_new = jnp.maximum(m_run, tile_max)
        alpha = jnp.exp2(m_run - m_new)
        sums = []
        for hd in range(ATT_HEADS):
            p = jnp.exp2(x_ref[hd] - m_new[hd:hd + 1, :])
            sums.append(jnp.sum(p, axis=0, keepdims=True))
            rows = slice(hd * ATT_DH, (hd + 1) * ATT_DH)
            pv = jnp.dot(vt_ref[kt_in, rows, :], p.astype(BF16), preferred_element_type=F32)
            acc_scr[rows, :] = alpha[hd:hd + 1, :] * acc_scr[rows, :] + pv
        return m_new, alpha * l_run + jnp.concatenate(sums, axis=0)

    def pair_body(pair, carry):
        max_a, m_run, l_run = carry
        max_b = logits_stage(2 * pair + 1, xb_scr)
        m_run, l_run = accumulate(2 * pair, xa_scr, max_a, m_run, l_run)
        max_a = logits_stage(2 * pair + 2, xa_scr)
        m_run, l_run = accumulate(2 * pair + 1, xb_scr, max_b, m_run, l_run)
        return max_a, m_run, l_run

    first_max = logits_stage(0, xa_scr)
    _, _, l_fin = lax.fori_loop(
        0, (n_kt + 1) // 2, pair_body,
        (first_max, jnp.full((ATT_HEADS, qb), NEG, F32), jnp.zeros((ATT_HEADS, qb), F32)))

    for hd in range(ATT_HEADS):
        rows = slice(hd * ATT_DH, (hd + 1) * ATT_DH)
        acc_scr[rows, :] = acc_scr[rows, :] / l_fin[hd:hd + 1, :]
    o_ref[...] = acc_scr[...].T.astype(o_ref.dtype)


def _attention(q_a, q_i, w_t, lim, k_i, k_a, v_t, *, qb, tk, n_keys, topk, causal, n_valid_q):
    batch, tq, _ = q_a.shape
    lp = k_i.shape[1]
    nq = tq // qb
    index_bits = int(np.ceil(np.log2(lp + 1)))
    kernel = functools.partial(_attention_kernel, qb=qb, tk=tk, n_keys=n_keys, topk=topk, causal=causal,
                               n_valid_q=n_valid_q, index_bits=index_bits)
    return pl.pallas_call(
        kernel,
        grid=(batch, nq),
        in_specs=[
            pl.BlockSpec((1, 1, qb), lambda b, i: (i, 0, 0)),
            pl.BlockSpec((None, qb, ATT_WIDTH), lambda b, i: (b, i, 0)),
            pl.BlockSpec((None, qb, IDX_HEADS * IDX_DH), lambda b, i: (b, i, 0)),
            pl.BlockSpec((1, IDX_HEADS, qb), lambda b, i: (b, 0, i)),
            _resident((None, lp, LANES), lambda b, i: (b, 0, 0)),
            _resident((None, lp, ATT_WIDTH), lambda b, i: (b, 0, 0)),
            _resident((None, lp // tk, ATT_WIDTH, tk), lambda b, i: (b, 0, 0, 0)),
        ],
        out_specs=pl.BlockSpec((None, qb, ATT_WIDTH), lambda b, i: (b, i, 0)),
        out_shape=jax.ShapeDtypeStruct((batch, tq, ATT_WIDTH), BF16),
        scratch_shapes=[
            pltpu.VMEM((lp + tk, qb), F32),
            pltpu.VMEM((ATT_HEADS, tk, qb), F32),
            pltpu.VMEM((ATT_HEADS, tk, qb), F32),
            pltpu.VMEM((ATT_WIDTH, qb), F32),
        ],
        compiler_params=_compiler_params(("parallel", "arbitrary")),
        name="attention",
    )(lim, q_a, q_i, w_t, k_i, k_a, v_t)


FF_CHUNK = 512


def _mlp_kernel(x_ref, yr_ref, ya_ref, wo_ref, g2_ref, wup_ref, wdn_ref, gf_ref, y_ref, x_scr, h_scr, a_scr):
    x = x_ref[...] + jnp.dot(yr_ref[...], wo_ref[:RET_WIDTH, :], preferred_element_type=F32)
    x = x + jnp.dot(ya_ref[...], wo_ref[RET_WIDTH:, :], preferred_element_type=F32)
    x_scr[...] = x
    ms = jnp.mean(x * x, axis=-1, keepdims=True)
    h_scr[...] = ((x * lax.rsqrt(ms + NORM_EPS)) * g2_ref[...]).astype(h_scr.dtype)
    for c in range(D_FF // FF_CHUNK):
        cols = slice(c * FF_CHUNK, (c + 1) * FF_CHUNK)
        u = jnp.dot(h_scr[...], wup_ref[:, cols], preferred_element_type=F32)
        a_scr[:, cols] = jnp.square(jnp.maximum(u, 0.0)).astype(a_scr.dtype)
    x = x_scr[...] + jnp.dot(a_scr[...], wdn_ref[...], preferred_element_type=F32)
    ms = jnp.mean(x * x, axis=-1, keepdims=True)
    y_ref[...] = (x * lax.rsqrt(ms + NORM_EPS)) * gf_ref[...]


def _mlp(x2d, y_r, y_a, w_out_b, norm2, w_up_b, w_down_b, norm_final, tm):
    m = x2d.shape[0]
    row = lambda i: (i, 0)
    const = lambda i: (0, 0)
    return pl.pallas_call(
        _mlp_kernel,
        grid=(m // tm,),
        in_specs=[
            pl.BlockSpec((tm, D_MODEL), row),
            pl.BlockSpec((tm, RET_WIDTH), row),
            pl.BlockSpec((tm, ATT_WIDTH), row),
            _resident((D_MODEL, D_MODEL), const),
            _resident((1, D_MODEL), const),
            _resident((D_MODEL, D_FF), const),
            _resident((D_FF, D_MODEL), const),
            _resident((1, D_MODEL), const),
        ],
        out_specs=pl.BlockSpec((tm, D_MODEL), row),
        out_shape=jax.ShapeDtypeStruct((m, D_MODEL), F32),
        scratch_shapes=[pltpu.VMEM((tm, D_MODEL), F32), pltpu.VMEM((tm, D_MODEL), BF16),
                        pltpu.VMEM((tm, D_FF), BF16)],
        compiler_params=_compiler_params(("parallel",)),
        name="mlp",
    )(x2d, y_r, y_a, w_out_b, norm2, w_up_b, w_down_b, norm_final)


def _pick_tile(n, target):
    t = min(n, target)
    while n % t:
        t //= 2
    return t


def _layer(x, pos, past, s0, ret_blk, weights, *, attn_qb, attn_tk):
    norm1, w_in_b, gn_gain, w_out_b, norm2, w_up_b, w_down_b, norm_final = weights
    batch, seq, _ = x.shape
    m = batch * seq
    x2d = x.reshape(m, D_MODEL)
    tm = _pick_tile(m, 512)

    tabs = _rope_tables(pos)
    if seq < tm:
        tabs = jnp.tile(tabs, (1, tm // seq, 1))
    ret_in, q_a, k_a, v_a, q_i, kw = _proj(x2d, norm1, w_in_b, tabs, tm)

    y_r, s_new = _retention(ret_in, s0, gn_gain, batch, seq, ret_blk)

    three = lambda a: a.reshape(batch, seq, a.shape[-1])
    k_i = three(kw[:, :IDX_DH])
    w_t = jnp.swapaxes(three(kw[:, IDX_DH:IDX_DH + IDX_HEADS]) * IDX_HEADS ** -0.5 * IDX_DH ** -0.5, 1, 2)
    q_a3, q_i3 = three(q_a), three(q_i)
    if past is None:
        n_keys = seq
        keys_i, keys_a, vals = three(kw), three(k_a), three(v_a)
        qpos = pos
        n_valid_q = attn_qb
    else:
        past_k, past_v, past_ki = past
        n_keys = past_k.shape[1] + seq
        pad_i = jnp.pad(past_ki, ((0, 0), (0, 0), (0, LANES - IDX_DH)))
        keys_i = jnp.concatenate([pad_i, three(kw)], axis=1)
        keys_a = jnp.concatenate([past_k.reshape(batch, -1, ATT_WIDTH), three(k_a)], axis=1)
        vals = jnp.concatenate([past_v.reshape(batch, -1, ATT_WIDTH), three(v_a)], axis=1)
        qpos = pos
        n_valid_q = seq
    topk = min(TOPK_MAX, n_keys // 4)
    tq = -(-seq // attn_qb) * attn_qb
    lp = -(-n_keys // attn_tk) * attn_tk
    padq = lambda a: jnp.pad(a, ((0, 0), (0, tq - seq), (0, 0)))
    padk = lambda a: jnp.pad(a, ((0, 0), (0, lp - n_keys), (0, 0)))
    lim = jnp.minimum((qpos // CHUNK + 1) * CHUNK, n_keys).astype(jnp.int32)
    lim = jnp.pad(lim, (0, tq - seq), constant_values=n_keys).reshape(tq // attn_qb, 1, attn_qb)
    v_t = jnp.swapaxes(padk(vals).astype(BF16).reshape(batch, lp // attn_tk, attn_tk, ATT_WIDTH), 2, 3)
    y_a = _attention(
        padq(q_a3), padq(q_i3), jnp.pad(w_t, ((0, 0), (0, 0), (0, tq - seq))), lim,
        padk(keys_i).astype(BF16), padk(keys_a).astype(BF16), v_t,
        qb=attn_qb, tk=attn_tk, n_keys=n_keys, topk=topk, causal=past is None, n_valid_q=n_valid_q)
    y_a = y_a[:, :seq].reshape(m, ATT_WIDTH)

    y = _mlp(x2d, y_r, y_a, w_out_b, norm2, w_up_b, w_down_b, norm_final, tm)
    return (y.reshape(batch, seq, D_MODEL),
            k_a.reshape(batch, seq, ATT_HEADS, ATT_DH),
            v_a.reshape(batch, seq, ATT_HEADS, ATT_DH),
            k_i, s_new)


def kernel(x_prompt, x_sample, cache_k_att, cache_v_att, cache_k_idx, state_ret, norm1, w_in, gn_gain, w_out,
           norm2, w_up, w_down, norm_final):
    depth = norm1.shape[0]
    assert depth == 1, "the final norm is fused into the layer's last kernel"
    past_len = cache_k_att.shape[2]
    pos_p = jnp.arange(x_prompt.shape[1], dtype=jnp.int32)
    pos_s = past_len + jnp.arange(x_sample.shape[1], dtype=jnp.int32)
    l = 0
    w_in_b = jnp.pad(w_in[l], ((0, 0), (0, IN_WIDTH_PADDED - IN_WIDTH))).astype(BF16)
    weights = (norm1[l][None], w_in_b, gn_gain[l][None], w_out[l].astype(BF16), norm2[l][None],
               w_up[l].astype(BF16), w_down[l].astype(BF16), norm_final[None])
    s0_p = jnp.zeros((x_prompt.shape[0], RET_HEADS, RET_DK, RET_DV), F32)
    yp, kap, vap, kip, sp = _layer(x_prompt, pos_p, None, s0_p, 256, weights, attn_qb=256, attn_tk=256)
    past = (cache_k_att[l], cache_v_att[l], cache_k_idx[l])
    ys, kas, vas, kis, ss = _layer(x_sample, pos_s, past, state_ret[l].astype(F32), x_sample.shape[1], weights,
                                   attn_qb=128, attn_tk=256)
    return (yp, ys, kap[None], vap[None], kip[None], sp[None], kas[None], vas[None], kis[None], ss[None])
```

```python
import functools

import jax
import jax.numpy as jnp
import numpy as np
from jax import lax
from jax.experimental import pallas as pl
from jax.experimental.pallas import tpu as pltpu

D_MODEL = 1024
CHUNK = 64
ROPE_THETA = 10000.0
NORM_EPS = 1e-6
NEG = -1e30
RET_WIDTH = 512
ATT_WIDTH = 512
RET_HEADS = 4
RET_DK = 128
RET_DV = 128
ATT_HEADS = 8
ATT_DH = 64
IDX_HEADS = 8
IDX_DH = 64
TOPK_MAX = 256
D_FF = 4 * D_MODEL
IN_WIDTH = 4 * RET_WIDTH + 3 * ATT_WIDTH + IDX_HEADS * IDX_DH + IDX_DH + IDX_HEADS

LANES = 128
IN_WIDTH_PADDED = ((IN_WIDTH + LANES - 1) // LANES) * LANES
VMEM_LIMIT_BYTES = 56 * 1024 * 1024

F32 = jnp.float32
BF16 = jnp.bfloat16

_NT = (((1,), (1,)), ((), ()))


def _compiler_params(semantics):
    return pltpu.CompilerParams(dimension_semantics=semantics, vmem_limit_bytes=VMEM_LIMIT_BYTES)


def _resident(block_shape, index_map):
    return pl.BlockSpec(block_shape, index_map, pipeline_mode=pl.Buffered(1))


def _rope_tables(pos):
    posf = pos.astype(F32)[:, None]

    def angles(d):
        half = d // 2
        inv = ROPE_THETA ** (-jnp.arange(half, dtype=F32) * 2.0 / d)
        ang = posf * inv[None, :]
        return jnp.cos(ang), jnp.sin(ang)

    c128, s128 = angles(128)
    zero64 = jnp.zeros_like(s128)
    cos128 = jnp.concatenate([c128, c128], axis=1)
    sin128 = jnp.concatenate([-s128, s128], axis=1)
    c64, s64 = angles(64)
    zero32 = jnp.zeros_like(s64)
    cos64 = jnp.concatenate([c64, c64, c64, c64], axis=1)
    sin64_up = jnp.concatenate([zero32, s64, zero32, s64], axis=1)
    sin64_dn = jnp.concatenate([-s64, zero32, -s64, zero32], axis=1)
    one64 = jnp.ones_like(zero64)
    cosk = jnp.concatenate([c64, c64, one64], axis=1)
    sink_up = jnp.concatenate([zero32, s64, zero64], axis=1)
    sink_dn = jnp.concatenate([-s64, zero32, zero64], axis=1)
    return jnp.stack([cos128, sin128, cos64, sin64_up, sin64_dn, cosk, sink_up, sink_dn])


def _proj_kernel(x_ref, g_ref, w_ref, tab_ref, ret_ref, qa_ref, ka_ref, va_ref, qi_ref, kw_ref):
    x = x_ref[...]
    ms = jnp.mean(x * x, axis=-1, keepdims=True)
    h = ((x * lax.rsqrt(ms + NORM_EPS)) * g_ref[...]).astype(BF16)

    def mm(c0, n):
        return jnp.dot(h, w_ref[:, c0:c0 + n], preferred_element_type=F32)

    def rope128(z):
        return z * tab_ref[0] + pltpu.roll(z, 64, 1) * tab_ref[1]

    def rope64(z):
        return z * tab_ref[2] + pltpu.roll(z, 32, 1) * tab_ref[3] + pltpu.roll(z, 96, 1) * tab_ref[4]

    def ropek(z):
        return z * tab_ref[5] + pltpu.roll(z, 32, 1) * tab_ref[6] + pltpu.roll(z, 96, 1) * tab_ref[7]

    def emit(out_ref, out_c0, w_c0, rope):
        z = mm(w_c0, 512)
        for t in range(4):
            zt = z[:, t * LANES:(t + 1) * LANES]
            if rope is not None:
                zt = rope(zt)
            out_ref[:, out_c0 + t * LANES:out_c0 + (t + 1) * LANES] = zt.astype(out_ref.dtype)

    emit(ret_ref, 0, 0, rope128)
    emit(ret_ref, 512, 512, rope128)
    emit(ret_ref, 1024, 1024, None)
    emit(ret_ref, 1536, 1536, None)
    emit(qa_ref, 0, 2048, rope64)
    emit(ka_ref, 0, 2560, rope64)
    emit(va_ref, 0, 3072, None)
    emit(qi_ref, 0, 3584, rope64)
    kw_ref[...] = ropek(mm(4096, LANES))


def _proj(x2d, norm_g, w_in_b, tabs, tm):
    m = x2d.shape[0]
    n_tab = tabs.shape[1] // tm
    row = lambda i: (i, 0)
    outs = (
        jax.ShapeDtypeStruct((m, 4 * RET_WIDTH), F32),
        jax.ShapeDtypeStruct((m, ATT_WIDTH), F32),
        jax.ShapeDtypeStruct((m, ATT_WIDTH), F32),
        jax.ShapeDtypeStruct((m, ATT_WIDTH), F32),
        jax.ShapeDtypeStruct((m, IDX_HEADS * IDX_DH), F32),
        jax.ShapeDtypeStruct((m, LANES), F32),
    )
    return pl.pallas_call(
        _proj_kernel,
        grid=(m // tm,),
        in_specs=[
            pl.BlockSpec((tm, D_MODEL), row),
            _resident((1, D_MODEL), lambda i: (0, 0)),
            _resident((D_MODEL, IN_WIDTH_PADDED), lambda i: (0, 0)),
            pl.BlockSpec((8, tm, LANES), lambda i: (0, i % n_tab, 0)),
        ],
        out_specs=[
            pl.BlockSpec((tm, 4 * RET_WIDTH), row),
            pl.BlockSpec((tm, ATT_WIDTH), row),
            pl.BlockSpec((tm, ATT_WIDTH), row),
            pl.BlockSpec((tm, ATT_WIDTH), row),
            pl.BlockSpec((tm, IDX_HEADS * IDX_DH), row),
            pl.BlockSpec((tm, LANES), row),
        ],
        out_shape=outs,
        compiler_params=_compiler_params(("parallel",)),
        name="proj",
    )(x2d, norm_g, w_in_b, tabs)


def _retention_tables(blk):
    lg = jnp.log1p(-jnp.exp2(-5.0 - jnp.arange(RET_HEADS, dtype=F32)))
    i = jnp.arange(blk, dtype=F32)
    diff = i[:, None] - i[None, :]
    dmat = jnp.where(diff[None] >= 0, jnp.exp(lg[:, None, None] * jnp.maximum(diff, 0.0)[None]), 0.0)
    w_end = jnp.exp(lg[:, None] * (blk - 1.0 - i)[None, :])
    w_q = jnp.exp(lg[:, None] * (i + 1.0)[None, :])
    g_block = jnp.exp(lg * blk)
    lane = lambda a: jnp.broadcast_to(a[:, :, None], (RET_HEADS, blk, LANES))
    return dmat, lane(w_q), lane(w_end), g_block


def _retention_kernel(gblk_ref, q_ref, k_ref, v_ref, g_ref, s0_ref, dmat_ref, wq_ref, wend_ref, gain_ref,
                      y_ref, sfin_ref, s_scr):
    t = pl.program_id(1)

    @pl.when(t == 0)
    def _():
        s_scr[...] = s0_ref[0]

    for hd in range(RET_HEADS):
        cols = slice(hd * RET_DK, (hd + 1) * RET_DK)
        q = q_ref[:, cols]
        k = k_ref[:, cols] * RET_DK ** -0.5
        vb = v_ref[:, cols].astype(BF16)
        state = s_scr[hd]
        scores = lax.dot_general(q.astype(BF16), k.astype(BF16), _NT, preferred_element_type=F32) * dmat_ref[hd]
        o = jnp.dot(scores.astype(BF16), vb, preferred_element_type=F32)
        o = o + jnp.dot((q * wq_ref[hd]).astype(BF16), state.astype(BF16), preferred_element_type=F32)
        k_end_t = (k * wend_ref[hd]).T.astype(BF16)
        s_scr[hd] = gblk_ref[hd] * state + jnp.dot(k_end_t, vb, preferred_element_type=F32)
        mu = jnp.mean(o, axis=-1, keepdims=True)
        var = jnp.mean(jnp.square(o - mu), axis=-1, keepdims=True)
        on = ((o - mu) * lax.rsqrt(var + NORM_EPS)) * gain_ref[:, cols]
        g = g_ref[:, cols]
        y_ref[:, cols] = ((g * jax.nn.sigmoid(g)) * on).astype(y_ref.dtype)

    @pl.when(t == pl.num_programs(1) - 1)
    def _():
        sfin_ref[0] = s_scr[...]


def _retention(ret_in, s0, gn_gain, batch, seq, blk):
    nt = seq // blk
    dmat, w_q, w_end, g_block = _retention_tables(blk)

    def col(c):
        return pl.BlockSpec((blk, RET_WIDTH), lambda b, t: (b * nt + t, c))

    const3 = lambda b, t: (0, 0, 0)
    return pl.pallas_call(
        _retention_kernel,
        grid=(batch, nt),
        in_specs=[
            pl.BlockSpec(memory_space=pltpu.SMEM),
            col(0), col(1), col(2), col(3),
            pl.BlockSpec((1, RET_HEADS, RET_DK, RET_DV), lambda b, t: (b, 0, 0, 0)),
            pl.BlockSpec((RET_HEADS, blk, blk), const3),
            pl.BlockSpec((RET_HEADS, blk, LANES), const3),
            pl.BlockSpec((RET_HEADS, blk, LANES), const3),
            pl.BlockSpec((1, RET_WIDTH), lambda b, t: (0, 0)),
        ],
        out_specs=[
            pl.BlockSpec((blk, RET_WIDTH), lambda b, t: (b * nt + t, 0)),
            pl.BlockSpec((1, RET_HEADS, RET_DK, RET_DV), lambda b, t: (b, 0, 0, 0)),
        ],
        out_shape=(
            jax.ShapeDtypeStruct((batch * seq, RET_WIDTH), BF16),
            jax.ShapeDtypeStruct((batch, RET_HEADS, RET_DK, RET_DV), F32),
        ),
        scratch_shapes=[pltpu.VMEM((RET_HEADS, RET_DK, RET_DV), F32)],
        compiler_params=_compiler_params(("parallel", "arbitrary")),
        name="retention",
    )(g_block, ret_in, ret_in, ret_in, ret_in, s0, dmat, w_q, w_end, gn_gain)


COUNT_ROWS = 32
MASKED = 2.0 * NEG
LOG2_E = 1.4426950408889634


def _ordered_bits_to_float(u):
    bits = jnp.where(u < 0, u ^ jnp.int32(-2 ** 31), ~u)
    return lax.bitcast_convert_type(bits, F32)


def _high_half(x):
    bits = lax.bitcast_convert_type(x, jnp.int32) & jnp.int32(-(2 ** 16))
    return lax.bitcast_convert_type(bits, F32)


def _attention_kernel(lim_ref, qa_ref, qi_ref, wt_ref, ki_ref, ka_ref, vt_ref, o_ref,
                      s_scr, s16_scr, xa_scr, xb_scr, acc_scr,
                      *, qb, tk, n_keys, topk, causal, n_valid_q, index_bits):
    n_kt = pl.program_id(1) * (qb // tk) + (qb // tk) if causal else ki_ref.shape[0] // tk
    lim = lim_ref[0]
    lane = lax.broadcasted_iota(jnp.int32, (qb, LANES), 1)
    key_iota = lax.broadcasted_iota(jnp.int32, (tk, qb), 0)

    def head_operand(ref, hd, scale, shift_to_low):
        pair = ref[:, (hd // 2) * LANES:(hd // 2 + 1) * LANES] * scale
        if shift_to_low and hd % 2:
            pair = pltpu.roll(pair, 64, 1)
        keep = (lane < 64) if (shift_to_low or hd % 2 == 0) else (lane >= 64)
        return jnp.where(keep, pair, 0.0).T.astype(BF16)

    qi_heads = [head_operand(qi_ref, hd, 1.0, True) for hd in range(IDX_HEADS)]
    wt = wt_ref[0]

    def score_body(kt, carry):
        r0 = pl.multiple_of(kt * tk, tk)
        keys = ki_ref[pl.ds(r0, tk), :]
        acc = jnp.zeros((tk, qb), F32)
        for hd in range(IDX_HEADS):
            d = jnp.dot(keys, qi_heads[hd], preferred_element_type=F32)
            acc = acc + jnp.maximum(d, 0.0) * wt[hd:hd + 1, :]
        s = jnp.where(key_iota + r0 < lim, acc, -jnp.inf)
        s_scr[pl.ds(r0, tk), :] = s
        s16_scr[pl.ds(r0, tk), :] = _high_half(s).astype(s16_scr.dtype)
        return carry

    lax.fori_loop(0, n_kt, score_body, 0)

    def count(src_ref, indicator):
        def body(kt, acc):
            for part in range(tk // COUNT_ROWS):
                r0 = pl.multiple_of(kt * tk + part * COUNT_ROWS, COUNT_ROWS)
                acc = acc + indicator(src_ref[pl.ds(r0, COUNT_ROWS), :], r0)
            return acc
        acc = lax.fori_loop(0, n_kt, body, jnp.zeros((COUNT_ROWS, qb), src_ref.dtype))
        return jnp.sum(acc.astype(F32), axis=0, keepdims=True)

    n_masked = (n_keys - lim).astype(F32)
    kf = float(topk)
    q_lane = lax.broadcasted_iota(jnp.int32, (1, qb), 1)
    one16 = jnp.ones((), s16_scr.dtype)
    zero16 = jnp.zeros((), s16_scr.dtype)

    def bisect_step(it, cur, cnt, settled, upper_half):
        cand = cur | lax.shift_left(jnp.int32(1), 31 - it)
        theta = _ordered_bits_to_float(cand)
        if upper_half:
            theta16 = _high_half(theta).astype(s16_scr.dtype)
            c = count(s16_scr, lambda s, r0: jnp.where(s >= theta16, one16, zero16))
        else:
            c = count(s_scr, lambda s, r0: jnp.where(s >= theta, 1.0, 0.0))
        total = c + jnp.where(theta <= NEG, n_masked, 0.0)
        ok = total >= kf
        settled = jnp.maximum(settled, jnp.where(total == kf, 1.0, 0.0))
        return jnp.where(ok, cand, cur), jnp.where(ok, c, cnt), settled

    state = (jnp.zeros((1, qb), jnp.int32), jnp.zeros((1, qb), F32),
             jnp.where(q_lane < n_valid_q, 0.0, 1.0))
    state = lax.fori_loop(0, 16, lambda it, st: bisect_step(it, *st, True), state)
    _, cur, cnt_ge, _ = lax.while_loop(
        lambda st: (st[0] < 32) & (jnp.min(st[3]) < 0.5),
        lambda st: (st[0] + 1,) + bisect_step(st[0], st[1], st[2], st[3], False),
        (jnp.int32(16),) + state)
    theta = _ordered_bits_to_float(cur)

    overflow = jnp.where((cnt_ge > kf) & (q_lane < n_valid_q), 1.0, 0.0)
    all_idx = jnp.full((1, qb), 2 ** index_bits - 1, jnp.int32)

    def tie_cut():
        need = kf - count(s_scr, lambda s, r0: jnp.where(s > theta, 1.0, 0.0))
        rows = lax.broadcasted_iota(jnp.int32, (COUNT_ROWS, qb), 0)

        def body(it, cut):
            cand = cut | lax.shift_left(jnp.int32(1), index_bits - 1 - it)
            c = count(s_scr, lambda s, r0: jnp.where(s == theta, jnp.where(rows + r0 < cand, 1.0, 0.0), 0.0))
            return jnp.where(c <= need, cand, cut)

        return lax.fori_loop(0, index_bits, body, jnp.zeros((1, qb), jnp.int32))

    idx_cut = lax.cond(jnp.max(overflow) > 0.0, tie_cut, lambda: all_idx)

    def bias_body(kt, carry):
        r0 = pl.multiple_of(kt * tk, tk)
        s = s_scr[pl.ds(r0, tk), :]
        tie = jnp.where(key_iota + r0 < idx_cut, 0.0, MASKED)
        s_scr[pl.ds(r0, tk), :] = jnp.where(s > theta, 0.0, jnp.where(s == theta, tie, MASKED))
        return carry

    lax.fori_loop(0, n_kt, bias_body, 0)

    lp = ki_ref.shape[0]
    s_scr[pl.ds(lp, tk), :] = jnp.full((tk, qb), MASKED, F32)
    qa_heads = [head_operand(qa_ref, hd, ATT_DH ** -0.5 * LOG2_E, False) for hd in range(ATT_HEADS)]
    acc_scr[...] = jnp.zeros(acc_scr.shape, F32)

    def logits_stage(kt, x_ref):
        kt_in = jnp.minimum(kt, n_kt - 1)
        r0 = pl.multiple_of(kt_in * tk, tk)
        b0 = pl.multiple_of(jnp.where(kt < n_kt, kt_in * tk, lp), tk)
        tops = []
        for hd in range(ATT_HEADS):
            keys = ka_ref[pl.ds(r0, tk), (hd // 2) * LANES:(hd // 2 + 1) * LANES]
            x = jnp.dot(keys, qa_heads[hd], preferred_element_type=F32) + s_scr[pl.ds(b0, tk), :]
            x_ref[hd] = x
            tops.append(jnp.max(x, axis=0, keepdims=True))
        return jnp.concatenate(tops, axis=0)

    def accumulate(kt, x_ref, tile_max, m_run, l_run):
        kt_in = jnp.minimum(kt, n_kt - 1)
        m_new = jnp.maximum(m_run, tile_max)
        alpha = jnp.exp2(m_run - m_new)
        sums = []
        for hd in range(ATT_HEADS):
            p = jnp.exp2(x_ref[hd] - m_new[hd:hd + 1, :])
            sums.append(jnp.sum(p, axis=0, keepdims=True))
            rows = slice(hd * ATT_DH, (hd + 1) * ATT_DH)
            pv = jnp.dot(vt_ref[kt_in, rows, :], p.astype(BF16), preferred_element_type=F32)
            acc_scr[rows, :] = alpha[hd:hd + 1, :] * acc_scr[rows, :] + pv
        return m_new, alpha * l_run + jnp.concatenate(sums, axis=0)

    def pair_body(pair, carry):
        max_a, m_run, l_run = carry
        max_b = logits_stage(2 * pair + 1, xb_scr)
        m_run, l_run = accumulate(2 * pair, xa_scr, max_a, m_run, l_run)
        max_a = logits_stage(2 * pair + 2, xa_scr)
        m_run, l_run = accumulate(2 * pair + 1, xb_scr, max_b, m_run, l_run)
        return max_a, m_run, l_run

    first_max = logits_stage(0, xa_scr)
    _, _, l_fin = lax.fori_loop(
        0, (n_kt + 1) // 2, pair_body,
        (first_max, jnp.full((ATT_HEADS, qb), NEG, F32), jnp.zeros((ATT_HEADS, qb), F32)))

    for hd in range(ATT_HEADS):
        rows = slice(hd * ATT_DH, (hd + 1) * ATT_DH)
        acc_scr[rows, :] = acc_scr[rows, :] / l_fin[hd:hd + 1, :]
    o_ref[...] = acc_scr[...].T.astype(o_ref.dtype)


def _attention(q_a, q_i, w_t, lim, k_i, k_a, v_t, *, qb, tk, n_keys, topk, causal, n_valid_q):
    batch, tq, _ = q_a.shape
    lp = k_i.shape[1]
    nq = tq // qb
    index_bits = int(np.ceil(np.log2(lp + 1)))
    assert lp // COUNT_ROWS <= 256, "bfloat16 partial counts are exact only up to 256"
    kernel = functools.partial(_attention_kernel, qb=qb, tk=tk, n_keys=n_keys, topk=topk, causal=causal,
                               n_valid_q=n_valid_q, index_bits=index_bits)
    return pl.pallas_call(
        kernel,
        grid=(batch, nq),
        in_specs=[
            pl.BlockSpec((1, 1, qb), lambda b, i: (i, 0, 0)),
            pl.BlockSpec((None, qb, ATT_WIDTH), lambda b, i: (b, i, 0)),
            pl.BlockSpec((None, qb, IDX_HEADS * IDX_DH), lambda b, i: (b, i, 0)),
            pl.BlockSpec((1, IDX_HEADS, qb), lambda b, i: (b, 0, i)),
            _resident((None, lp, LANES), lambda b, i: (b, 0, 0)),
            _resident((None, lp, ATT_WIDTH), lambda b, i: (b, 0, 0)),
            _resident((None, lp // tk, ATT_WIDTH, tk), lambda b, i: (b, 0, 0, 0)),
        ],
        out_specs=pl.BlockSpec((None, qb, ATT_WIDTH), lambda b, i: (b, i, 0)),
        out_shape=jax.ShapeDtypeStruct((batch, tq, ATT_WIDTH), BF16),
        scratch_shapes=[
            pltpu.VMEM((lp + tk, qb), F32),
            pltpu.VMEM((lp, qb), BF16),
            pltpu.VMEM((ATT_HEADS, tk, qb), F32),
            pltpu.VMEM((ATT_HEADS, tk, qb), F32),
            pltpu.VMEM((ATT_WIDTH, qb), F32),
        ],
        compiler_params=_compiler_params(("parallel", "arbitrary")),
        name="attention",
    )(lim, q_a, q_i, w_t, k_i, k_a, v_t)


FF_CHUNK = 512


def _mlp_kernel(x_ref, yr_ref, ya_ref, wo_ref, g2_ref, wup_ref, wdn_ref, gf_ref, y_ref, x_scr, h_scr, a_scr):
    x = x_ref[...] + jnp.dot(yr_ref[...], wo_ref[:RET_WIDTH, :], preferred_element_type=F32)
    x = x + jnp.dot(ya_ref[...], wo_ref[RET_WIDTH:, :], preferred_element_type=F32)
    x_scr[...] = x
    ms = jnp.mean(x * x, axis=-1, keepdims=True)
    h_scr[...] = ((x * lax.rsqrt(ms + NORM_EPS)) * g2_ref[...]).astype(h_scr.dtype)
    for c in range(D_FF // FF_CHUNK):
        cols = slice(c * FF_CHUNK, (c + 1) * FF_CHUNK)
        u = jnp.dot(h_scr[...], wup_ref[:, cols], preferred_element_type=F32)
        a_scr[:, cols] = jnp.square(jnp.maximum(u, 0.0)).astype(a_scr.dtype)
    x = x_scr[...] + jnp.dot(a_scr[...], wdn_ref[...], preferred_element_type=F32)
    ms = jnp.mean(x * x, axis=-1, keepdims=True)
    y_ref[...] = (x * lax.rsqrt(ms + NORM_EPS)) * gf_ref[...]


def _mlp(x2d, y_r, y_a, w_out_b, norm2, w_up_b, w_down_b, norm_final, tm):
    m = x2d.shape[0]
    row = lambda i: (i, 0)
    const = lambda i: (0, 0)
    return pl.pallas_call(
        _mlp_kernel,
        grid=(m // tm,),
        in_specs=[
            pl.BlockSpec((tm, D_MODEL), row),
            pl.BlockSpec((tm, RET_WIDTH), row),
            pl.BlockSpec((tm, ATT_WIDTH), row),
            _resident((D_MODEL, D_MODEL), const),
            _resident((1, D_MODEL), const),
            _resident((D_MODEL, D_FF), const),
            _resident((D_FF, D_MODEL), const),
            _resident((1, D_MODEL), const),
        ],
        out_specs=pl.BlockSpec((tm, D_MODEL), row),
        out_shape=jax.ShapeDtypeStruct((m, D_MODEL), F32),
        scratch_shapes=[pltpu.VMEM((tm, D_MODEL), F32), pltpu.VMEM((tm, D_MODEL), BF16),
                        pltpu.VMEM((tm, D_FF), BF16)],
        compiler_params=_compiler_params(("parallel",)),
        name="mlp",
    )(x2d, y_r, y_a, w_out_b, norm2, w_up_b, w_down_b, norm_final)


def _pick_tile(n, target):
    t = min(n, target)
    while n % t:
        t //= 2
    return t


def _layer(x, pos, past, s0, ret_blk, weights, *, attn_qb, attn_tk):
    norm1, w_in_b, gn_gain, w_out_b, norm2, w_up_b, w_down_b, norm_final = weights
    batch, seq, _ = x.shape
    m = batch * seq
    x2d = x.reshape(m, D_MODEL)
    tm = _pick_tile(m, 512)

    tabs = _rope_tables(pos)
    if seq < tm:
        tabs = jnp.tile(tabs, (1, tm // seq, 1))
    ret_in, q_a, k_a, v_a, q_i, kw = _proj(x2d, norm1, w_in_b, tabs, tm)

    y_r, s_new = _retention(ret_in, s0, gn_gain, batch, seq, ret_blk)

    three = lambda a: a.reshape(batch, seq, a.shape[-1])
    k_i = three(kw[:, :IDX_DH])
    w_t = jnp.swapaxes(three(kw[:, IDX_DH:IDX_DH + IDX_HEADS]) * IDX_HEADS ** -0.5 * IDX_DH ** -0.5, 1, 2)
    q_a3, q_i3 = three(q_a), three(q_i)
    if past is None:
        n_keys = seq
        keys_i, keys_a, vals = three(kw), three(k_a), three(v_a)
        qpos = pos
        n_valid_q = attn_qb
    else:
        past_k, past_v, past_ki = past
        n_keys = past_k.shape[1] + seq
        pad_i = jnp.pad(past_ki, ((0, 0), (0, 0), (0, LANES - IDX_DH)))
        keys_i = jnp.concatenate([pad_i, three(kw)], axis=1)
        keys_a = jnp.concatenate([past_k.reshape(batch, -1, ATT_WIDTH), three(k_a)], axis=1)
        vals = jnp.concatenate([past_v.reshape(batch, -1, ATT_WIDTH), three(v_a)], axis=1)
        qpos = pos
        n_valid_q = seq
    topk = min(TOPK_MAX, n_keys // 4)
    tq = -(-seq // attn_qb) * attn_qb
    lp = -(-n_keys // attn_tk) * attn_tk
    padq = lambda a: jnp.pad(a, ((0, 0), (0, tq - seq), (0, 0)))
    padk = lambda a: jnp.pad(a, ((0, 0), (0, lp - n_keys), (0, 0)))
    lim = jnp.minimum((qpos // CHUNK + 1) * CHUNK, n_keys).astype(jnp.int32)
    lim = jnp.pad(lim, (0, tq - seq), constant_values=n_keys).reshape(tq // attn_qb, 1, attn_qb)
    v_t = jnp.swapaxes(padk(vals).astype(BF16).reshape(batch, lp // attn_tk, attn_tk, ATT_WIDTH), 2, 3)
    y_a = _attention(
        padq(q_a3), padq(q_i3), jnp.pad(w_t, ((0, 0), (0, 0), (0, tq - seq))), lim,
        padk(keys_i).astype(BF16), padk(keys_a).astype(BF16), v_t,
        qb=attn_qb, tk=attn_tk, n_keys=n_keys, topk=topk, causal=past is None, n_valid_q=n_valid_q)
    y_a = y_a[:, :seq].reshape(m, ATT_WIDTH)

    y = _mlp(x2d, y_r, y_a, w_out_b, norm2, w_up_b, w_down_b, norm_final, tm)
    return (y.reshape(batch, seq, D_MODEL),
            k_a.reshape(batch, seq, ATT_HEADS, ATT_DH),
            v_a.reshape(batch, seq, ATT_HEADS, ATT_DH),
            k_i, s_new)


def kernel(x_prompt, x_sample, cache_k_att, cache_v_att, cache_k_idx, state_ret, norm1, w_in, gn_gain, w_out,
           norm2, w_up, w_down, norm_final):
    depth = norm1.shape[0]
    assert depth == 1, "the final norm is fused into the layer's last kernel"
    past_len = cache_k_att.shape[2]
    pos_p = jnp.arange(x_prompt.shape[1], dtype=jnp.int32)
    pos_s = past_len + jnp.arange(x_sample.shape[1], dtype=jnp.int32)
    l = 0
    w_in_b = jnp.pad(w_in[l], ((0, 0), (0, IN_WIDTH_PADDED - IN_WIDTH))).astype(BF16)
    weights = (norm1[l][None], w_in_b, gn_gain[l][None], w_out[l].astype(BF16), norm2[l][None],
               w_up[l].astype(BF16), w_down[l].astype(BF16), norm_final[None])
    s0_p = jnp.zeros((x_prompt.shape[0], RET_HEADS, RET_DK, RET_DV), F32)
    yp, kap, vap, kip, sp = _layer(x_prompt, pos_p, None, s0_p, 256, weights, attn_qb=256, attn_tk=256)
    past = (cache_k_att[l], cache_v_att[l], cache_k_idx[l])
    ys, kas, vas, kis, ss = _layer(x_sample, pos_s, past, state_ret[l].astype(F32), x_sample.shape[1], weights,
                                   attn_qb=128, attn_tk=256)
    return (yp, ys, kap[None], vap[None], kip[None], sp[None], kas[None], vas[None], kis[None], ss[None])
```

```python
import functools

import jax
import jax.numpy as jnp
import numpy as np
from jax import lax
from jax.experimental import pallas as pl
from jax.experimental.pallas import tpu as pltpu

D_MODEL = 1024
CHUNK = 64
ROPE_THETA = 10000.0
NORM_EPS = 1e-6
NEG = -1e30
RET_WIDTH = 512
ATT_WIDTH = 512
RET_HEADS = 4
RET_DK = 128
RET_DV = 128
ATT_HEADS = 8
ATT_DH = 64
IDX_HEADS = 8
IDX_DH = 64
TOPK_MAX = 256
D_FF = 4 * D_MODEL
IN_WIDTH = 4 * RET_WIDTH + 3 * ATT_WIDTH + IDX_HEADS * IDX_DH + IDX_DH + IDX_HEADS

LANES = 128
IN_WIDTH_PADDED = ((IN_WIDTH + LANES - 1) // LANES) * LANES
VMEM_LIMIT_BYTES = 56 * 1024 * 1024

F32 = jnp.float32
BF16 = jnp.bfloat16

_NT = (((1,), (1,)), ((), ()))


def _compiler_params(semantics):
    return pltpu.CompilerParams(dimension_semantics=semantics, vmem_limit_bytes=VMEM_LIMIT_BYTES)


def _resident(block_shape, index_map):
    return pl.BlockSpec(block_shape, index_map, pipeline_mode=pl.Buffered(1))


def _rope_tables(pos):
    posf = pos.astype(F32)[:, None]

    def angles(d):
        half = d // 2
        inv = ROPE_THETA ** (-jnp.arange(half, dtype=F32) * 2.0 / d)
        ang = posf * inv[None, :]
        return jnp.cos(ang), jnp.sin(ang)

    c128, s128 = angles(128)
    zero64 = jnp.zeros_like(s128)
    cos128 = jnp.concatenate([c128, c128], axis=1)
    sin128 = jnp.concatenate([-s128, s128], axis=1)
    c64, s64 = angles(64)
    zero32 = jnp.zeros_like(s64)
    cos64 = jnp.concatenate([c64, c64, c64, c64], axis=1)
    sin64_up = jnp.concatenate([zero32, s64, zero32, s64], axis=1)
    sin64_dn = jnp.concatenate([-s64, zero32, -s64, zero32], axis=1)
    one64 = jnp.ones_like(zero64)
    cosk = jnp.concatenate([c64, c64, one64], axis=1)
    sink_up = jnp.concatenate([zero32, s64, zero64], axis=1)
    sink_dn = jnp.concatenate([-s64, zero32, zero64], axis=1)
    return jnp.stack([cos128, sin128, cos64, sin64_up, sin64_dn, cosk, sink_up, sink_dn])


def _proj_kernel(x_ref, g_ref, w_ref, tab_ref, ret_ref, qa_ref, ka_ref, va_ref, qi_ref, kw_ref):
    x = x_ref[...]
    ms = jnp.mean(x * x, axis=-1, keepdims=True)
    h = ((x * lax.rsqrt(ms + NORM_EPS)) * g_ref[...]).astype(BF16)

    def mm(c0, n):
        return jnp.dot(h, w_ref[:, c0:c0 + n], preferred_element_type=F32)

    def rope128(z):
        return z * tab_ref[0] + pltpu.roll(z, 64, 1) * tab_ref[1]

    def rope64(z):
        return z * tab_ref[2] + pltpu.roll(z, 32, 1) * tab_ref[3] + pltpu.roll(z, 96, 1) * tab_ref[4]

    def ropek(z):
        return z * tab_ref[5] + pltpu.roll(z, 32, 1) * tab_ref[6] + pltpu.roll(z, 96, 1) * tab_ref[7]

    def emit(out_ref, out_c0, w_c0, rope):
        z = mm(w_c0, 512)
        for t in range(4):
            zt = z[:, t * LANES:(t + 1) * LANES]
            if rope is not None:
                zt = rope(zt)
            out_ref[:, out_c0 + t * LANES:out_c0 + (t + 1) * LANES] = zt.astype(out_ref.dtype)

    emit(ret_ref, 0, 0, rope128)
    emit(ret_ref, 512, 512, rope128)
    emit(ret_ref, 1024, 1024, None)
    emit(ret_ref, 1536, 1536, None)
    emit(qa_ref, 0, 2048, rope64)
    emit(ka_ref, 0, 2560, rope64)
    emit(va_ref, 0, 3072, None)
    emit(qi_ref, 0, 3584, rope64)
    kw_ref[...] = ropek(mm(4096, LANES))


def _proj(x2d, norm_g, w_in_b, tabs, tm):
    m = x2d.shape[0]
    n_tab = tabs.shape[1] // tm
    row = lambda i: (i, 0)
    outs = (
        jax.ShapeDtypeStruct((m, 4 * RET_WIDTH), F32),
        jax.ShapeDtypeStruct((m, ATT_WIDTH), F32),
        jax.ShapeDtypeStruct((m, ATT_WIDTH), F32),
        jax.ShapeDtypeStruct((m, ATT_WIDTH), F32),
        jax.ShapeDtypeStruct((m, IDX_HEADS * IDX_DH), F32),
        jax.ShapeDtypeStruct((m, LANES), F32),
    )
    return pl.pallas_call(
        _proj_kernel,
        grid=(m // tm,),
        in_specs=[
            pl.BlockSpec((tm, D_MODEL), row),
            _resident((1, D_MODEL), lambda i: (0, 0)),
            _resident((D_MODEL, IN_WIDTH_PADDED), lambda i: (0, 0)),
            pl.BlockSpec((8, tm, LANES), lambda i: (0, i % n_tab, 0)),
        ],
        out_specs=[
            pl.BlockSpec((tm, 4 * RET_WIDTH), row),
            pl.BlockSpec((tm, ATT_WIDTH), row),
            pl.BlockSpec((tm, ATT_WIDTH), row),
            pl.BlockSpec((tm, ATT_WIDTH), row),
            pl.BlockSpec((tm, IDX_HEADS * IDX_DH), row),
            pl.BlockSpec((tm, LANES), row),
        ],
        out_shape=outs,
        compiler_params=_compiler_params(("parallel",)),
        name="proj",
    )(x2d, norm_g, w_in_b, tabs)


def _retention_tables(blk):
    lg = jnp.log1p(-jnp.exp2(-5.0 - jnp.arange(RET_HEADS, dtype=F32)))
    i = jnp.arange(blk, dtype=F32)
    diff = i[:, None] - i[None, :]
    dmat = jnp.where(diff[None] >= 0, jnp.exp(lg[:, None, None] * jnp.maximum(diff, 0.0)[None]), 0.0)
    w_end = jnp.exp(lg[:, None] * (blk - 1.0 - i)[None, :])
    w_q = jnp.exp(lg[:, None] * (i + 1.0)[None, :])
    g_block = jnp.exp(lg * blk)
    lane = lambda a: jnp.broadcast_to(a[:, :, None], (RET_HEADS, blk, LANES))
    return dmat, lane(w_q), lane(w_end), g_block


def _retention_kernel(gblk_ref, q_ref, k_ref, v_ref, g_ref, s0_ref, dmat_ref, wq_ref, wend_ref, gain_ref,
                      y_ref, sfin_ref, s_scr):
    t = pl.program_id(1)

    @pl.when(t == 0)
    def _():
        s_scr[...] = s0_ref[0]

    for hd in range(RET_HEADS):
        cols = slice(hd * RET_DK, (hd + 1) * RET_DK)
        q = q_ref[:, cols]
        k = k_ref[:, cols] * RET_DK ** -0.5
        vb = v_ref[:, cols].astype(BF16)
        state = s_scr[hd]
        scores = lax.dot_general(q.astype(BF16), k.astype(BF16), _NT, preferred_element_type=F32) * dmat_ref[hd]
        o = jnp.dot(scores.astype(BF16), vb, preferred_element_type=F32)
        o = o + jnp.dot((q * wq_ref[hd]).astype(BF16), state.astype(BF16), preferred_element_type=F32)
        k_end_t = (k * wend_ref[hd]).T.astype(BF16)
        s_scr[hd] = gblk_ref[hd] * state + jnp.dot(k_end_t, vb, preferred_element_type=F32)
        mu = jnp.mean(o, axis=-1, keepdims=True)
        var = jnp.mean(jnp.square(o - mu), axis=-1, keepdims=True)
        on = ((o - mu) * lax.rsqrt(var + NORM_EPS)) * gain_ref[:, cols]
        g = g_ref[:, cols]
        y_ref[:, cols] = ((g * jax.nn.sigmoid(g)) * on).astype(y_ref.dtype)

    @pl.when(t == pl.num_programs(1) - 1)
    def _():
        sfin_ref[0] = s_scr[...]


def _retention(ret_in, s0, gn_gain, batch, seq, blk):
    nt = seq // blk
    dmat, w_q, w_end, g_block = _retention_tables(blk)

    def col(c):
        return pl.BlockSpec((blk, RET_WIDTH), lambda b, t: (b * nt + t, c))

    const3 = lambda b, t: (0, 0, 0)
    return pl.pallas_call(
        _retention_kernel,
        grid=(batch, nt),
        in_specs=[
            pl.BlockSpec(memory_space=pltpu.SMEM),
            col(0), col(1), col(2), col(3),
            pl.BlockSpec((1, RET_HEADS, RET_DK, RET_DV), lambda b, t: (b, 0, 0, 0)),
            pl.BlockSpec((RET_HEADS, blk, blk), const3),
            pl.BlockSpec((RET_HEADS, blk, LANES), const3),
            pl.BlockSpec((RET_HEADS, blk, LANES), const3),
            pl.BlockSpec((1, RET_WIDTH), lambda b, t: (0, 0)),
        ],
        out_specs=[
            pl.BlockSpec((blk, RET_WIDTH), lambda b, t: (b * nt + t, 0)),
            pl.BlockSpec((1, RET_HEADS, RET_DK, RET_DV), lambda b, t: (b, 0, 0, 0)),
        ],
        out_shape=(
            jax.ShapeDtypeStruct((batch * seq, RET_WIDTH), BF16),
            jax.ShapeDtypeStruct((batch, RET_HEADS, RET_DK, RET_DV), F32),
        ),
        scratch_shapes=[pltpu.VMEM((RET_HEADS, RET_DK, RET_DV), F32)],
        compiler_params=_compiler_params(("parallel", "arbitrary")),
        name="retention",
    )(g_block, ret_in, ret_in, ret_in, ret_in, s0, dmat, w_q, w_end, gn_gain)


COUNT_ROWS = 32
MASKED = 2.0 * NEG
LOG2_E = 1.4426950408889634
V_ROWS = ATT_DH + 16


def _ordered_bits_to_float(u):
    bits = jnp.where(u < 0, u ^ jnp.int32(-2 ** 31), ~u)
    return lax.bitcast_convert_type(bits, F32)


def _high_half(x):
    bits = lax.bitcast_convert_type(x, jnp.int32) & jnp.int32(-(2 ** 16))
    return lax.bitcast_convert_type(bits, F32)


def _attention_kernel(lim_ref, qa_ref, qi_ref, wt_ref, ki_ref, ka_ref, vt_ref, o_ref,
                      s_scr, s16_scr, xa_scr, xb_scr, acc_scr, out_scr,
                      *, qb, tk, n_keys, topk, causal, n_valid_q, index_bits):
    n_kt = pl.program_id(1) * (qb // tk) + (qb // tk) if causal else ki_ref.shape[0] // tk
    lim = lim_ref[0]
    lane = lax.broadcasted_iota(jnp.int32, (qb, LANES), 1)
    key_iota = lax.broadcasted_iota(jnp.int32, (tk, qb), 0)

    def head_operand(ref, hd, scale, shift_to_low):
        pair = ref[:, (hd // 2) * LANES:(hd // 2 + 1) * LANES] * scale
        if shift_to_low and hd % 2:
            pair = pltpu.roll(pair, 64, 1)
        keep = (lane < 64) if (shift_to_low or hd % 2 == 0) else (lane >= 64)
        return jnp.where(keep, pair, 0.0).T.astype(BF16)

    qi_heads = [head_operand(qi_ref, hd, 1.0, True) for hd in range(IDX_HEADS)]
    wt = wt_ref[0]

    def score_body(kt, carry):
        r0 = pl.multiple_of(kt * tk, tk)
        keys = ki_ref[pl.ds(r0, tk), :]
        acc = jnp.zeros((tk, qb), F32)
        for hd in range(IDX_HEADS):
            d = jnp.dot(keys, qi_heads[hd], preferred_element_type=F32)
            acc = acc + jnp.maximum(d, 0.0) * wt[hd:hd + 1, :]
        s = jnp.where(key_iota + r0 < lim, acc, -jnp.inf)
        s_scr[pl.ds(r0, tk), :] = s
        s16_scr[pl.ds(r0, tk), :] = _high_half(s).astype(s16_scr.dtype)
        return carry

    lax.fori_loop(0, n_kt, score_body, 0)

    def count(src_ref, indicator):
        def body(kt, acc):
            for part in range(tk // COUNT_ROWS):
                r0 = pl.multiple_of(kt * tk + part * COUNT_ROWS, COUNT_ROWS)
                acc = acc + indicator(src_ref[pl.ds(r0, COUNT_ROWS), :], r0)
            return acc
        acc = lax.fori_loop(0, n_kt, body, jnp.zeros((COUNT_ROWS, qb), src_ref.dtype))
        return jnp.sum(acc.astype(F32), axis=0, keepdims=True)

    n_masked = (n_keys - lim).astype(F32)
    kf = float(topk)
    q_lane = lax.broadcasted_iota(jnp.int32, (1, qb), 1)
    one16 = jnp.ones((), s16_scr.dtype)
    zero16 = jnp.zeros((), s16_scr.dtype)

    def bisect_step(it, cur, cnt, settled, upper_half):
        cand = cur | lax.shift_left(jnp.int32(1), 31 - it)
        theta = _ordered_bits_to_float(cand)
        if upper_half:
            theta16 = _high_half(theta).astype(s16_scr.dtype)
            c = count(s16_scr, lambda s, r0: jnp.where(s >= theta16, one16, zero16))
        else:
            c = count(s_scr, lambda s, r0: jnp.where(s >= theta, 1.0, 0.0))
        total = c + jnp.where(theta <= NEG, n_masked, 0.0)
        ok = total >= kf
        settled = jnp.maximum(settled, jnp.where(total == kf, 1.0, 0.0))
        return jnp.where(ok, cand, cur), jnp.where(ok, c, cnt), settled

    state = (jnp.zeros((1, qb), jnp.int32), jnp.zeros((1, qb), F32),
             jnp.where(q_lane < n_valid_q, 0.0, 1.0))
    state = lax.fori_loop(0, 16, lambda it, st: bisect_step(it, *st, True), state)
    _, cur, cnt_ge, _ = lax.while_loop(
        lambda st: (st[0] < 32) & (jnp.min(st[3]) < 0.5),
        lambda st: (st[0] + 1,) + bisect_step(st[0], st[1], st[2], st[3], False),
        (jnp.int32(16),) + state)
    theta = _ordered_bits_to_float(cur)

    overflow = jnp.where((cnt_ge > kf) & (q_lane < n_valid_q), 1.0, 0.0)
    all_idx = jnp.full((1, qb), 2 ** index_bits - 1, jnp.int32)

    def tie_cut():
        need = kf - count(s_scr, lambda s, r0: jnp.where(s > theta, 1.0, 0.0))
        rows = lax.broadcasted_iota(jnp.int32, (COUNT_ROWS, qb), 0)

        def body(it, cut):
            cand = cut | lax.shift_left(jnp.int32(1), index_bits - 1 - it)
            c = count(s_scr, lambda s, r0: jnp.where(s == theta, jnp.where(rows + r0 < cand, 1.0, 0.0), 0.0))
            return jnp.where(c <= need, cand, cut)

        return lax.fori_loop(0, index_bits, body, jnp.zeros((1, qb), jnp.int32))

    idx_cut = lax.cond(jnp.max(overflow) > 0.0, tie_cut, lambda: all_idx)

    def bias_body(kt, carry):
        r0 = pl.multiple_of(kt * tk, tk)
        s = s_scr[pl.ds(r0, tk), :]
        tie = jnp.where(key_iota + r0 < idx_cut, 0.0, MASKED)
        s_scr[pl.ds(r0, tk), :] = jnp.where(s > theta, 0.0, jnp.where(s == theta, tie, MASKED))
        return carry

    lax.fori_loop(0, n_kt, bias_body, 0)

    lp = ki_ref.shape[0]
    s_scr[pl.ds(lp, tk), :] = jnp.full((tk, qb), MASKED, F32)
    qa_heads = [head_operand(qa_ref, hd, ATT_DH ** -0.5 * LOG2_E, False) for hd in range(ATT_HEADS)]
    acc_scr[...] = jnp.zeros(acc_scr.shape, F32)

    def logits_stage(kt, x_ref):
        kt_in = jnp.minimum(kt, n_kt - 1)
        r0 = pl.multiple_of(kt_in * tk, tk)
        b0 = pl.multiple_of(jnp.where(kt < n_kt, kt_in * tk, lp), tk)
        tops = []
        for hd in range(ATT_HEADS):
            keys = ka_ref[pl.ds(r0, tk), (hd // 2) * LANES:(hd // 2 + 1) * LANES]
            x = jnp.dot(keys, qa_heads[hd], preferred_element_type=F32) + s_scr[pl.ds(b0, tk), :]
            x_ref[hd] = x.astype(x_ref.dtype)
            tops.append(jnp.max(x, axis=0, keepdims=True))
        return jnp.concatenate(tops, axis=0).astype(x_ref.dtype).astype(F32)

    def accumulate(kt, x_ref, tile_max, m_run):
        kt_in = jnp.minimum(kt, n_kt - 1)
        m_new = jnp.maximum(m_run, tile_max)
        alpha = jnp.exp2(m_run - m_new)
        m_staged = m_new.astype(x_ref.dtype)
        for hd in range(ATT_HEADS):
            p = jnp.exp2(x_ref[hd] - m_staged[hd:hd + 1, :])
            rows = slice(hd * V_ROWS, (hd + 1) * V_ROWS)
            pv = jnp.dot(vt_ref[kt_in, rows, :], p.astype(BF16), preferred_element_type=F32)
            acc_scr[rows, :] = alpha[hd:hd + 1, :] * acc_scr[rows, :] + pv
        return m_new

    def pair_body(pair, carry):
        max_a, m_run = carry
        max_b = logits_stage(2 * pair + 1, xb_scr)
        m_run = accumulate(2 * pair, xa_scr, max_a, m_run)
        max_a = logits_stage(2 * pair + 2, xa_scr)
        m_run = accumulate(2 * pair + 1, xb_scr, max_b, m_run)
        return max_a, m_run

    first_max = logits_stage(0, xa_scr)
    lax.fori_loop(0, (n_kt + 1) // 2, pair_body, (first_max, jnp.full((ATT_HEADS, qb), NEG, F32)))

    for hd in range(ATT_HEADS):
        num = acc_scr[hd * V_ROWS:hd * V_ROWS + ATT_DH, :]
        den = acc_scr[hd * V_ROWS + ATT_DH:hd * V_ROWS + ATT_DH + 1, :]
        out_scr[hd * ATT_DH:(hd + 1) * ATT_DH, :] = num / den
    o_ref[...] = out_scr[...].T.astype(o_ref.dtype)


def _attention(q_a, q_i, w_t, lim, k_i, k_a, v_t, *, qb, tk, n_keys, topk, causal, n_valid_q):
    batch, tq, _ = q_a.shape
    lp = k_i.shape[1]
    nq = tq // qb
    index_bits = int(np.ceil(np.log2(lp + 1)))
    assert lp // COUNT_ROWS <= 256, "bfloat16 partial counts are exact only up to 256"
    kernel = functools.partial(_attention_kernel, qb=qb, tk=tk, n_keys=n_keys, topk=topk, causal=causal,
                               n_valid_q=n_valid_q, index_bits=index_bits)
    return pl.pallas_call(
        kernel,
        grid=(batch, nq),
        in_specs=[
            pl.BlockSpec((1, 1, qb), lambda b, i: (i, 0, 0)),
            pl.BlockSpec((None, qb, ATT_WIDTH), lambda b, i: (b, i, 0)),
            pl.BlockSpec((None, qb, IDX_HEADS * IDX_DH), lambda b, i: (b, i, 0)),
            pl.BlockSpec((1, IDX_HEADS, qb), lambda b, i: (b, 0, i)),
            _resident((None, lp, LANES), lambda b, i: (b, 0, 0)),
            _resident((None, lp, ATT_WIDTH), lambda b, i: (b, 0, 0)),
            _resident((None, lp // tk, ATT_HEADS * V_ROWS, tk), lambda b, i: (b, 0, 0, 0)),
        ],
        out_specs=pl.BlockSpec((None, qb, ATT_WIDTH), lambda b, i: (b, i, 0)),
        out_shape=jax.ShapeDtypeStruct((batch, tq, ATT_WIDTH), BF16),
        scratch_shapes=[
            pltpu.VMEM((lp + tk, qb), F32),
            pltpu.VMEM((lp, qb), BF16),
            pltpu.VMEM((ATT_HEADS, tk, qb), BF16),
            pltpu.VMEM((ATT_HEADS, tk, qb), BF16),
            pltpu.VMEM((ATT_HEADS * V_ROWS, qb), F32),
            pltpu.VMEM((ATT_WIDTH, qb), F32),
        ],
        compiler_params=_compiler_params(("parallel", "arbitrary")),
        name="attention",
    )(lim, q_a, q_i, w_t, k_i, k_a, v_t)


FF_CHUNK = 512


def _mlp_kernel(x_ref, yr_ref, ya_ref, wo_ref, g2_ref, wup_ref, wdn_ref, gf_ref, y_ref, x_scr, h_scr, a_scr):
    x = x_ref[...] + jnp.dot(yr_ref[...], wo_ref[:RET_WIDTH, :], preferred_element_type=F32)
    x = x + jnp.dot(ya_ref[...], wo_ref[RET_WIDTH:, :], preferred_element_type=F32)
    x_scr[...] = x
    ms = jnp.mean(x * x, axis=-1, keepdims=True)
    h_scr[...] = ((x * lax.rsqrt(ms + NORM_EPS)) * g2_ref[...]).astype(h_scr.dtype)
    for c in range(D_FF // FF_CHUNK):
        cols = slice(c * FF_CHUNK, (c + 1) * FF_CHUNK)
        u = jnp.dot(h_scr[...], wup_ref[:, cols], preferred_element_type=F32)
        a_scr[:, cols] = jnp.square(jnp.maximum(u, 0.0)).astype(a_scr.dtype)
    x = x_scr[...] + jnp.dot(a_scr[...], wdn_ref[...], preferred_element_type=F32)
    ms = jnp.mean(x * x, axis=-1, keepdims=True)
    y_ref[...] = (x * lax.rsqrt(ms + NORM_EPS)) * gf_ref[...]


def _mlp(x2d, y_r, y_a, w_out_b, norm2, w_up_b, w_down_b, norm_final, tm):
    m = x2d.shape[0]
    row = lambda i: (i, 0)
    const = lambda i: (0, 0)
    return pl.pallas_call(
        _mlp_kernel,
        grid=(m // tm,),
        in_specs=[
            pl.BlockSpec((tm, D_MODEL), row),
            pl.BlockSpec((tm, RET_WIDTH), row),
            pl.BlockSpec((tm, ATT_WIDTH), row),
            _resident((D_MODEL, D_MODEL), const),
            _resident((1, D_MODEL), const),
            _resident((D_MODEL, D_FF), const),
            _resident((D_FF, D_MODEL), const),
            _resident((1, D_MODEL), const),
        ],
        out_specs=pl.BlockSpec((tm, D_MODEL), row),
        out_shape=jax.ShapeDtypeStruct((m, D_MODEL), F32),
        scratch_shapes=[pltpu.VMEM((tm, D_MODEL), F32), pltpu.VMEM((tm, D_MODEL), BF16),
                        pltpu.VMEM((tm, D_FF), BF16)],
        compiler_params=_compiler_params(("parallel",)),
        name="mlp",
    )(x2d, y_r, y_a, w_out_b, norm2, w_up_b, w_down_b, norm_final)


def _pick_tile(n, target):
    t = min(n, target)
    while n % t:
        t //= 2
    return t


def _layer(x, pos, past, s0, ret_blk, weights, *, attn_qb, attn_tk):
    norm1, w_in_b, gn_gain, w_out_b, norm2, w_up_b, w_down_b, norm_final = weights
    batch, seq, _ = x.shape
    m = batch * seq
    x2d = x.reshape(m, D_MODEL)
    tm = _pick_tile(m, 512)

    tabs = _rope_tables(pos)
    if seq < tm:
        tabs = jnp.tile(tabs, (1, tm // seq, 1))
    ret_in, q_a, k_a, v_a, q_i, kw = _proj(x2d, norm1, w_in_b, tabs, tm)

    y_r, s_new = _retention(ret_in, s0, gn_gain, batch, seq, ret_blk)

    three = lambda a: a.reshape(batch, seq, a.shape[-1])
    k_i = three(kw[:, :IDX_DH])
    w_t = jnp.swapaxes(three(kw[:, IDX_DH:IDX_DH + IDX_HEADS]) * IDX_HEADS ** -0.5 * IDX_DH ** -0.5, 1, 2)
    q_a3, q_i3 = three(q_a), three(q_i)
    if past is None:
        n_keys = seq
        keys_i, keys_a, vals = three(kw), three(k_a), three(v_a)
        qpos = pos
        n_valid_q = attn_qb
    else:
        past_k, past_v, past_ki = past
        n_keys = past_k.shape[1] + seq
        pad_i = jnp.pad(past_ki, ((0, 0), (0, 0), (0, LANES - IDX_DH)))
        keys_i = jnp.concatenate([pad_i, three(kw)], axis=1)
        keys_a = jnp.concatenate([past_k.reshape(batch, -1, ATT_WIDTH), three(k_a)], axis=1)
        vals = jnp.concatenate([past_v.reshape(batch, -1, ATT_WIDTH), three(v_a)], axis=1)
        qpos = pos
        n_valid_q = seq
    topk = min(TOPK_MAX, n_keys // 4)
    tq = -(-seq // attn_qb) * attn_qb
    lp = -(-n_keys // attn_tk) * attn_tk
    padq = lambda a: jnp.pad(a, ((0, 0), (0, tq - seq), (0, 0)))
    padk = lambda a: jnp.pad(a, ((0, 0), (0, lp - n_keys), (0, 0)))
    lim = jnp.minimum((qpos // CHUNK + 1) * CHUNK, n_keys).astype(jnp.int32)
    lim = jnp.pad(lim, (0, tq - seq), constant_values=n_keys).reshape(tq // attn_qb, 1, attn_qb)
    v_t = jnp.transpose(padk(vals).astype(BF16).reshape(batch, lp // attn_tk, attn_tk, ATT_HEADS, ATT_DH),
                        (0, 1, 3, 4, 2))
    ones_rows = jnp.zeros(v_t.shape[:3] + (V_ROWS - ATT_DH, attn_tk), BF16).at[:, :, :, 0, :].set(1.0)
    v_t = jnp.concatenate([v_t, ones_rows], axis=3).reshape(batch, lp // attn_tk, ATT_HEADS * V_ROWS, attn_tk)
    y_a = _attention(
        padq(q_a3), padq(q_i3), jnp.pad(w_t, ((0, 0), (0, 0), (0, tq - seq))), lim,
        padk(keys_i).astype(BF16), padk(keys_a).astype(BF16), v_t,
        qb=attn_qb, tk=attn_tk, n_keys=n_keys, topk=topk, causal=past is None, n_valid_q=n_valid_q)
    y_a = y_a[:, :seq].reshape(m, ATT_WIDTH)

    y = _mlp(x2d, y_r, y_a, w_out_b, norm2, w_up_b, w_down_b, norm_final, tm)
    return (y.reshape(batch, seq, D_MODEL),
            k_a.reshape(batch, seq, ATT_HEADS, ATT_DH),
            v_a.reshape(batch, seq, ATT_HEADS, ATT_DH),
            k_i, s_new)


def kernel(x_prompt, x_sample, cache_k_att, cache_v_att, cache_k_idx, state_ret, norm1, w_in, gn_gain, w_out,
           norm2, w_up, w_down, norm_final):
    depth = norm1.shape[0]
    assert depth == 1, "the final norm is fused into the layer's last kernel"
    past_len = cache_k_att.shape[2]
    pos_p = jnp.arange(x_prompt.shape[1], dtype=jnp.int32)
    pos_s = past_len + jnp.arange(x_sample.shape[1], dtype=jnp.int32)
    l = 0
    w_in_b = jnp.pad(w_in[l], ((0, 0), (0, IN_WIDTH_PADDED - IN_WIDTH))).astype(BF16)
    weights = (norm1[l][None], w_in_b, gn_gain[l][None], w_out[l].astype(BF16), norm2[l][None],
               w_up[l].astype(BF16), w_down[l].astype(BF16), norm_final[None])
    s0_p = jnp.zeros((x_prompt.shape[0], RET_HEADS, RET_DK, RET_DV), F32)
    yp, kap, vap, kip, sp = _layer(x_prompt, pos_p, None, s0_p, 256, weights, attn_qb=256, attn_tk=256)
    past = (cache_k_att[l], cache_v_att[l], cache_k_idx[l])
    ys, kas, vas, kis, ss = _layer(x_sample, pos_s, past, state_ret[l].astype(F32), x_sample.shape[1], weights,
                                   attn_qb=128, attn_tk=256)
    return (yp, ys, kap[None], vap[None], kip[None], sp[None], kas[None], vas[None], kis[None], ss[None])
```

```python
import functools

import jax
import jax.numpy as jnp
import numpy as np
from jax import lax
from jax.experimental import pallas as pl
from jax.experimental.pallas import tpu as pltpu

D_MODEL = 1024
CHUNK = 64
ROPE_THETA = 10000.0
NORM_EPS = 1e-6
NEG = -1e30
RET_WIDTH = 512
ATT_WIDTH = 512
RET_HEADS = 4
RET_DK = 128
RET_DV = 128
ATT_HEADS = 8
ATT_DH = 64
IDX_HEADS = 8
IDX_DH = 64
TOPK_MAX = 256
D_FF = 4 * D_MODEL
IN_WIDTH = 4 * RET_WIDTH + 3 * ATT_WIDTH + IDX_HEADS * IDX_DH + IDX_DH + IDX_HEADS

LANES = 128
IN_WIDTH_PADDED = ((IN_WIDTH + LANES - 1) // LANES) * LANES
VMEM_LIMIT_BYTES = 56 * 1024 * 1024

F32 = jnp.float32
BF16 = jnp.bfloat16

_NT = (((1,), (1,)), ((), ()))


def _compiler_params(semantics):
    return pltpu.CompilerParams(dimension_semantics=semantics, vmem_limit_bytes=VMEM_LIMIT_BYTES)


def _resident(block_shape, index_map):
    return pl.BlockSpec(block_shape, index_map, pipeline_mode=pl.Buffered(1))


def _rope_tables(pos):
    posf = pos.astype(F32)[:, None]

    def angles(d):
        half = d // 2
        inv = ROPE_THETA ** (-jnp.arange(half, dtype=F32) * 2.0 / d)
        ang = posf * inv[None, :]
        return jnp.cos(ang), jnp.sin(ang)

    c128, s128 = angles(128)
    zero64 = jnp.zeros_like(s128)
    cos128 = jnp.concatenate([c128, c128], axis=1)
    sin128 = jnp.concatenate([-s128, s128], axis=1)
    c64, s64 = angles(64)
    zero32 = jnp.zeros_like(s64)
    cos64 = jnp.concatenate([c64, c64, c64, c64], axis=1)
    sin64_up = jnp.concatenate([zero32, s64, zero32, s64], axis=1)
    sin64_dn = jnp.concatenate([-s64, zero32, -s64, zero32], axis=1)
    one64 = jnp.ones_like(zero64)
    cosk = jnp.concatenate([c64, c64, one64], axis=1)
    sink_up = jnp.concatenate([zero32, s64, zero64], axis=1)
    sink_dn = jnp.concatenate([-s64, zero32, zero64], axis=1)
    return jnp.stack([cos128, sin128, cos64, sin64_up, sin64_dn, cosk, sink_up, sink_dn])


def _proj_kernel(x_ref, g_ref, w_ref, tab_ref, ret_ref, qa_ref, ka_ref, va_ref, qi_ref, kw_ref):
    x = x_ref[...]
    ms = jnp.mean(x * x, axis=-1, keepdims=True)
    h = ((x * lax.rsqrt(ms + NORM_EPS)) * g_ref[...]).astype(BF16)

    def mm(c0, n):
        return jnp.dot(h, w_ref[:, c0:c0 + n], preferred_element_type=F32)

    def rope128(z):
        return z * tab_ref[0] + pltpu.roll(z, 64, 1) * tab_ref[1]

    def rope64(z):
        return z * tab_ref[2] + pltpu.roll(z, 32, 1) * tab_ref[3] + pltpu.roll(z, 96, 1) * tab_ref[4]

    def ropek(z):
        return z * tab_ref[5] + pltpu.roll(z, 32, 1) * tab_ref[6] + pltpu.roll(z, 96, 1) * tab_ref[7]

    def emit(out_ref, out_c0, w_c0, rope):
        z = mm(w_c0, 512)
        for t in range(4):
            zt = z[:, t * LANES:(t + 1) * LANES]
            if rope is not None:
                zt = rope(zt)
            out_ref[:, out_c0 + t * LANES:out_c0 + (t + 1) * LANES] = zt.astype(out_ref.dtype)

    emit(ret_ref, 0, 0, rope128)
    emit(ret_ref, 512, 512, rope128)
    emit(ret_ref, 1024, 1024, None)
    emit(ret_ref, 1536, 1536, None)
    emit(qa_ref, 0, 2048, rope64)
    emit(ka_ref, 0, 2560, rope64)
    emit(va_ref, 0, 3072, None)
    emit(qi_ref, 0, 3584, rope64)
    kw_ref[...] = ropek(mm(4096, LANES))


def _proj(x2d, norm_g, w_in_b, tabs, tm):
    m = x2d.shape[0]
    n_tab = tabs.shape[1] // tm
    row = lambda i: (i, 0)
    outs = (
        jax.ShapeDtypeStruct((m, 4 * RET_WIDTH), F32),
        jax.ShapeDtypeStruct((m, ATT_WIDTH), F32),
        jax.ShapeDtypeStruct((m, ATT_WIDTH), F32),
        jax.ShapeDtypeStruct((m, ATT_WIDTH), F32),
        jax.ShapeDtypeStruct((m, IDX_HEADS * IDX_DH), F32),
        jax.ShapeDtypeStruct((m, LANES), F32),
    )
    return pl.pallas_call(
        _proj_kernel,
        grid=(m // tm,),
        in_specs=[
            pl.BlockSpec((tm, D_MODEL), row),
            _resident((1, D_MODEL), lambda i: (0, 0)),
            _resident((D_MODEL, IN_WIDTH_PADDED), lambda i: (0, 0)),
            pl.BlockSpec((8, tm, LANES), lambda i: (0, i % n_tab, 0)),
        ],
        out_specs=[
            pl.BlockSpec((tm, 4 * RET_WIDTH), row),
            pl.BlockSpec((tm, ATT_WIDTH), row),
            pl.BlockSpec((tm, ATT_WIDTH), row),
            pl.BlockSpec((tm, ATT_WIDTH), row),
            pl.BlockSpec((tm, IDX_HEADS * IDX_DH), row),
            pl.BlockSpec((tm, LANES), row),
        ],
        out_shape=outs,
        compiler_params=_compiler_params(("parallel",)),
        name="proj",
    )(x2d, norm_g, w_in_b, tabs)


def _retention_tables(blk):
    lg = jnp.log1p(-jnp.exp2(-5.0 - jnp.arange(RET_HEADS, dtype=F32)))
    i = jnp.arange(blk, dtype=F32)
    diff = i[:, None] - i[None, :]
    dmat = jnp.where(diff[None] >= 0, jnp.exp(lg[:, None, None] * jnp.maximum(diff, 0.0)[None]), 0.0)
    w_end = jnp.exp(lg[:, None] * (blk - 1.0 - i)[None, :])
    w_q = jnp.exp(lg[:, None] * (i + 1.0)[None, :])
    g_block = jnp.exp(lg * blk)
    lane = lambda a: jnp.broadcast_to(a[:, :, None], (RET_HEADS, blk, LANES))
    return dmat, lane(w_q), lane(w_end), g_block


def _retention_kernel(gblk_ref, q_ref, k_ref, v_ref, g_ref, s0_ref, dmat_ref, wq_ref, wend_ref, gain_ref,
                      y_ref, sfin_ref, s_scr):
    t = pl.program_id(1)

    @pl.when(t == 0)
    def _():
        s_scr[...] = s0_ref[0]

    for hd in range(RET_HEADS):
        cols = slice(hd * RET_DK, (hd + 1) * RET_DK)
        q = q_ref[:, cols]
        k = k_ref[:, cols] * RET_DK ** -0.5
        vb = v_ref[:, cols].astype(BF16)
        state = s_scr[hd]
        scores = lax.dot_general(q.astype(BF16), k.astype(BF16), _NT, preferred_element_type=F32) * dmat_ref[hd]
        o = jnp.dot(scores.astype(BF16), vb, preferred_element_type=F32)
        o = o + jnp.dot((q * wq_ref[hd]).astype(BF16), state.astype(BF16), preferred_element_type=F32)
        k_end_t = (k * wend_ref[hd]).T.astype(BF16)
        s_scr[hd] = gblk_ref[hd] * state + jnp.dot(k_end_t, vb, preferred_element_type=F32)
        mu = jnp.mean(o, axis=-1, keepdims=True)
        var = jnp.mean(jnp.square(o - mu), axis=-1, keepdims=True)
        on = ((o - mu) * lax.rsqrt(var + NORM_EPS)) * gain_ref[:, cols]
        g = g_ref[:, cols]
        y_ref[:, cols] = ((g * jax.nn.sigmoid(g)) * on).astype(y_ref.dtype)

    @pl.when(t == pl.num_programs(1) - 1)
    def _():
        sfin_ref[0] = s_scr[...]


def _retention(ret_in, s0, gn_gain, batch, seq, blk):
    nt = seq // blk
    dmat, w_q, w_end, g_block = _retention_tables(blk)

    def col(c):
        return pl.BlockSpec((blk, RET_WIDTH), lambda b, t: (b * nt + t, c))

    const3 = lambda b, t: (0, 0, 0)
    return pl.pallas_call(
        _retention_kernel,
        grid=(batch, nt),
        in_specs=[
            pl.BlockSpec(memory_space=pltpu.SMEM),
            col(0), col(1), col(2), col(3),
            pl.BlockSpec((1, RET_HEADS, RET_DK, RET_DV), lambda b, t: (b, 0, 0, 0)),
            pl.BlockSpec((RET_HEADS, blk, blk), const3),
            pl.BlockSpec((RET_HEADS, blk, LANES), const3),
            pl.BlockSpec((RET_HEADS, blk, LANES), const3),
            pl.BlockSpec((1, RET_WIDTH), lambda b, t: (0, 0)),
        ],
        out_specs=[
            pl.BlockSpec((blk, RET_WIDTH), lambda b, t: (b * nt + t, 0)),
            pl.BlockSpec((1, RET_HEADS, RET_DK, RET_DV), lambda b, t: (b, 0, 0, 0)),
        ],
        out_shape=(
            jax.ShapeDtypeStruct((batch * seq, RET_WIDTH), BF16),
            jax.ShapeDtypeStruct((batch, RET_HEADS, RET_DK, RET_DV), F32),
        ),
        scratch_shapes=[pltpu.VMEM((RET_HEADS, RET_DK, RET_DV), F32)],
        compiler_params=_compiler_params(("parallel", "arbitrary")),
        name="retention",
    )(g_block, ret_in, ret_in, ret_in, ret_in, s0, dmat, w_q, w_end, gn_gain)


COUNT_ROWS = 32
MASKED = 2.0 * NEG
LOG2_E = 1.4426950408889634


def _ordered_bits_to_float(u):
    bits = jnp.where(u < 0, u ^ jnp.int32(-2 ** 31), ~u)
    return lax.bitcast_convert_type(bits, F32)


def _signed_half(x):
    return (x - 2 ** 15).astype(jnp.int16)


def _attention_kernel(lim_ref, qa_ref, qi_ref, wt_ref, ki_ref, ka_ref, vt_ref, o_ref,
                      s_scr, hi16_scr, lo16_scr, xa_scr, xb_scr, acc_scr,
                      *, qb, tk, n_keys, topk, causal, n_valid_q, index_bits):
    n_kt = pl.program_id(1) * (qb // tk) + (qb // tk) if causal else ki_ref.shape[0] // tk
    lim = lim_ref[0]
    lane = lax.broadcasted_iota(jnp.int32, (qb, LANES), 1)
    key_iota = lax.broadcasted_iota(jnp.int32, (tk, qb), 0)

    def head_operand(ref, hd, scale, shift_to_low):
        pair = ref[:, (hd // 2) * LANES:(hd // 2 + 1) * LANES] * scale
        if shift_to_low and hd % 2:
            pair = pltpu.roll(pair, 64, 1)
        keep = (lane < 64) if (shift_to_low or hd % 2 == 0) else (lane >= 64)
        return jnp.where(keep, pair, 0.0).T.astype(BF16)

    qi_heads = [head_operand(qi_ref, hd, 1.0, True) for hd in range(IDX_HEADS)]
    wt = wt_ref[0]

    def score_body(kt, carry):
        r0 = pl.multiple_of(kt * tk, tk)
        keys = ki_ref[pl.ds(r0, tk), :]
        acc = jnp.zeros((tk, qb), F32)
        for hd in range(IDX_HEADS):
            d = jnp.dot(keys, qi_heads[hd], preferred_element_type=F32)
            acc = acc + jnp.maximum(d, 0.0) * wt[hd:hd + 1, :]
        s = jnp.where(key_iota + r0 < lim, acc, -jnp.inf)
        s_scr[pl.ds(r0, tk), :] = s
        bits = lax.bitcast_convert_type(s, jnp.int32)
        ordered = bits ^ (lax.shift_right_arithmetic(bits, 31) | jnp.int32(-2 ** 31))
        hi16_scr[pl.ds(r0, tk), :] = _signed_half(lax.shift_right_logical(ordered, 16))
        lo16_scr[pl.ds(r0, tk), :] = _signed_half(ordered & 0xFFFF)
        return carry

    lax.fori_loop(0, n_kt, score_body, 0)

    def count(src_ref, indicator, dtype):
        def body(kt, acc):
            for part in range(tk // COUNT_ROWS):
                r0 = pl.multiple_of(kt * tk + part * COUNT_ROWS, COUNT_ROWS)
                acc = acc + indicator(src_ref[pl.ds(r0, COUNT_ROWS), :], r0)
            return acc
        acc = lax.fori_loop(0, n_kt, body, jnp.zeros((COUNT_ROWS, qb), dtype))
        return jnp.sum(acc.astype(F32), axis=0, keepdims=True)

    n_masked = (n_keys - lim).astype(F32)
    kf = float(topk)
    q_lane = lax.broadcasted_iota(jnp.int32, (1, qb), 1)
    one16 = jnp.ones((), BF16)
    zero16 = jnp.zeros((), BF16)

    def bisect_step(it, carry, half_ref):
        cur, cnt = carry
        cand = cur | lax.shift_left(jnp.int32(1), 31 - it)
        if half_ref is hi16_scr:
            half = _signed_half(lax.shift_right_logical(cand, 16))
        else:
            half = _signed_half(cand & 0xFFFF)
        c = count(half_ref, lambda h, r0: jnp.where(h >= half, one16, zero16), BF16)
        ok = c + jnp.where(_ordered_bits_to_float(cand) <= NEG, n_masked, 0.0) >= kf
        return jnp.where(ok, cand, cur), jnp.where(ok, c, cnt)

    carry = (jnp.zeros((1, qb), jnp.int32), jnp.zeros((1, qb), F32))
    carry = lax.fori_loop(0, 16, lambda it, st: bisect_step(it, st, hi16_scr), carry)

    top = _signed_half(lax.shift_right_logical(carry[0], 16))

    def narrow_body(kt, c):
        r0 = pl.multiple_of(kt * tk, tk)
        hi = hi16_scr[pl.ds(r0, tk), :]
        lo = lo16_scr[pl.ds(r0, tk), :]
        lo16_scr[pl.ds(r0, tk), :] = jnp.where(hi > top, jnp.int16(2 ** 15 - 1),
                                               jnp.where(hi == top, lo, jnp.int16(-2 ** 15)))
        return c

    lax.fori_loop(0, n_kt, narrow_body, 0)
    cur, cnt_ge = lax.fori_loop(16, 32, lambda it, st: bisect_step(it, st, lo16_scr), carry)
    theta = _ordered_bits_to_float(cur)

    overflow = jnp.where((cnt_ge > kf) & (q_lane < n_valid_q), 1.0, 0.0)
    all_idx = jnp.full((1, qb), 2 ** index_bits - 1, jnp.int32)

    def tie_cut():
        need = kf - count(s_scr, lambda s, r0: jnp.where(s > theta, 1.0, 0.0), F32)
        rows = lax.broadcasted_iota(jnp.int32, (COUNT_ROWS, qb), 0)

        def body(it, cut):
            cand = cut | lax.shift_left(jnp.int32(1), index_bits - 1 - it)
            c = count(s_scr, lambda s, r0: jnp.where(s == theta, jnp.where(rows + r0 < cand, 1.0, 0.0), 0.0), F32)
            return jnp.where(c <= need, cand, cut)

        return lax.fori_loop(0, index_bits, body, jnp.zeros((1, qb), jnp.int32))

    idx_cut = lax.cond(jnp.max(overflow) > 0.0, tie_cut, lambda: all_idx)

    def bias_body(kt, carry):
        r0 = pl.multiple_of(kt * tk, tk)
        s = s_scr[pl.ds(r0, tk), :]
        tie = jnp.where(key_iota + r0 < idx_cut, 0.0, MASKED)
        s_scr[pl.ds(r0, tk), :] = jnp.where(s > theta, 0.0, jnp.where(s == theta, tie, MASKED))
        return carry

    lax.fori_loop(0, n_kt, bias_body, 0)

    lp = ki_ref.shape[0]
    s_scr[pl.ds(lp, tk), :] = jnp.full((tk, qb), MASKED, F32)
    qa_heads = [head_operand(qa_ref, hd, ATT_DH ** -0.5 * LOG2_E, False) for hd in range(ATT_HEADS)]
    acc_scr[...] = jnp.zeros(acc_scr.shape, F32)

    def logits_stage(kt, x_ref):
        kt_in = jnp.minimum(kt, n_kt - 1)
        r0 = pl.multiple_of(kt_in * tk, tk)
        b0 = pl.multiple_of(jnp.where(kt < n_kt, kt_in * tk, lp), tk)
        tops = []
        for hd in range(ATT_HEADS):
            keys = ka_ref[pl.ds(r0, tk), (hd // 2) * LANES:(hd // 2 + 1) * LANES]
            x = jnp.dot(keys, qa_heads[hd], preferred_element_type=F32) + s_scr[pl.ds(b0, tk), :]
            x_ref[hd] = x.astype(x_ref.dtype)
            tops.append(jnp.max(x, axis=0, keepdims=True))
        return jnp.concatenate(tops, axis=0).astype(x_ref.dtype).astype(F32)

    def accumulate(kt, x_ref, tile_max, m_run, l_run):
        kt_in = jnp.minimum(kt, n_kt - 1)
        m_new = jnp.maximum(m_run, tile_max)
        alpha = jnp.exp2(m_run - m_new)
        m_staged = m_new.astype(x_ref.dtype)
        sums = []
        for hd in range(ATT_HEADS):
            p = jnp.exp2(x_ref[hd] - m_staged[hd:hd + 1, :]).astype(BF16)
            sums.append(jnp.sum(p.astype(F32), axis=0, keepdims=True))
            rows = slice(hd * ATT_DH, (hd + 1) * ATT_DH)
            pv = jnp.dot(vt_ref[kt_in, rows, :], p, preferred_element_type=F32)
            acc_scr[rows, :] = alpha[hd:hd + 1, :] * acc_scr[rows, :] + pv
        return m_new, alpha * l_run + jnp.concatenate(sums, axis=0)

    def pair_body(pair, carry):
        max_a, m_run, l_run = carry
        max_b = logits_stage(2 * pair + 1, xb_scr)
        m_run, l_run = accumulate(2 * pair, xa_scr, max_a, m_run, l_run)
        max_a = logits_stage(2 * pair + 2, xa_scr)
        m_run, l_run = accumulate(2 * pair + 1, xb_scr, max_b, m_run, l_run)
        return max_a, m_run, l_run

    first_max = logits_stage(0, xa_scr)
    _, _, l_fin = lax.fori_loop(
        0, (n_kt + 1) // 2, pair_body,
        (first_max, jnp.full((ATT_HEADS, qb), NEG, F32), jnp.zeros((ATT_HEADS, qb), F32)))

    for hd in range(ATT_HEADS):
        rows = slice(hd * ATT_DH, (hd + 1) * ATT_DH)
        acc_scr[rows, :] = acc_scr[rows, :] / l_fin[hd:hd + 1, :]
    o_ref[...] = acc_scr[...].T.astype(o_ref.dtype)


def _attention(q_a, q_i, w_t, lim, k_i, k_a, v_t, *, qb, tk, n_keys, topk, causal, n_valid_q):
    batch, tq, _ = q_a.shape
    lp = k_i.shape[1]
    nq = tq // qb
    index_bits = int(np.ceil(np.log2(lp + 1)))
    assert lp // COUNT_ROWS <= 256, "bfloat16 partial counts are exact only up to 256"
    kernel = functools.partial(_attention_kernel, qb=qb, tk=tk, n_keys=n_keys, topk=topk, causal=causal,
                               n_valid_q=n_valid_q, index_bits=index_bits)
    return pl.pallas_call(
        kernel,
        grid=(batch, nq),
        in_specs=[
            pl.BlockSpec((1, 1, qb), lambda b, i: (i, 0, 0)),
            pl.BlockSpec((None, qb, ATT_WIDTH), lambda b, i: (b, i, 0)),
            pl.BlockSpec((None, qb, IDX_HEADS * IDX_DH), lambda b, i: (b, i, 0)),
            pl.BlockSpec((1, IDX_HEADS, qb), lambda b, i: (b, 0, i)),
            _resident((None, lp, LANES), lambda b, i: (b, 0, 0)),
            _resident((None, lp, ATT_WIDTH), lambda b, i: (b, 0, 0)),
            _resident((None, lp // tk, ATT_WIDTH, tk), lambda b, i: (b, 0, 0, 0)),
        ],
        out_specs=pl.BlockSpec((None, qb, ATT_WIDTH), lambda b, i: (b, i, 0)),
        out_shape=jax.ShapeDtypeStruct((batch, tq, ATT_WIDTH), BF16),
        scratch_shapes=[
            pltpu.VMEM((lp + tk, qb), F32),
            pltpu.VMEM((lp, qb), jnp.int16),
            pltpu.VMEM((lp, qb), jnp.int16),
            pltpu.VMEM((ATT_HEADS, tk, qb), BF16),
            pltpu.VMEM((ATT_HEADS, tk, qb), BF16),
            pltpu.VMEM((ATT_WIDTH, qb), F32),
        ],
        compiler_params=_compiler_params(("parallel", "arbitrary")),
        name="attention",
    )(lim, q_a, q_i, w_t, k_i, k_a, v_t)


FF_CHUNK = 512


def _mlp_kernel(x_ref, yr_ref, ya_ref, wo_ref, g2_ref, wup_ref, wdn_ref, gf_ref, y_ref, x_scr, h_scr, a_scr):
    x = x_ref[...] + jnp.dot(yr_ref[...], wo_ref[:RET_WIDTH, :], preferred_element_type=F32)
    x = x + jnp.dot(ya_ref[...], wo_ref[RET_WIDTH:, :], preferred_element_type=F32)
    x_scr[...] = x
    ms = jnp.mean(x * x, axis=-1, keepdims=True)
    h_scr[...] = ((x * lax.rsqrt(ms + NORM_EPS)) * g2_ref[...]).astype(h_scr.dtype)
    for c in range(D_FF // FF_CHUNK):
        cols = slice(c * FF_CHUNK, (c + 1) * FF_CHUNK)
        u = jnp.dot(h_scr[...], wup_ref[:, cols], preferred_element_type=F32)
        a_scr[:, cols] = jnp.square(jnp.maximum(u, 0.0)).astype(a_scr.dtype)
    x = x_scr[...] + jnp.dot(a_scr[...], wdn_ref[...], preferred_element_type=F32)
    ms = jnp.mean(x * x, axis=-1, keepdims=True)
    y_ref[...] = (x * lax.rsqrt(ms + NORM_EPS)) * gf_ref[...]


def _mlp(x2d, y_r, y_a, w_out_b, norm2, w_up_b, w_down_b, norm_final, tm):
    m = x2d.shape[0]
    row = lambda i: (i, 0)
    const = lambda i: (0, 0)
    return pl.pallas_call(
        _mlp_kernel,
        grid=(m // tm,),
        in_specs=[
            pl.BlockSpec((tm, D_MODEL), row),
            pl.BlockSpec((tm, RET_WIDTH), row),
            pl.BlockSpec((tm, ATT_WIDTH), row),
            _resident((D_MODEL, D_MODEL), const),
            _resident((1, D_MODEL), const),
            _resident((D_MODEL, D_FF), const),
            _resident((D_FF, D_MODEL), const),
            _resident((1, D_MODEL), const),
        ],
        out_specs=pl.BlockSpec((tm, D_MODEL), row),
        out_shape=jax.ShapeDtypeStruct((m, D_MODEL), F32),
        scratch_shapes=[pltpu.VMEM((tm, D_MODEL), F32), pltpu.VMEM((tm, D_MODEL), BF16),
                        pltpu.VMEM((tm, D_FF), BF16)],
        compiler_params=_compiler_params(("parallel",)),
        name="mlp",
    )(x2d, y_r, y_a, w_out_b, norm2, w_up_b, w_down_b, norm_final)


def _pick_tile(n, target):
    t = min(n, target)
    while n % t:
        t //= 2
    return t


def _layer(x, pos, past, s0, ret_blk, weights, *, attn_qb, attn_tk):
    norm1, w_in_b, gn_gain, w_out_b, norm2, w_up_b, w_down_b, norm_final = weights
    batch, seq, _ = x.shape
    m = batch * seq
    x2d = x.reshape(m, D_MODEL)
    tm = _pick_tile(m, 512)

    tabs = _rope_tables(pos)
    if seq < tm:
        tabs = jnp.tile(tabs, (1, tm // seq, 1))
    ret_in, q_a, k_a, v_a, q_i, kw = _proj(x2d, norm1, w_in_b, tabs, tm)

    y_r, s_new = _retention(ret_in, s0, gn_gain, batch, seq, ret_blk)

    three = lambda a: a.reshape(batch, seq, a.shape[-1])
    k_i = three(kw[:, :IDX_DH])
    w_t = jnp.swapaxes(three(kw[:, IDX_DH:IDX_DH + IDX_HEADS]) * IDX_HEADS ** -0.5 * IDX_DH ** -0.5, 1, 2)
    q_a3, q_i3 = three(q_a), three(q_i)
    if past is None:
        n_keys = seq
        keys_i, keys_a, vals = three(kw), three(k_a), three(v_a)
        qpos = pos
        n_valid_q = attn_qb
    else:
        past_k, past_v, past_ki = past
        n_keys = past_k.shape[1] + seq
        pad_i = jnp.pad(past_ki, ((0, 0), (0, 0), (0, LANES - IDX_DH)))
        keys_i = jnp.concatenate([pad_i, three(kw)], axis=1)
        keys_a = jnp.concatenate([past_k.reshape(batch, -1, ATT_WIDTH), three(k_a)], axis=1)
        vals = jnp.concatenate([past_v.reshape(batch, -1, ATT_WIDTH), three(v_a)], axis=1)
        qpos = pos
        n_valid_q = seq
    topk = min(TOPK_MAX, n_keys // 4)
    tq = -(-seq // attn_qb) * attn_qb
    lp = -(-n_keys // attn_tk) * attn_tk
    padq = lambda a: jnp.pad(a, ((0, 0), (0, tq - seq), (0, 0)))
    padk = lambda a: jnp.pad(a, ((0, 0), (0, lp - n_keys), (0, 0)))
    lim = jnp.minimum((qpos // CHUNK + 1) * CHUNK, n_keys).astype(jnp.int32)
    lim = jnp.pad(lim, (0, tq - seq), constant_values=n_keys).reshape(tq // attn_qb, 1, attn_qb)
    v_t = jnp.swapaxes(padk(vals).astype(BF16).reshape(batch, lp // attn_tk, attn_tk, ATT_WIDTH), 2, 3)
    y_a = _attention(
        padq(q_a3), padq(q_i3), jnp.pad(w_t, ((0, 0), (0, 0), (0, tq - seq))), lim,
        padk(keys_i).astype(BF16), padk(keys_a).astype(BF16), v_t,
        qb=attn_qb, tk=attn_tk, n_keys=n_keys, topk=topk, causal=past is None, n_valid_q=n_valid_q)
    y_a = y_a[:, :seq].reshape(m, ATT_WIDTH)

    y = _mlp(x2d, y_r, y_a, w_out_b, norm2, w_up_b, w_down_b, norm_final, tm)
    return (y.reshape(batch, seq, D_MODEL),
            k_a.reshape(batch, seq, ATT_HEADS, ATT_DH),
            v_a.reshape(batch, seq, ATT_HEADS, ATT_DH),
            k_i, s_new)


def kernel(x_prompt, x_sample, cache_k_att, cache_v_att, cache_k_idx, state_ret, norm1, w_in, gn_gain, w_out,
           norm2, w_up, w_down, norm_final):
    depth = norm1.shape[0]
    assert depth == 1, "the final norm is fused into the layer's last kernel"
    past_len = cache_k_att.shape[2]
    pos_p = jnp.arange(x_prompt.shape[1], dtype=jnp.int32)
    pos_s = past_len + jnp.arange(x_sample.shape[1], dtype=jnp.int32)
    l = 0
    w_in_b = jnp.pad(w_in[l], ((0, 0), (0, IN_WIDTH_PADDED - IN_WIDTH))).astype(BF16)
    weights = (norm1[l][None], w_in_b, gn_gain[l][None], w_out[l].astype(BF16), norm2[l][None],
               w_up[l].astype(BF16), w_down[l].astype(BF16), norm_final[None])
    s0_p = jnp.zeros((x_prompt.shape[0], RET_HEADS, RET_DK, RET_DV), F32)
    yp, kap, vap, kip, sp = _layer(x_prompt, pos_p, None, s0_p, 256, weights, attn_qb=256, attn_tk=256)
    past = (cache_k_att[l], cache_v_att[l], cache_k_idx[l])
    ys, kas, vas, kis, ss = _layer(x_sample, pos_s, past, state_ret[l].astype(F32), x_sample.shape[1], weights,
                                   attn_qb=128, attn_tk=256)
    return (yp, ys, kap[None], vap[None], kip[None], sp[None], kas[None], vas[None], kis[None], ss[None])
```

```python
import functools

import jax
import jax.numpy as jnp
import numpy as np
from jax import lax
from jax.experimental import pallas as pl
from jax.experimental.pallas import tpu as pltpu

D_MODEL = 1024
CHUNK = 64
ROPE_THETA = 10000.0
NORM_EPS = 1e-6
NEG = -1e30
RET_WIDTH = 512
ATT_WIDTH = 512
RET_HEADS = 4
RET_DK = 128
RET_DV = 128
ATT_HEADS = 8
ATT_DH = 64
IDX_HEADS = 8
IDX_DH = 64
TOPK_MAX = 256
D_FF = 4 * D_MODEL
IN_WIDTH = 4 * RET_WIDTH + 3 * ATT_WIDTH + IDX_HEADS * IDX_DH + IDX_DH + IDX_HEADS

LANES = 128
IN_WIDTH_PADDED = ((IN_WIDTH + LANES - 1) // LANES) * LANES
VMEM_LIMIT_BYTES = 56 * 1024 * 1024

F32 = jnp.float32
BF16 = jnp.bfloat16

_NT = (((1,), (1,)), ((), ()))


def _compiler_params(semantics):
    return pltpu.CompilerParams(dimension_semantics=semantics, vmem_limit_bytes=VMEM_LIMIT_BYTES)


def _resident(block_shape, index_map):
    return pl.BlockSpec(block_shape, index_map, pipeline_mode=pl.Buffered(1))


def _rope_tables(pos):
    posf = pos.astype(F32)[:, None]

    def angles(d):
        half = d // 2
        inv = ROPE_THETA ** (-jnp.arange(half, dtype=F32) * 2.0 / d)
        ang = posf * inv[None, :]
        return jnp.cos(ang), jnp.sin(ang)

    c128, s128 = angles(128)
    zero64 = jnp.zeros_like(s128)
    cos128 = jnp.concatenate([c128, c128], axis=1)
    sin128 = jnp.concatenate([-s128, s128], axis=1)
    c64, s64 = angles(64)
    zero32 = jnp.zeros_like(s64)
    cos64 = jnp.concatenate([c64, c64, c64, c64], axis=1)
    sin64_up = jnp.concatenate([zero32, s64, zero32, s64], axis=1)
    sin64_dn = jnp.concatenate([-s64, zero32, -s64, zero32], axis=1)
    one64 = jnp.ones_like(zero64)
    cosk = jnp.concatenate([c64, c64, one64], axis=1)
    sink_up = jnp.concatenate([zero32, s64, zero64], axis=1)
    sink_dn = jnp.concatenate([-s64, zero32, zero64], axis=1)
    return jnp.stack([cos128, sin128, cos64, sin64_up, sin64_dn, cosk, sink_up, sink_dn])


def _proj_kernel(x_ref, g_ref, w_ref, tab_ref, ret_ref, qa_ref, ka_ref, va_ref, qi_ref, kw_ref):
    x = x_ref[...]
    ms = jnp.mean(x * x, axis=-1, keepdims=True)
    h = ((x * lax.rsqrt(ms + NORM_EPS)) * g_ref[...]).astype(BF16)

    def mm(c0, n):
        return jnp.dot(h, w_ref[:, c0:c0 + n], preferred_element_type=F32)

    def rope128(z):
        return z * tab_ref[0] + pltpu.roll(z, 64, 1) * tab_ref[1]

    def rope64(z):
        return z * tab_ref[2] + pltpu.roll(z, 32, 1) * tab_ref[3] + pltpu.roll(z, 96, 1) * tab_ref[4]

    def ropek(z):
        return z * tab_ref[5] + pltpu.roll(z, 32, 1) * tab_ref[6] + pltpu.roll(z, 96, 1) * tab_ref[7]

    def emit(out_ref, out_c0, w_c0, rope):
        z = mm(w_c0, 512)
        for t in range(4):
            zt = z[:, t * LANES:(t + 1) * LANES]
            if rope is not None:
                zt = rope(zt)
            out_ref[:, out_c0 + t * LANES:out_c0 + (t + 1) * LANES] = zt.astype(out_ref.dtype)

    emit(ret_ref, 0, 0, rope128)
    emit(ret_ref, 512, 512, rope128)
    emit(ret_ref, 1024, 1024, None)
    emit(ret_ref, 1536, 1536, None)
    emit(qa_ref, 0, 2048, rope64)
    emit(ka_ref, 0, 2560, rope64)
    emit(va_ref, 0, 3072, None)
    emit(qi_ref, 0, 3584, rope64)
    kw_ref[...] = ropek(mm(4096, LANES))


def _proj(x2d, norm_g, w_in_b, tabs, tm):
    m = x2d.shape[0]
    n_tab = tabs.shape[1] // tm
    row = lambda i: (i, 0)
    outs = (
        jax.ShapeDtypeStruct((m, 4 * RET_WIDTH), F32),
        jax.ShapeDtypeStruct((m, ATT_WIDTH), F32),
        jax.ShapeDtypeStruct((m, ATT_WIDTH), F32),
        jax.ShapeDtypeStruct((m, ATT_WIDTH), F32),
        jax.ShapeDtypeStruct((m, IDX_HEADS * IDX_DH), F32),
        jax.ShapeDtypeStruct((m, LANES), F32),
    )
    return pl.pallas_call(
        _proj_kernel,
        grid=(m // tm,),
        in_specs=[
            pl.BlockSpec((tm, D_MODEL), row),
            _resident((1, D_MODEL), lambda i: (0, 0)),
            _resident((D_MODEL, IN_WIDTH_PADDED), lambda i: (0, 0)),
            pl.BlockSpec((8, tm, LANES), lambda i: (0, i % n_tab, 0)),
        ],
        out_specs=[
            pl.BlockSpec((tm, 4 * RET_WIDTH), row),
            pl.BlockSpec((tm, ATT_WIDTH), row),
            pl.BlockSpec((tm, ATT_WIDTH), row),
            pl.BlockSpec((tm, ATT_WIDTH), row),
            pl.BlockSpec((tm, IDX_HEADS * IDX_DH), row),
            pl.BlockSpec((tm, LANES), row),
        ],
        out_shape=outs,
        compiler_params=_compiler_params(("parallel",)),
        name="proj",
    )(x2d, norm_g, w_in_b, tabs)


def _retention_tables(blk):
    lg = jnp.log1p(-jnp.exp2(-5.0 - jnp.arange(RET_HEADS, dtype=F32)))
    i = jnp.arange(blk, dtype=F32)
    diff = i[:, None] - i[None, :]
    dmat = jnp.where(diff[None] >= 0, jnp.exp(lg[:, None, None] * jnp.maximum(diff, 0.0)[None]), 0.0)
    w_end = jnp.exp(lg[:, None] * (blk - 1.0 - i)[None, :])
    w_q = jnp.exp(lg[:, None] * (i + 1.0)[None, :])
    g_block = jnp.exp(lg * blk)
    lane = lambda a: jnp.broadcast_to(a[:, :, None], (RET_HEADS, blk, LANES))
    return dmat, lane(w_q), lane(w_end), g_block


def _retention_kernel(gblk_ref, q_ref, k_ref, v_ref, g_ref, s0_ref, dmat_ref, wq_ref, wend_ref, gain_ref,
                      y_ref, sfin_ref, s_scr):
    t = pl.program_id(1)

    @pl.when(t == 0)
    def _():
        s_scr[...] = s0_ref[0]

    for hd in range(RET_HEADS):
        cols = slice(hd * RET_DK, (hd + 1) * RET_DK)
        q = q_ref[:, cols]
        k = k_ref[:, cols] * RET_DK ** -0.5
        vb = v_ref[:, cols].astype(BF16)
        state = s_scr[hd]
        scores = lax.dot_general(q.astype(BF16), k.astype(BF16), _NT, preferred_element_type=F32) * dmat_ref[hd]
        o = jnp.dot(scores.astype(BF16), vb, preferred_element_type=F32)
        o = o + jnp.dot((q * wq_ref[hd]).astype(BF16), state.astype(BF16), preferred_element_type=F32)
        k_end_t = (k * wend_ref[hd]).T.astype(BF16)
        s_scr[hd] = gblk_ref[hd] * state + jnp.dot(k_end_t, vb, preferred_element_type=F32)
        mu = jnp.mean(o, axis=-1, keepdims=True)
        var = jnp.mean(jnp.square(o - mu), axis=-1, keepdims=True)
        on = ((o - mu) * lax.rsqrt(var + NORM_EPS)) * gain_ref[:, cols]
        g = g_ref[:, cols]
        y_ref[:, cols] = ((g * jax.nn.sigmoid(g)) * on).astype(y_ref.dtype)

    @pl.when(t == pl.num_programs(1) - 1)
    def _():
        sfin_ref[0] = s_scr[...]


def _retention(ret_in, s0, gn_gain, batch, seq, blk):
    nt = seq // blk
    dmat, w_q, w_end, g_block = _retention_tables(blk)

    def col(c):
        return pl.BlockSpec((blk, RET_WIDTH), lambda b, t: (b * nt + t, c))

    const3 = lambda b, t: (0, 0, 0)
    return pl.pallas_call(
        _retention_kernel,
        grid=(batch, nt),
        in_specs=[
            pl.BlockSpec(memory_space=pltpu.SMEM),
            col(0), col(1), col(2), col(3),
            pl.BlockSpec((1, RET_HEADS, RET_DK, RET_DV), lambda b, t: (b, 0, 0, 0)),
            pl.BlockSpec((RET_HEADS, blk, blk), const3),
            pl.BlockSpec((RET_HEADS, blk, LANES), const3),
            pl.BlockSpec((RET_HEADS, blk, LANES), const3),
            pl.BlockSpec((1, RET_WIDTH), lambda b, t: (0, 0)),
        ],
        out_specs=[
            pl.BlockSpec((blk, RET_WIDTH), lambda b, t: (b * nt + t, 0)),
            pl.BlockSpec((1, RET_HEADS, RET_DK, RET_DV), lambda b, t: (b, 0, 0, 0)),
        ],
        out_shape=(
            jax.ShapeDtypeStruct((batch * seq, RET_WIDTH), BF16),
            jax.ShapeDtypeStruct((batch, RET_HEADS, RET_DK, RET_DV), F32),
        ),
        scratch_shapes=[pltpu.VMEM((RET_HEADS, RET_DK, RET_DV), F32)],
        compiler_params=_compiler_params(("parallel", "arbitrary")),
        name="retention",
    )(g_block, ret_in, ret_in, ret_in, ret_in, s0, dmat, w_q, w_end, gn_gain)


COUNT_ROWS = 32
MASKED = 2.0 * NEG
LOG2_E = 1.4426950408889634


def _ordered_bits_to_float(u):
    bits = jnp.where(u < 0, u ^ jnp.int32(-2 ** 31), ~u)
    return lax.bitcast_convert_type(bits, F32)


def _signed_half(x):
    return (x - 2 ** 15).astype(jnp.int16)


def _attention_kernel(lim_ref, qa_ref, qi_ref, wt_ref, ki_ref, ka_ref, vt_ref, o_ref,
                      s_scr, hi16_scr, lo16_scr, xa_scr, xb_scr, acc_scr,
                      *, qb, tk, n_keys, topk, causal, n_valid_q, index_bits):
    n_kt = pl.program_id(1) * (qb // tk) + (qb // tk) if causal else ki_ref.shape[0] // tk
    lim = lim_ref[0]
    lane = lax.broadcasted_iota(jnp.int32, (qb, LANES), 1)
    key_iota = lax.broadcasted_iota(jnp.int32, (tk, qb), 0)

    def head_operand(ref, hd, scale, shift_to_low):
        pair = ref[:, (hd // 2) * LANES:(hd // 2 + 1) * LANES] * scale
        if shift_to_low and hd % 2:
            pair = pltpu.roll(pair, 64, 1)
        keep = (lane < 64) if (shift_to_low or hd % 2 == 0) else (lane >= 64)
        return jnp.where(keep, pair, 0.0).T.astype(BF16)

    qi_heads = [head_operand(qi_ref, hd, 1.0, True) for hd in range(IDX_HEADS)]
    wt = wt_ref[0]

    def score_body(kt, carry):
        r0 = pl.multiple_of(kt * tk, tk)
        keys = ki_ref[pl.ds(r0, tk), :]
        acc = jnp.zeros((tk, qb), F32)
        for hd in range(IDX_HEADS):
            d = jnp.dot(keys, qi_heads[hd], preferred_element_type=F32)
            acc = acc + jnp.maximum(d, 0.0) * wt[hd:hd + 1, :]
        acc = jnp.where(acc == 0.0, 0.0, acc)
        s = jnp.where(key_iota + r0 < lim, acc, -jnp.inf)
        s_scr[pl.ds(r0, tk), :] = s
        bits = lax.bitcast_convert_type(s, jnp.int32)
        ordered = bits ^ (lax.shift_right_arithmetic(bits, 31) | jnp.int32(-2 ** 31))
        hi16_scr[pl.ds(r0, tk), :] = _signed_half(lax.shift_right_logical(ordered, 16))
        lo16_scr[pl.ds(r0, tk), :] = _signed_half(ordered & 0xFFFF)
        return carry

    lax.fori_loop(0, n_kt, score_body, 0)

    def count(src_ref, indicator, dtype):
        def body(kt, acc):
            for part in range(tk // COUNT_ROWS):
                r0 = pl.multiple_of(kt * tk + part * COUNT_ROWS, COUNT_ROWS)
                acc = acc + indicator(src_ref[pl.ds(r0, COUNT_ROWS), :], r0)
            return acc
        acc = lax.fori_loop(0, n_kt, body, jnp.zeros((COUNT_ROWS, qb), dtype))
        return jnp.sum(acc.astype(F32), axis=0, keepdims=True)

    n_masked = (n_keys - lim).astype(F32)
    kf = float(topk)
    q_lane = lax.broadcasted_iota(jnp.int32, (1, qb), 1)
    one16 = jnp.ones((), BF16)
    zero16 = jnp.zeros((), BF16)

    def bisect_step(it, carry, half_ref):
        cur, cnt = carry
        cand = cur | lax.shift_left(jnp.int32(1), 31 - it)
        if half_ref is hi16_scr:
            half = _signed_half(lax.shift_right_logical(cand, 16))
        else:
            half = _signed_half(cand & 0xFFFF)
        c = count(half_ref, lambda h, r0: jnp.where(h >= half, one16, zero16), BF16)
        ok = c + jnp.where(_ordered_bits_to_float(cand) <= NEG, n_masked, 0.0) >= kf
        return jnp.where(ok, cand, cur), jnp.where(ok, c, cnt)

    carry = (jnp.zeros((1, qb), jnp.int32), jnp.zeros((1, qb), F32))
    carry = lax.fori_loop(0, 16, lambda it, st: bisect_step(it, st, hi16_scr), carry)

    top = _signed_half(lax.shift_right_logical(carry[0], 16))

    def narrow_body(kt, c):
        r0 = pl.multiple_of(kt * tk, tk)
        hi = hi16_scr[pl.ds(r0, tk), :]
        lo = lo16_scr[pl.ds(r0, tk), :]
        lo16_scr[pl.ds(r0, tk), :] = jnp.where(hi > top, jnp.int16(2 ** 15 - 1),
                                               jnp.where(hi == top, lo, jnp.int16(-2 ** 15)))
        return c

    lax.fori_loop(0, n_kt, narrow_body, 0)
    cur, cnt_ge = lax.fori_loop(16, 32, lambda it, st: bisect_step(it, st, lo16_scr), carry)
    theta = _ordered_bits_to_float(cur)

    overflow = jnp.where((cnt_ge > kf) & (q_lane < n_valid_q), 1.0, 0.0)
    all_idx = jnp.full((1, qb), 2 ** index_bits - 1, jnp.int32)

    def tie_cut():
        need = kf - count(s_scr, lambda s, r0: jnp.where(s > theta, 1.0, 0.0), F32)

        def mark_body(kt, c):
            r0 = pl.multiple_of(kt * tk, tk)
            tied = jnp.where(s_scr[pl.ds(r0, tk), :] == theta, key_iota + r0, 2 ** 15 - 1)
            hi16_scr[pl.ds(r0, tk), :] = tied.astype(jnp.int16)
            return c

        lax.fori_loop(0, n_kt, mark_body, 0)

        def body(it, cut):
            cand = cut | lax.shift_left(jnp.int32(1), index_bits - 1 - it)
            cand16 = cand.astype(jnp.int16)
            c = count(hi16_scr, lambda idx, r0: jnp.where(idx < cand16, one16, zero16), BF16)
            return jnp.where(c <= need, cand, cut)

        return lax.fori_loop(0, index_bits, body, jnp.zeros((1, qb), jnp.int32))

    any_overflow = jnp.max(overflow) > 0.0
    idx_cut = lax.cond(any_overflow, tie_cut, lambda: all_idx)

    def bias_body(kt, carry, with_ties):
        r0 = pl.multiple_of(kt * tk, tk)
        s = s_scr[pl.ds(r0, tk), :]
        if with_ties:
            tie = jnp.where(key_iota + r0 < idx_cut, 0.0, MASKED)
            s_scr[pl.ds(r0, tk), :] = jnp.where(s > theta, 0.0, jnp.where(s == theta, tie, MASKED))
        else:
            s_scr[pl.ds(r0, tk), :] = jnp.where(s >= theta, 0.0, MASKED)
        return carry

    lax.cond(any_overflow,
             lambda: lax.fori_loop(0, n_kt, functools.partial(bias_body, with_ties=True), 0),
             lambda: lax.fori_loop(0, n_kt, functools.partial(bias_body, with_ties=False), 0))

    lp = ki_ref.shape[0]
    s_scr[pl.ds(lp, tk), :] = jnp.full((tk, qb), MASKED, F32)
    qa_heads = [head_operand(qa_ref, hd, ATT_DH ** -0.5 * LOG2_E, False) for hd in range(ATT_HEADS)]
    acc_scr[...] = jnp.zeros(acc_scr.shape, F32)

    def logits_stage(kt, x_ref):
        kt_in = jnp.minimum(kt, n_kt - 1)
        r0 = pl.multiple_of(kt_in * tk, tk)
        b0 = pl.multiple_of(jnp.where(kt < n_kt, kt_in * tk, lp), tk)
        tops = []
        for hd in range(ATT_HEADS):
            keys = ka_ref[pl.ds(r0, tk), (hd // 2) * LANES:(hd // 2 + 1) * LANES]
            x = jnp.dot(keys, qa_heads[hd], preferred_element_type=F32) + s_scr[pl.ds(b0, tk), :]
            x_ref[hd] = x.astype(x_ref.dtype)
            tops.append(jnp.max(x, axis=0, keepdims=True))
        return jnp.concatenate(tops, axis=0).astype(x_ref.dtype).astype(F32)

    def accumulate(kt, x_ref, tile_max, m_run, l_run):
        kt_in = jnp.minimum(kt, n_kt - 1)
        m_new = jnp.maximum(m_run, tile_max)
        alpha = jnp.exp2(m_run - m_new)
        m_staged = m_new.astype(x_ref.dtype)
        sums = []
        for hd in range(ATT_HEADS):
            p = jnp.exp2(x_ref[hd] - m_staged[hd:hd + 1, :]).astype(BF16)
            sums.append(jnp.sum(p.astype(F32), axis=0, keepdims=True))
            rows = slice(hd * ATT_DH, (hd + 1) * ATT_DH)
            pv = jnp.dot(vt_ref[kt_in, rows, :], p, preferred_element_type=F32)
            acc_scr[rows, :] = alpha[hd:hd + 1, :] * acc_scr[rows, :] + pv
        return m_new, alpha * l_run + jnp.concatenate(sums, axis=0)

    def pair_body(pair, carry):
        max_a, m_run, l_run = carry
        max_b = logits_stage(2 * pair + 1, xb_scr)
        m_run, l_run = accumulate(2 * pair, xa_scr, max_a, m_run, l_run)
        max_a = logits_stage(2 * pair + 2, xa_scr)
        m_run, l_run = accumulate(2 * pair + 1, xb_scr, max_b, m_run, l_run)
        return max_a, m_run, l_run

    first_max = logits_stage(0, xa_scr)
    _, _, l_fin = lax.fori_loop(
        0, (n_kt + 1) // 2, pair_body,
        (first_max, jnp.full((ATT_HEADS, qb), NEG, F32), jnp.zeros((ATT_HEADS, qb), F32)))

    for hd in range(ATT_HEADS):
        rows = slice(hd * ATT_DH, (hd + 1) * ATT_DH)
        acc_scr[rows, :] = acc_scr[rows, :] / l_fin[hd:hd + 1, :]
    o_ref[...] = acc_scr[...].T.astype(o_ref.dtype)


def _attention(q_a, q_i, w_t, lim, k_i, k_a, v_t, *, qb, tk, n_keys, topk, causal, n_valid_q):
    batch, tq, _ = q_a.shape
    lp = k_i.shape[1]
    nq = tq // qb
    index_bits = int(np.ceil(np.log2(lp + 1)))
    assert lp // COUNT_ROWS <= 256, "bfloat16 partial counts are exact only up to 256"
    kernel = functools.partial(_attention_kernel, qb=qb, tk=tk, n_keys=n_keys, topk=topk, causal=causal,
                               n_valid_q=n_valid_q, index_bits=index_bits)
    return pl.pallas_call(
        kernel,
        grid=(batch, nq),
        in_specs=[
            pl.BlockSpec((1, 1, qb), lambda b, i: (i, 0, 0)),
            pl.BlockSpec((None, qb, ATT_WIDTH), lambda b, i: (b, i, 0)),
            pl.BlockSpec((None, qb, IDX_HEADS * IDX_DH), lambda b, i: (b, i, 0)),
            pl.BlockSpec((1, IDX_HEADS, qb), lambda b, i: (b, 0, i)),
            _resident((None, lp, LANES), lambda b, i: (b, 0, 0)),
            _resident((None, lp, ATT_WIDTH), lambda b, i: (b, 0, 0)),
            _resident((None, lp // tk, ATT_WIDTH, tk), lambda b, i: (b, 0, 0, 0)),
        ],
        out_specs=pl.BlockSpec((None, qb, ATT_WIDTH), lambda b, i: (b, i, 0)),
        out_shape=jax.ShapeDtypeStruct((batch, tq, ATT_WIDTH), BF16),
        scratch_shapes=[
            pltpu.VMEM((lp + tk, qb), F32),
            pltpu.VMEM((lp, qb), jnp.int16),
            pltpu.VMEM((lp, qb), jnp.int16),
            pltpu.VMEM((ATT_HEADS, tk, qb), BF16),
            pltpu.VMEM((ATT_HEADS, tk, qb), BF16),
            pltpu.VMEM((ATT_WIDTH, qb), F32),
        ],
        compiler_params=_compiler_params(("parallel", "arbitrary")),
        name="attention",
    )(lim, q_a, q_i, w_t, k_i, k_a, v_t)


FF_CHUNK = 512


def _mlp_kernel(x_ref, yr_ref, ya_ref, wo_ref, g2_ref, wup_ref, wdn_ref, gf_ref, y_ref, x_scr, h_scr, a_scr):
    x = x_ref[...] + jnp.dot(yr_ref[...], wo_ref[:RET_WIDTH, :], preferred_element_type=F32)
    x = x + jnp.dot(ya_ref[...], wo_ref[RET_WIDTH:, :], preferred_element_type=F32)
    x_scr[...] = x
    ms = jnp.mean(x * x, axis=-1, keepdims=True)
    h_scr[...] = ((x * lax.rsqrt(ms + NORM_EPS)) * g2_ref[...]).astype(h_scr.dtype)
    for c in range(D_FF // FF_CHUNK):
        cols = slice(c * FF_CHUNK, (c + 1) * FF_CHUNK)
        u = jnp.dot(h_scr[...], wup_ref[:, cols], preferred_element_type=F32)
        a_scr[:, cols] = jnp.square(jnp.maximum(u, 0.0)).astype(a_scr.dtype)
    x = x_scr[...] + jnp.dot(a_scr[...], wdn_ref[...], preferred_element_type=F32)
    ms = jnp.mean(x * x, axis=-1, keepdims=True)
    y_ref[...] = (x * lax.rsqrt(ms + NORM_EPS)) * gf_ref[...]


def _mlp(x2d, y_r, y_a, w_out_b, norm2, w_up_b, w_down_b, norm_final, tm):
    m = x2d.shape[0]
    row = lambda i: (i, 0)
    const = lambda i: (0, 0)
    return pl.pallas_call(
        _mlp_kernel,
        grid=(m // tm,),
        in_specs=[
            pl.BlockSpec((tm, D_MODEL), row),
            pl.BlockSpec((tm, RET_WIDTH), row),
            pl.BlockSpec((tm, ATT_WIDTH), row),
            _resident((D_MODEL, D_MODEL), const),
            _resident((1, D_MODEL), const),
            _resident((D_MODEL, D_FF), const),
            _resident((D_FF, D_MODEL), const),
            _resident((1, D_MODEL), const),
        ],
        out_specs=pl.BlockSpec((tm, D_MODEL), row),
        out_shape=jax.ShapeDtypeStruct((m, D_MODEL), F32),
        scratch_shapes=[pltpu.VMEM((tm, D_MODEL), F32), pltpu.VMEM((tm, D_MODEL), BF16),
                        pltpu.VMEM((tm, D_FF), BF16)],
        compiler_params=_compiler_params(("parallel",)),
        name="mlp",
    )(x2d, y_r, y_a, w_out_b, norm2, w_up_b, w_down_b, norm_final)


def _pick_tile(n, target):
    t = min(n, target)
    while n % t:
        t //= 2
    return t


def _layer(x, pos, past, s0, ret_blk, weights, *, attn_qb, attn_tk):
    norm1, w_in_b, gn_gain, w_out_b, norm2, w_up_b, w_down_b, norm_final = weights
    batch, seq, _ = x.shape
    m = batch * seq
    x2d = x.reshape(m, D_MODEL)
    tm = _pick_tile(m, 512)

    tabs = _rope_tables(pos)
    if seq < tm:
        tabs = jnp.tile(tabs, (1, tm // seq, 1))
    ret_in, q_a, k_a, v_a, q_i, kw = _proj(x2d, norm1, w_in_b, tabs, tm)

    y_r, s_new = _retention(ret_in, s0, gn_gain, batch, seq, ret_blk)

    three = lambda a: a.reshape(batch, seq, a.shape[-1])
    k_i = three(kw[:, :IDX_DH])
    w_t = jnp.swapaxes(three(kw[:, IDX_DH:IDX_DH + IDX_HEADS]) * IDX_HEADS ** -0.5 * IDX_DH ** -0.5, 1, 2)
    q_a3, q_i3 = three(q_a), three(q_i)
    if past is None:
        n_keys = seq
        keys_i, keys_a, vals = three(kw), three(k_a), three(v_a)
        qpos = pos
        n_valid_q = attn_qb
    else:
        past_k, past_v, past_ki = past
        n_keys = past_k.shape[1] + seq
        pad_i = jnp.pad(past_ki, ((0, 0), (0, 0), (0, LANES - IDX_DH)))
        keys_i = jnp.concatenate([pad_i, three(kw)], axis=1)
        keys_a = jnp.concatenate([past_k.reshape(batch, -1, ATT_WIDTH), three(k_a)], axis=1)
        vals = jnp.concatenate([past_v.reshape(batch, -1, ATT_WIDTH), three(v_a)], axis=1)
        qpos = pos
        n_valid_q = seq
    topk = min(TOPK_MAX, n_keys // 4)
    tq = -(-seq // attn_qb) * attn_qb
    lp = -(-n_keys // attn_tk) * attn_tk
    padq = lambda a: jnp.pad(a, ((0, 0), (0, tq - seq), (0, 0)))
    padk = lambda a: jnp.pad(a, ((0, 0), (0, lp - n_keys), (0, 0)))
    lim = jnp.minimum((qpos // CHUNK + 1) * CHUNK, n_keys).astype(jnp.int32)
    lim = jnp.pad(lim, (0, tq - seq), constant_values=n_keys).reshape(tq // attn_qb, 1, attn_qb)
    v_t = jnp.swapaxes(padk(vals).astype(BF16).reshape(batch, lp // attn_tk, attn_tk, ATT_WIDTH), 2, 3)
    y_a = _attention(
        padq(q_a3), padq(q_i3), jnp.pad(w_t, ((0, 0), (0, 0), (0, tq - seq))), lim,
        padk(keys_i).astype(BF16), padk(keys_a).astype(BF16), v_t,
        qb=attn_qb, tk=attn_tk, n_keys=n_keys, topk=topk, causal=past is None, n_valid_q=n_valid_q)
    y_a = y_a[:, :seq].reshape(m, ATT_WIDTH)

    y = _mlp(x2d, y_r, y_a, w_out_b, norm2, w_up_b, w_down_b, norm_final, tm)
    return (y.reshape(batch, seq, D_MODEL),
            k_a.reshape(batch, seq, ATT_HEADS, ATT_DH),
            v_a.reshape(batch, seq, ATT_HEADS, ATT_DH),
            k_i, s_new)


def kernel(x_prompt, x_sample, cache_k_att, cache_v_att, cache_k_idx, state_ret, norm1, w_in, gn_gain, w_out,
           norm2, w_up, w_down, norm_final):
    depth = norm1.shape[0]
    assert depth == 1, "the final norm is fused into the layer's last kernel"
    past_len = cache_k_att.shape[2]
    pos_p = jnp.arange(x_prompt.shape[1], dtype=jnp.int32)
    pos_s = past_len + jnp.arange(x_sample.shape[1], dtype=jnp.int32)
    l = 0
    w_in_b = jnp.pad(w_in[l], ((0, 0), (0, IN_WIDTH_PADDED - IN_WIDTH))).astype(BF16)
    weights = (norm1[l][None], w_in_b, gn_gain[l][None], w_out[l].astype(BF16), norm2[l][None],
               w_up[l].astype(BF16), w_down[l].astype(BF16), norm_final[None])
    s0_p = jnp.zeros((x_prompt.shape[0], RET_HEADS, RET_DK, RET_DV), F32)
    yp, kap, vap, kip, sp = _layer(x_prompt, pos_p, None, s0_p, 256, weights, attn_qb=256, attn_tk=256)
    past = (cache_k_att[l], cache_v_att[l], cache_k_idx[l])
    ys, kas, vas, kis, ss = _layer(x_sample, pos_s, past, state_ret[l].astype(F32), x_sample.shape[1], weights,
                                   attn_qb=128, attn_tk=256)
    return (yp, ys, kap[None], vap[None], kip[None], sp[None], kas[None], vas[None], kis[None], ss[None])
```

```python
import functools

import jax
import jax.numpy as jnp
import numpy as np
from jax import lax
from jax.experimental import pallas as pl
from jax.experimental.pallas import tpu as pltpu

D_MODEL = 1024
CHUNK = 64
ROPE_THETA = 10000.0
NORM_EPS = 1e-6
NEG = -1e30
RET_WIDTH = 512
ATT_WIDTH = 512
RET_HEADS = 4
RET_DK = 128
RET_DV = 128
ATT_HEADS = 8
ATT_DH = 64
IDX_HEADS = 8
IDX_DH = 64
TOPK_MAX = 256
D_FF = 4 * D_MODEL
IN_WIDTH = 4 * RET_WIDTH + 3 * ATT_WIDTH + IDX_HEADS * IDX_DH + IDX_DH + IDX_HEADS

LANES = 128
IN_WIDTH_PADDED = ((IN_WIDTH + LANES - 1) // LANES) * LANES
VMEM_LIMIT_BYTES = 56 * 1024 * 1024

F32 = jnp.float32
BF16 = jnp.bfloat16

_NT = (((1,), (1,)), ((), ()))


def _compiler_params(semantics):
    return pltpu.CompilerParams(dimension_semantics=semantics, vmem_limit_bytes=VMEM_LIMIT_BYTES)


def _resident(block_shape, index_map):
    return pl.BlockSpec(block_shape, index_map, pipeline_mode=pl.Buffered(1))


def _rope_tables(pos):
    posf = pos.astype(F32)[:, None]

    def angles(d):
        half = d // 2
        inv = ROPE_THETA ** (-jnp.arange(half, dtype=F32) * 2.0 / d)
        ang = posf * inv[None, :]
        return jnp.cos(ang), jnp.sin(ang)

    c128, s128 = angles(128)
    zero64 = jnp.zeros_like(s128)
    cos128 = jnp.concatenate([c128, c128], axis=1)
    sin128 = jnp.concatenate([-s128, s128], axis=1)
    c64, s64 = angles(64)
    zero32 = jnp.zeros_like(s64)
    cos64 = jnp.concatenate([c64, c64, c64, c64], axis=1)
    sin64_up = jnp.concatenate([zero32, s64, zero32, s64], axis=1)
    sin64_dn = jnp.concatenate([-s64, zero32, -s64, zero32], axis=1)
    one64 = jnp.ones_like(zero64)
    cosk = jnp.concatenate([c64, c64, one64], axis=1)
    sink_up = jnp.concatenate([zero32, s64, zero64], axis=1)
    sink_dn = jnp.concatenate([-s64, zero32, zero64], axis=1)
    return jnp.stack([cos128, sin128, cos64, sin64_up, sin64_dn, cosk, sink_up, sink_dn])


V_TILE = 256


def _proj_kernel(x_ref, g_ref, w_ref, tab_ref, ret_ref, qa_ref, qi_ref, kw_ref, ka_ref, va_ref,
                 ka16_ref, vt16_ref, kw16_ref):
    x = x_ref[...]
    ms = jnp.mean(x * x, axis=-1, keepdims=True)
    h = ((x * lax.rsqrt(ms + NORM_EPS)) * g_ref[...]).astype(BF16)

    def mm(c0, n):
        return jnp.dot(h, w_ref[:, c0:c0 + n], preferred_element_type=F32)

    def rope128(z):
        return z * tab_ref[0] + pltpu.roll(z, 64, 1) * tab_ref[1]

    def rope64(z):
        return z * tab_ref[2] + pltpu.roll(z, 32, 1) * tab_ref[3] + pltpu.roll(z, 96, 1) * tab_ref[4]

    def ropek(z):
        return z * tab_ref[5] + pltpu.roll(z, 32, 1) * tab_ref[6] + pltpu.roll(z, 96, 1) * tab_ref[7]

    def emit(out_ref, out_c0, w_c0, rope):
        z = mm(w_c0, 512)
        for t in range(4):
            zt = z[:, t * LANES:(t + 1) * LANES]
            if rope is not None:
                zt = rope(zt)
            out_ref[:, out_c0 + t * LANES:out_c0 + (t + 1) * LANES] = zt.astype(out_ref.dtype)

    emit(ret_ref, 0, 0, rope128)
    emit(ret_ref, 512, 512, rope128)
    emit(ret_ref, 1024, 1024, None)
    emit(ret_ref, 1536, 1536, None)
    emit(qa_ref, 0, 2048, rope64)
    emit(qi_ref, 0, 3584, rope64)
    k_a = mm(2560, 512)
    k_a = jnp.concatenate([rope64(k_a[:, t * LANES:(t + 1) * LANES]) for t in range(4)], axis=1)
    ka_ref[...] = k_a
    ka16_ref[...] = k_a.astype(ka16_ref.dtype)
    v_a = mm(3072, 512)
    va_ref[...] = v_a
    v_tile = vt16_ref.shape[2]
    for part in range(vt16_ref.shape[0]):
        vt16_ref[part] = v_a[part * v_tile:(part + 1) * v_tile, :].T.astype(vt16_ref.dtype)
    kw = ropek(mm(4096, LANES))
    kw_ref[...] = kw
    kw16_ref[...] = kw.astype(kw16_ref.dtype)


def _proj(x2d, norm_g, w_in_b, tabs, tm):
    m = x2d.shape[0]
    n_tab = tabs.shape[1] // tm
    row = lambda i: (i, 0)
    row3 = lambda i: (i, 0, 0)
    assert tm % V_TILE == 0 or m == tm, (tm, m)
    n_vt = max(tm // V_TILE, 1)
    v_tile = min(V_TILE, tm)
    outs = (
        jax.ShapeDtypeStruct((m, 4 * RET_WIDTH), F32),
        jax.ShapeDtypeStruct((m, ATT_WIDTH), F32),
        jax.ShapeDtypeStruct((m, IDX_HEADS * IDX_DH), F32),
        jax.ShapeDtypeStruct((m, LANES), F32),
        jax.ShapeDtypeStruct((m, ATT_WIDTH), F32),
        jax.ShapeDtypeStruct((m, ATT_WIDTH), F32),
        jax.ShapeDtypeStruct((m, ATT_WIDTH), BF16),
        jax.ShapeDtypeStruct((m // v_tile, ATT_WIDTH, v_tile), BF16),
        jax.ShapeDtypeStruct((m, LANES), BF16),
    )
    return pl.pallas_call(
        _proj_kernel,
        grid=(m // tm,),
        in_specs=[
            pl.BlockSpec((tm, D_MODEL), row),
            _resident((1, D_MODEL), lambda i: (0, 0)),
            _resident((D_MODEL, IN_WIDTH_PADDED), lambda i: (0, 0)),
            pl.BlockSpec((8, tm, LANES), lambda i: (0, i % n_tab, 0)),
        ],
        out_specs=[
            pl.BlockSpec((tm, 4 * RET_WIDTH), row),
            pl.BlockSpec((tm, ATT_WIDTH), row),
            pl.BlockSpec((tm, IDX_HEADS * IDX_DH), row),
            pl.BlockSpec((tm, LANES), row),
            pl.BlockSpec((tm, ATT_WIDTH), row),
            pl.BlockSpec((tm, ATT_WIDTH), row),
            pl.BlockSpec((tm, ATT_WIDTH), row),
            pl.BlockSpec((n_vt, ATT_WIDTH, v_tile), row3),
            pl.BlockSpec((tm, LANES), row),
        ],
        out_shape=outs,
        compiler_params=_compiler_params(("parallel",)),
        name="proj",
    )(x2d, norm_g, w_in_b, tabs)


def _retention_tables(blk):
    lg = jnp.log1p(-jnp.exp2(-5.0 - jnp.arange(RET_HEADS, dtype=F32)))
    i = jnp.arange(blk, dtype=F32)
    diff = i[:, None] - i[None, :]
    dmat = jnp.where(diff[None] >= 0, jnp.exp(lg[:, None, None] * jnp.maximum(diff, 0.0)[None]), 0.0)
    w_end = jnp.exp(lg[:, None] * (blk - 1.0 - i)[None, :])
    w_q = jnp.exp(lg[:, None] * (i + 1.0)[None, :])
    g_block = jnp.exp(lg * blk)
    lane = lambda a: jnp.broadcast_to(a[:, :, None], (RET_HEADS, blk, LANES))
    return dmat, lane(w_q), lane(w_end), g_block


def _retention_kernel(gblk_ref, q_ref, k_ref, v_ref, g_ref, s0_ref, dmat_ref, wq_ref, wend_ref, gain_ref,
                      y_ref, sfin_ref, s_scr):
    t = pl.program_id(1)

    @pl.when(t == 0)
    def _():
        s_scr[...] = s0_ref[0]

    for hd in range(RET_HEADS):
        cols = slice(hd * RET_DK, (hd + 1) * RET_DK)
        q = q_ref[:, cols]
        k = k_ref[:, cols] * RET_DK ** -0.5
        vb = v_ref[:, cols].astype(BF16)
        state = s_scr[hd]
        scores = lax.dot_general(q.astype(BF16), k.astype(BF16), _NT, preferred_element_type=F32) * dmat_ref[hd]
        o = jnp.dot(scores.astype(BF16), vb, preferred_element_type=F32)
        o = o + jnp.dot((q * wq_ref[hd]).astype(BF16), state.astype(BF16), preferred_element_type=F32)
        k_end_t = (k * wend_ref[hd]).T.astype(BF16)
        s_scr[hd] = gblk_ref[hd] * state + jnp.dot(k_end_t, vb, preferred_element_type=F32)
        mu = jnp.mean(o, axis=-1, keepdims=True)
        var = jnp.mean(jnp.square(o - mu), axis=-1, keepdims=True)
        on = ((o - mu) * lax.rsqrt(var + NORM_EPS)) * gain_ref[:, cols]
        g = g_ref[:, cols]
        y_ref[:, cols] = ((g * jax.nn.sigmoid(g)) * on).astype(y_ref.dtype)

    @pl.when(t == pl.num_programs(1) - 1)
    def _():
        sfin_ref[0] = s_scr[...]


def _retention(ret_in, s0, gn_gain, batch, seq, blk):
    nt = seq // blk
    dmat, w_q, w_end, g_block = _retention_tables(blk)

    def col(c):
        return pl.BlockSpec((blk, RET_WIDTH), lambda b, t: (b * nt + t, c))

    const3 = lambda b, t: (0, 0, 0)
    return pl.pallas_call(
        _retention_kernel,
        grid=(batch, nt),
        in_specs=[
            pl.BlockSpec(memory_space=pltpu.SMEM),
            col(0), col(1), col(2), col(3),
            pl.BlockSpec((1, RET_HEADS, RET_DK, RET_DV), lambda b, t: (b, 0, 0, 0)),
            pl.BlockSpec((RET_HEADS, blk, blk), const3),
            pl.BlockSpec((RET_HEADS, blk, LANES), const3),
            pl.BlockSpec((RET_HEADS, blk, LANES), const3),
            pl.BlockSpec((1, RET_WIDTH), lambda b, t: (0, 0)),
        ],
        out_specs=[
            pl.BlockSpec((blk, RET_WIDTH), lambda b, t: (b * nt + t, 0)),
            pl.BlockSpec((1, RET_HEADS, RET_DK, RET_DV), lambda b, t: (b, 0, 0, 0)),
        ],
        out_shape=(
            jax.ShapeDtypeStruct((batch * seq, RET_WIDTH), BF16),
            jax.ShapeDtypeStruct((batch, RET_HEADS, RET_DK, RET_DV), F32),
        ),
        scratch_shapes=[pltpu.VMEM((RET_HEADS, RET_DK, RET_DV), F32)],
        compiler_params=_compiler_params(("parallel", "arbitrary")),
        name="retention",
    )(g_block, ret_in, ret_in, ret_in, ret_in, s0, dmat, w_q, w_end, gn_gain)


COUNT_ROWS = 32
MASKED = 2.0 * NEG
LOG2_E = 1.4426950408889634


def _ordered_bits_to_float(u):
    bits = jnp.where(u < 0, u ^ jnp.int32(-2 ** 31), ~u)
    return lax.bitcast_convert_type(bits, F32)


def _signed_half(x):
    return (x - 2 ** 15).astype(jnp.int16)


def _attention_kernel(lim_ref, qa_ref, qi_ref, wt_ref, ki_ref, ka_ref, vt_ref, o_ref,
                      s_scr, hi16_scr, lo16_scr, xa_scr, xb_scr, acc_scr,
                      *, qb, tk, n_keys, topk, causal, n_valid_q, index_bits):
    n_kt = pl.program_id(1) * (qb // tk) + (qb // tk) if causal else ki_ref.shape[0] // tk
    lim = lim_ref[0]
    lane = lax.broadcasted_iota(jnp.int32, (qb, LANES), 1)
    key_iota = lax.broadcasted_iota(jnp.int32, (tk, qb), 0)

    def head_operand(ref, hd, scale, shift_to_low):
        pair = ref[:, (hd // 2) * LANES:(hd // 2 + 1) * LANES] * scale
        if shift_to_low and hd % 2:
            pair = pltpu.roll(pair, 64, 1)
        keep = (lane < 64) if (shift_to_low or hd % 2 == 0) else (lane >= 64)
        return jnp.where(keep, pair, 0.0).T.astype(BF16)

    qi_heads = [head_operand(qi_ref, hd, 1.0, True) for hd in range(IDX_HEADS)]
    wt = wt_ref[0]

    def score_body(kt, carry):
        r0 = pl.multiple_of(kt * tk, tk)
        keys = ki_ref[pl.ds(r0, tk), :]
        acc = jnp.zeros((tk, qb), F32)
        for hd in range(IDX_HEADS):
            d = jnp.dot(keys, qi_heads[hd], preferred_element_type=F32)
            acc = acc + jnp.maximum(d, 0.0) * wt[hd:hd + 1, :]
        acc = jnp.where(acc == 0.0, 0.0, acc)
        s = jnp.where(key_iota + r0 < lim, acc, -jnp.inf)
        s_scr[pl.ds(r0, tk), :] = s
        bits = lax.bitcast_convert_type(s, jnp.int32)
        ordered = bits ^ (lax.shift_right_arithmetic(bits, 31) | jnp.int32(-2 ** 31))
        hi16_scr[pl.ds(r0, tk), :] = _signed_half(lax.shift_right_logical(ordered, 16))
        lo16_scr[pl.ds(r0, tk), :] = _signed_half(ordered & 0xFFFF)
        return carry

    lax.fori_loop(0, n_kt, score_body, 0)

    def count(src_ref, indicator, dtype):
        def body(kt, acc):
            for part in range(tk // COUNT_ROWS):
                r0 = pl.multiple_of(kt * tk + part * COUNT_ROWS, COUNT_ROWS)
                acc = acc + indicator(src_ref[pl.ds(r0, COUNT_ROWS), :], r0)
            return acc
        acc = lax.fori_loop(0, n_kt, body, jnp.zeros((COUNT_ROWS, qb), dtype))
        return jnp.sum(acc.astype(F32), axis=0, keepdims=True)

    n_masked = (n_keys - lim).astype(F32)
    kf = float(topk)
    q_lane = lax.broadcasted_iota(jnp.int32, (1, qb), 1)
    one16 = jnp.ones((), BF16)
    zero16 = jnp.zeros((), BF16)

    def bisect_step(it, carry, half_ref):
        cur, cnt = carry
        cand = cur | lax.shift_left(jnp.int32(1), 31 - it)
        if half_ref is hi16_scr:
            half = _signed_half(lax.shift_right_logical(cand, 16))
        else:
            half = _signed_half(cand & 0xFFFF)
        c = count(half_ref, lambda h, r0: jnp.where(h >= half, one16, zero16), BF16)
        ok = c + jnp.where(_ordered_bits_to_float(cand) <= NEG, n_masked, 0.0) >= kf
        return jnp.where(ok, cand, cur), jnp.where(ok, c, cnt)

    carry = (jnp.zeros((1, qb), jnp.int32), jnp.zeros((1, qb), F32))
    carry = lax.fori_loop(0, 16, lambda it, st: bisect_step(it, st, hi16_scr), carry)

    top = _signed_half(lax.shift_right_logical(carry[0], 16))

    def narrow_body(kt, c):
        r0 = pl.multiple_of(kt * tk, tk)
        hi = hi16_scr[pl.ds(r0, tk), :]
        lo = lo16_scr[pl.ds(r0, tk), :]
        lo16_scr[pl.ds(r0, tk), :] = jnp.where(hi > top, jnp.int16(2 ** 15 - 1),
                                               jnp.where(hi == top, lo, jnp.int16(-2 ** 15)))
        return c

    lax.fori_loop(0, n_kt, narrow_body, 0)
    cur, cnt_ge = lax.fori_loop(16, 32, lambda it, st: bisect_step(it, st, lo16_scr), carry)
    theta = _ordered_bits_to_float(cur)

    overflow = jnp.where((cnt_ge > kf) & (q_lane < n_valid_q), 1.0, 0.0)
    all_idx = jnp.full((1, qb), 2 ** index_bits - 1, jnp.int32)

    def tie_cut():
        need = kf - count(s_scr, lambda s, r0: jnp.where(s > theta, 1.0, 0.0), F32)

        def mark_body(kt, c):
            r0 = pl.multiple_of(kt * tk, tk)
            tied = jnp.where(s_scr[pl.ds(r0, tk), :] == theta, key_iota + r0, 2 ** 15 - 1)
            hi16_scr[pl.ds(r0, tk), :] = tied.astype(jnp.int16)
            return c

        lax.fori_loop(0, n_kt, mark_body, 0)

        def body(it, cut):
            cand = cut | lax.shift_left(jnp.int32(1), index_bits - 1 - it)
            cand16 = cand.astype(jnp.int16)
            c = count(hi16_scr, lambda idx, r0: jnp.where(idx < cand16, one16, zero16), BF16)
            return jnp.where(c <= need, cand, cut)

        return lax.fori_loop(0, index_bits, body, jnp.zeros((1, qb), jnp.int32))

    any_overflow = jnp.max(overflow) > 0.0
    idx_cut = lax.cond(any_overflow, tie_cut, lambda: all_idx)

    def bias_body(kt, carry, with_ties):
        r0 = pl.multiple_of(kt * tk, tk)
        s = s_scr[pl.ds(r0, tk), :]
        if with_ties:
            tie = jnp.where(key_iota + r0 < idx_cut, 0.0, MASKED)
            s_scr[pl.ds(r0, tk), :] = jnp.where(s > theta, 0.0, jnp.where(s == theta, tie, MASKED))
        else:
            s_scr[pl.ds(r0, tk), :] = jnp.where(s >= theta, 0.0, MASKED)
        return carry

    lax.cond(any_overflow,
             lambda: lax.fori_loop(0, n_kt, functools.partial(bias_body, with_ties=True), 0),
             lambda: lax.fori_loop(0, n_kt, functools.partial(bias_body, with_ties=False), 0))

    lp = ki_ref.shape[0]
    s_scr[pl.ds(lp, tk), :] = jnp.full((tk, qb), MASKED, F32)
    qa_heads = [head_operand(qa_ref, hd, ATT_DH ** -0.5 * LOG2_E, False) for hd in range(ATT_HEADS)]
    acc_scr[...] = jnp.zeros(acc_scr.shape, F32)

    def logits_stage(kt, x_ref):
        kt_in = jnp.minimum(kt, n_kt - 1)
        r0 = pl.multiple_of(kt_in * tk, tk)
        b0 = pl.multiple_of(jnp.where(kt < n_kt, kt_in * tk, lp), tk)
        tops = []
        for hd in range(ATT_HEADS):
            keys = ka_ref[pl.ds(r0, tk), (hd // 2) * LANES:(hd // 2 + 1) * LANES]
            x = jnp.dot(keys, qa_heads[hd], preferred_element_type=F32) + s_scr[pl.ds(b0, tk), :]
            x_ref[hd] = x.astype(x_ref.dtype)
            tops.append(jnp.max(x, axis=0, keepdims=True))
        return jnp.concatenate(tops, axis=0).astype(x_ref.dtype).astype(F32)

    def accumulate(kt, x_ref, tile_max, m_run, l_run):
        kt_in = jnp.minimum(kt, n_kt - 1)
        m_new = jnp.maximum(m_run, tile_max)
        alpha = jnp.exp2(m_run - m_new)
        m_staged = m_new.astype(x_ref.dtype)
        sums = []
        for hd in range(ATT_HEADS):
            p = jnp.exp2(x_ref[hd] - m_staged[hd:hd + 1, :]).astype(BF16)
            sums.append(jnp.sum(p.astype(F32), axis=0, keepdims=True))
            rows = slice(hd * ATT_DH, (hd + 1) * ATT_DH)
            pv = jnp.dot(vt_ref[kt_in, rows, :], p, preferred_element_type=F32)
            acc_scr[rows, :] = alpha[hd:hd + 1, :] * acc_scr[rows, :] + pv
        return m_new, alpha * l_run + jnp.concatenate(sums, axis=0)

    def pair_body(pair, carry):
        max_a, m_run, l_run = carry
        max_b = logits_stage(2 * pair + 1, xb_scr)
        m_run, l_run = accumulate(2 * pair, xa_scr, max_a, m_run, l_run)
        max_a = logits_stage(2 * pair + 2, xa_scr)
        m_run, l_run = accumulate(2 * pair + 1, xb_scr, max_b, m_run, l_run)
        return max_a, m_run, l_run

    first_max = logits_stage(0, xa_scr)
    _, _, l_fin = lax.fori_loop(
        0, (n_kt + 1) // 2, pair_body,
        (first_max, jnp.full((ATT_HEADS, qb), NEG, F32), jnp.zeros((ATT_HEADS, qb), F32)))

    for hd in range(ATT_HEADS):
        rows = slice(hd * ATT_DH, (hd + 1) * ATT_DH)
        acc_scr[rows, :] = acc_scr[rows, :] / l_fin[hd:hd + 1, :]
    o_ref[...] = acc_scr[...].T.astype(o_ref.dtype)


def _attention(q_a, q_i, w_t, lim, k_i, k_a, v_t, *, qb, tk, n_keys, topk, causal, n_valid_q):
    batch, tq, _ = q_a.shape
    lp = k_i.shape[1]
    nq = tq // qb
    index_bits = int(np.ceil(np.log2(lp + 1)))
    assert lp // COUNT_ROWS <= 256, "bfloat16 partial counts are exact only up to 256"
    kernel = functools.partial(_attention_kernel, qb=qb, tk=tk, n_keys=n_keys, topk=topk, causal=causal,
                               n_valid_q=n_valid_q, index_bits=index_bits)
    return pl.pallas_call(
        kernel,
        grid=(batch, nq),
        in_specs=[
            pl.BlockSpec((1, 1, qb), lambda b, i: (i, 0, 0)),
            pl.BlockSpec((None, qb, ATT_WIDTH), lambda b, i: (b, i, 0)),
            pl.BlockSpec((None, qb, IDX_HEADS * IDX_DH), lambda b, i: (b, i, 0)),
            pl.BlockSpec((1, IDX_HEADS, qb), lambda b, i: (b, 0, i)),
            _resident((None, lp, LANES), lambda b, i: (b, 0, 0)),
            _resident((None, lp, ATT_WIDTH), lambda b, i: (b, 0, 0)),
            _resident((None, lp // tk, ATT_WIDTH, tk), lambda b, i: (b, 0, 0, 0)),
        ],
        out_specs=pl.BlockSpec((None, qb, ATT_WIDTH), lambda b, i: (b, i, 0)),
        out_shape=jax.ShapeDtypeStruct((batch, tq, ATT_WIDTH), BF16),
        scratch_shapes=[
            pltpu.VMEM((lp + tk, qb), F32),
            pltpu.VMEM((lp, qb), jnp.int16),
            pltpu.VMEM((lp, qb), jnp.int16),
            pltpu.VMEM((ATT_HEADS, tk, qb), BF16),
            pltpu.VMEM((ATT_HEADS, tk, qb), BF16),
            pltpu.VMEM((ATT_WIDTH, qb), F32),
        ],
        compiler_params=_compiler_params(("parallel", "arbitrary")),
        name="attention",
    )(lim, q_a, q_i, w_t, k_i, k_a, v_t)


FF_CHUNK = 512


def _mlp_kernel(x_ref, yr_ref, ya_ref, wo_ref, g2_ref, wup_ref, wdn_ref, gf_ref, y_ref, x_scr, h_scr, a_scr):
    x = x_ref[...] + jnp.dot(yr_ref[...], wo_ref[:RET_WIDTH, :], preferred_element_type=F32)
    x = x + jnp.dot(ya_ref[...], wo_ref[RET_WIDTH:, :], preferred_element_type=F32)
    x_scr[...] = x
    ms = jnp.mean(x * x, axis=-1, keepdims=True)
    h_scr[...] = ((x * lax.rsqrt(ms + NORM_EPS)) * g2_ref[...]).astype(h_scr.dtype)
    for c in range(D_FF // FF_CHUNK):
        cols = slice(c * FF_CHUNK, (c + 1) * FF_CHUNK)
        u = jnp.dot(h_scr[...], wup_ref[:, cols], preferred_element_type=F32)
        a_scr[:, cols] = jnp.square(jnp.maximum(u, 0.0)).astype(a_scr.dtype)
    x = x_scr[...] + jnp.dot(a_scr[...], wdn_ref[...], preferred_element_type=F32)
    ms = jnp.mean(x * x, axis=-1, keepdims=True)
    y_ref[...] = (x * lax.rsqrt(ms + NORM_EPS)) * gf_ref[...]


def _mlp(x2d, y_r, y_a, w_out_b, norm2, w_up_b, w_down_b, norm_final, tm):
    m = x2d.shape[0]
    row = lambda i: (i, 0)
    const = lambda i: (0, 0)
    return pl.pallas_call(
        _mlp_kernel,
        grid=(m // tm,),
        in_specs=[
            pl.BlockSpec((tm, D_MODEL), row),
            pl.BlockSpec((tm, RET_WIDTH), row),
            pl.BlockSpec((tm, ATT_WIDTH), row),
            _resident((D_MODEL, D_MODEL), const),
            _resident((1, D_MODEL), const),
            _resident((D_MODEL, D_FF), const),
            _resident((D_FF, D_MODEL), const),
            _resident((1, D_MODEL), const),
        ],
        out_specs=pl.BlockSpec((tm, D_MODEL), row),
        out_shape=jax.ShapeDtypeStruct((m, D_MODEL), F32),
        scratch_shapes=[pltpu.VMEM((tm, D_MODEL), F32), pltpu.VMEM((tm, D_MODEL), BF16),
                        pltpu.VMEM((tm, D_FF), BF16)],
        compiler_params=_compiler_params(("parallel",)),
        name="mlp",
    )(x2d, y_r, y_a, w_out_b, norm2, w_up_b, w_down_b, norm_final)


def _pick_tile(n, target):
    t = min(n, target)
    while n % t:
        t //= 2
    return t


def _layer(x, pos, past, s0, ret_blk, weights, *, attn_qb, attn_tk):
    norm1, w_in_b, gn_gain, w_out_b, norm2, w_up_b, w_down_b, norm_final = weights
    batch, seq, _ = x.shape
    m = batch * seq
    x2d = x.reshape(m, D_MODEL)
    tm = _pick_tile(m, 512)

    tabs = _rope_tables(pos)
    if seq < tm:
        tabs = jnp.tile(tabs, (1, tm // seq, 1))
    ret_in, q_a, q_i, kw, k_a, v_a, ka16, vt16, kw16 = _proj(x2d, norm1, w_in_b, tabs, tm)

    y_r, s_new = _retention(ret_in, s0, gn_gain, batch, seq, ret_blk)

    three = lambda a: a.reshape(batch, seq, -1)
    k_i = three(kw[:, :IDX_DH])
    w_t = jnp.swapaxes(three(kw[:, IDX_DH:IDX_DH + IDX_HEADS]) * IDX_HEADS ** -0.5 * IDX_DH ** -0.5, 1, 2)
    q_a3, q_i3 = three(q_a), three(q_i)
    n_keys = seq if past is None else past[0].shape[1] + seq
    topk = min(TOPK_MAX, n_keys // 4)
    tq = -(-seq // attn_qb) * attn_qb
    lp = -(-n_keys // attn_tk) * attn_tk
    padq = lambda a: jnp.pad(a, ((0, 0), (0, tq - seq), (0, 0)))
    padk = lambda a: jnp.pad(a, ((0, 0), (0, lp - n_keys), (0, 0)))
    if past is None and attn_tk == vt16.shape[2] and lp == n_keys:
        keys_i, keys_a = three(kw16), three(ka16)
        v_t = vt16.reshape(batch, lp // attn_tk, ATT_WIDTH, attn_tk)
        n_valid_q = attn_qb
    else:
        keys_i, keys_a, vals = three(kw16), three(ka16), three(v_a).astype(BF16)
        n_valid_q = attn_qb
        if past is not None:
            past_k, past_v, past_ki = past
            pad_i = jnp.pad(past_ki, ((0, 0), (0, 0), (0, LANES - IDX_DH))).astype(BF16)
            keys_i = jnp.concatenate([pad_i, keys_i], axis=1)
            keys_a = jnp.concatenate([past_k.reshape(batch, -1, ATT_WIDTH).astype(BF16), keys_a], axis=1)
            vals = jnp.concatenate([past_v.reshape(batch, -1, ATT_WIDTH).astype(BF16), vals], axis=1)
            n_valid_q = seq
        keys_i, keys_a = padk(keys_i), padk(keys_a)
        v_t = jnp.swapaxes(padk(vals).reshape(batch, lp // attn_tk, attn_tk, ATT_WIDTH), 2, 3)
    lim = jnp.minimum((pos // CHUNK + 1) * CHUNK, n_keys).astype(jnp.int32)
    lim = jnp.pad(lim, (0, tq - seq), constant_values=n_keys).reshape(tq // attn_qb, 1, attn_qb)
    y_a = _attention(
        padq(q_a3), padq(q_i3), jnp.pad(w_t, ((0, 0), (0, 0), (0, tq - seq))), lim, keys_i, keys_a, v_t,
        qb=attn_qb, tk=attn_tk, n_keys=n_keys, topk=topk, causal=past is None, n_valid_q=n_valid_q)
    y_a = y_a[:, :seq].reshape(m, ATT_WIDTH)

    y = _mlp(x2d, y_r, y_a, w_out_b, norm2, w_up_b, w_down_b, norm_final, tm)
    return (y.reshape(batch, seq, D_MODEL),
            k_a.reshape(batch, seq, ATT_HEADS, ATT_DH),
            v_a.reshape(batch, seq, ATT_HEADS, ATT_DH),
            k_i, s_new)


def kernel(x_prompt, x_sample, cache_k_att, cache_v_att, cache_k_idx, state_ret, norm1, w_in, gn_gain, w_out,
           norm2, w_up, w_down, norm_final):
    depth = norm1.shape[0]
    assert depth == 1, "the final norm is fused into the layer's last kernel"
    past_len = cache_k_att.shape[2]
    pos_p = jnp.arange(x_prompt.shape[1], dtype=jnp.int32)
    pos_s = past_len + jnp.arange(x_sample.shape[1], dtype=jnp.int32)
    l = 0
    w_in_b = jnp.pad(w_in[l], ((0, 0), (0, IN_WIDTH_PADDED - IN_WIDTH))).astype(BF16)
    weights = (norm1[l][None], w_in_b, gn_gain[l][None], w_out[l].astype(BF16), norm2[l][None],
               w_up[l].astype(BF16), w_down[l].astype(BF16), norm_final[None])
    s0_p = jnp.zeros((x_prompt.shape[0], RET_HEADS, RET_DK, RET_DV), F32)
    yp, kap, vap, kip, sp = _layer(x_prompt, pos_p, None, s0_p, 256, weights, attn_qb=256, attn_tk=256)
    past = (cache_k_att[l], cache_v_att[l], cache_k_idx[l])
    ys, kas, vas, kis, ss = _layer(x_sample, pos_s, past, state_ret[l].astype(F32), x_sample.shape[1], weights,
                                   attn_qb=128, attn_tk=256)
    return (yp, ys, kap[None], vap[None], kip[None], sp[None], kas[None], vas[None], kis[None], ss[None])
```

```python
import functools

import jax
import jax.numpy as jnp
import numpy as np
from jax import lax
from jax.experimental import pallas as pl
from jax.experimental.pallas import tpu as pltpu

D_MODEL = 1024
CHUNK = 64
ROPE_THETA = 10000.0
NORM_EPS = 1e-6
NEG = -1e30
RET_WIDTH = 512
ATT_WIDTH = 512
RET_HEADS = 4
RET_DK = 128
RET_DV = 128
ATT_HEADS = 8
ATT_DH = 64
IDX_HEADS = 8
IDX_DH = 64
TOPK_MAX = 256
D_FF = 4 * D_MODEL
IN_WIDTH = 4 * RET_WIDTH + 3 * ATT_WIDTH + IDX_HEADS * IDX_DH + IDX_DH + IDX_HEADS

LANES = 128
IN_WIDTH_PADDED = ((IN_WIDTH + LANES - 1) // LANES) * LANES
VMEM_LIMIT_BYTES = 56 * 1024 * 1024

F32 = jnp.float32
BF16 = jnp.bfloat16

_NT = (((1,), (1,)), ((), ()))


def _compiler_params(semantics):
    return pltpu.CompilerParams(dimension_semantics=semantics, vmem_limit_bytes=VMEM_LIMIT_BYTES)


def _resident(block_shape, index_map):
    return pl.BlockSpec(block_shape, index_map, pipeline_mode=pl.Buffered(1))


def _rope_tables(pos):
    posf = pos.astype(F32)[:, None]

    def angles(d):
        half = d // 2
        inv = ROPE_THETA ** (-jnp.arange(half, dtype=F32) * 2.0 / d)
        ang = posf * inv[None, :]
        return jnp.cos(ang), jnp.sin(ang)

    c128, s128 = angles(128)
    zero64 = jnp.zeros_like(s128)
    cos128 = jnp.concatenate([c128, c128], axis=1)
    sin128 = jnp.concatenate([-s128, s128], axis=1)
    c64, s64 = angles(64)
    zero32 = jnp.zeros_like(s64)
    cos64 = jnp.concatenate([c64, c64, c64, c64], axis=1)
    sin64_up = jnp.concatenate([zero32, s64, zero32, s64], axis=1)
    sin64_dn = jnp.concatenate([-s64, zero32, -s64, zero32], axis=1)
    one64 = jnp.ones_like(zero64)
    cosk = jnp.concatenate([c64, c64, one64], axis=1)
    sink_up = jnp.concatenate([zero32, s64, zero64], axis=1)
    sink_dn = jnp.concatenate([-s64, zero32, zero64], axis=1)
    return jnp.stack([cos128, sin128, cos64, sin64_up, sin64_dn, cosk, sink_up, sink_dn])


V_TILE = 256


def _proj_kernel(x_ref, g_ref, w_ref, tab_ref, ret_ref, qa_ref, qi_ref, kw_ref, ka_ref, va_ref,
                 ka16_ref, vt16_ref, kw16_ref):
    x = x_ref[...]
    ms = jnp.mean(x * x, axis=-1, keepdims=True)
    h = ((x * lax.rsqrt(ms + NORM_EPS)) * g_ref[...]).astype(BF16)

    def mm(c0, n):
        return jnp.dot(h, w_ref[:, c0:c0 + n], preferred_element_type=F32)

    def rope128(z):
        return z * tab_ref[0] + pltpu.roll(z, 64, 1) * tab_ref[1]

    def rope64(z):
        return z * tab_ref[2] + pltpu.roll(z, 32, 1) * tab_ref[3] + pltpu.roll(z, 96, 1) * tab_ref[4]

    def ropek(z):
        return z * tab_ref[5] + pltpu.roll(z, 32, 1) * tab_ref[6] + pltpu.roll(z, 96, 1) * tab_ref[7]

    def emit(out_ref, out_c0, w_c0, rope):
        z = mm(w_c0, 512)
        for t in range(4):
            zt = z[:, t * LANES:(t + 1) * LANES]
            if rope is not None:
                zt = rope(zt)
            out_ref[:, out_c0 + t * LANES:out_c0 + (t + 1) * LANES] = zt.astype(out_ref.dtype)

    emit(ret_ref, 0, 0, rope128)
    emit(ret_ref, 512, 512, rope128)
    emit(ret_ref, 1024, 1024, None)
    emit(ret_ref, 1536, 1536, None)
    emit(qa_ref, 0, 2048, rope64)
    emit(qi_ref, 0, 3584, rope64)
    k_a = mm(2560, 512)
    k_a = jnp.concatenate([rope64(k_a[:, t * LANES:(t + 1) * LANES]) for t in range(4)], axis=1)
    ka_ref[...] = k_a
    ka16_ref[...] = k_a.astype(ka16_ref.dtype)
    v_a = mm(3072, 512)
    va_ref[...] = v_a
    v_tile = vt16_ref.shape[2]
    for part in range(vt16_ref.shape[0]):
        vt16_ref[part] = v_a[part * v_tile:(part + 1) * v_tile, :].T.astype(vt16_ref.dtype)
    kw = ropek(mm(4096, LANES))
    kw_ref[...] = kw
    kw16_ref[...] = kw.astype(kw16_ref.dtype)


def _proj(x2d, norm_g, w_in_b, tabs, tm):
    m = x2d.shape[0]
    n_tab = tabs.shape[1] // tm
    row = lambda i: (i, 0)
    row3 = lambda i: (i, 0, 0)
    assert tm % V_TILE == 0 or m == tm, (tm, m)
    n_vt = max(tm // V_TILE, 1)
    v_tile = min(V_TILE, tm)
    outs = (
        jax.ShapeDtypeStruct((m, 4 * RET_WIDTH), F32),
        jax.ShapeDtypeStruct((m, ATT_WIDTH), F32),
        jax.ShapeDtypeStruct((m, IDX_HEADS * IDX_DH), F32),
        jax.ShapeDtypeStruct((m, LANES), F32),
        jax.ShapeDtypeStruct((m, ATT_WIDTH), F32),
        jax.ShapeDtypeStruct((m, ATT_WIDTH), F32),
        jax.ShapeDtypeStruct((m, ATT_WIDTH), BF16),
        jax.ShapeDtypeStruct((m // v_tile, ATT_WIDTH, v_tile), BF16),
        jax.ShapeDtypeStruct((m, LANES), BF16),
    )
    return pl.pallas_call(
        _proj_kernel,
        grid=(m // tm,),
        in_specs=[
            pl.BlockSpec((tm, D_MODEL), row),
            _resident((1, D_MODEL), lambda i: (0, 0)),
            _resident((D_MODEL, IN_WIDTH_PADDED), lambda i: (0, 0)),
            pl.BlockSpec((8, tm, LANES), lambda i: (0, i % n_tab, 0)),
        ],
        out_specs=[
            pl.BlockSpec((tm, 4 * RET_WIDTH), row),
            pl.BlockSpec((tm, ATT_WIDTH), row),
            pl.BlockSpec((tm, IDX_HEADS * IDX_DH), row),
            pl.BlockSpec((tm, LANES), row),
            pl.BlockSpec((tm, ATT_WIDTH), row),
            pl.BlockSpec((tm, ATT_WIDTH), row),
            pl.BlockSpec((tm, ATT_WIDTH), row),
            pl.BlockSpec((n_vt, ATT_WIDTH, v_tile), row3),
            pl.BlockSpec((tm, LANES), row),
        ],
        out_shape=outs,
        compiler_params=_compiler_params(("parallel",)),
        name="proj",
    )(x2d, norm_g, w_in_b, tabs)


def _retention_tables(blk):
    lg = jnp.log1p(-jnp.exp2(-5.0 - jnp.arange(RET_HEADS, dtype=F32)))
    i = jnp.arange(blk, dtype=F32)
    diff = i[:, None] - i[None, :]
    dmat = jnp.where(diff[None] >= 0, jnp.exp(lg[:, None, None] * jnp.maximum(diff, 0.0)[None]), 0.0)
    w_end = jnp.exp(lg[:, None] * (blk - 1.0 - i)[None, :])
    w_q = jnp.exp(lg[:, None] * (i + 1.0)[None, :])
    g_block = jnp.exp(lg * blk)
    lane = lambda a: jnp.broadcast_to(a[:, :, None], (RET_HEADS, blk, LANES))
    return dmat, lane(w_q), lane(w_end), g_block


def _retention_kernel(gblk_ref, q_ref, k_ref, v_ref, g_ref, s0_ref, dmat_ref, wq_ref, wend_ref, gain_ref,
                      y_ref, sfin_ref, s_scr):
    t = pl.program_id(1)

    @pl.when(t == 0)
    def _():
        s_scr[...] = s0_ref[0]

    for hd in range(RET_HEADS):
        cols = slice(hd * RET_DK, (hd + 1) * RET_DK)
        q = q_ref[:, cols]
        k = k_ref[:, cols] * RET_DK ** -0.5
        vb = v_ref[:, cols].astype(BF16)
        state = s_scr[hd]
        scores = lax.dot_general(q.astype(BF16), k.astype(BF16), _NT, preferred_element_type=F32) * dmat_ref[hd]
        o = jnp.dot(scores.astype(BF16), vb, preferred_element_type=F32)
        o = o + jnp.dot((q * wq_ref[hd]).astype(BF16), state.astype(BF16), preferred_element_type=F32)
        k_end_t = (k * wend_ref[hd]).T.astype(BF16)
        s_scr[hd] = gblk_ref[hd] * state + jnp.dot(k_end_t, vb, preferred_element_type=F32)
        mu = jnp.mean(o, axis=-1, keepdims=True)
        var = jnp.mean(jnp.square(o - mu), axis=-1, keepdims=True)
        on = ((o - mu) * lax.rsqrt(var + NORM_EPS)) * gain_ref[:, cols]
        g = g_ref[:, cols]
        y_ref[:, cols] = ((g * jax.nn.sigmoid(g)) * on).astype(y_ref.dtype)

    @pl.when(t == pl.num_programs(1) - 1)
    def _():
        sfin_ref[0] = s_scr[...]


def _retention(ret_in, s0, gn_gain, batch, seq, blk):
    nt = seq // blk
    dmat, w_q, w_end, g_block = _retention_tables(blk)

    def col(c):
        return pl.BlockSpec((blk, RET_WIDTH), lambda b, t: (b * nt + t, c))

    const3 = lambda b, t: (0, 0, 0)
    return pl.pallas_call(
        _retention_kernel,
        grid=(batch, nt),
        in_specs=[
            pl.BlockSpec(memory_space=pltpu.SMEM),
            col(0), col(1), col(2), col(3),
            pl.BlockSpec((1, RET_HEADS, RET_DK, RET_DV), lambda b, t: (b, 0, 0, 0)),
            pl.BlockSpec((RET_HEADS, blk, blk), const3),
            pl.BlockSpec((RET_HEADS, blk, LANES), const3),
            pl.BlockSpec((RET_HEADS, blk, LANES), const3),
            pl.BlockSpec((1, RET_WIDTH), lambda b, t: (0, 0)),
        ],
        out_specs=[
            pl.BlockSpec((blk, RET_WIDTH), lambda b, t: (b * nt + t, 0)),
            pl.BlockSpec((1, RET_HEADS, RET_DK, RET_DV), lambda b, t: (b, 0, 0, 0)),
        ],
        out_shape=(
            jax.ShapeDtypeStruct((batch * seq, RET_WIDTH), BF16),
            jax.ShapeDtypeStruct((batch, RET_HEADS, RET_DK, RET_DV), F32),
        ),
        scratch_shapes=[pltpu.VMEM((RET_HEADS, RET_DK, RET_DV), F32)],
        compiler_params=_compiler_params(("parallel", "arbitrary")),
        name="retention",
    )(g_block, ret_in, ret_in, ret_in, ret_in, s0, dmat, w_q, w_end, gn_gain)


COUNT_ROWS = 32
MASKED = 2.0 * NEG
LOG2_E = 1.4426950408889634


def _ordered_bits_to_float(u):
    bits = jnp.where(u < 0, u ^ jnp.int32(-2 ** 31), ~u)
    return lax.bitcast_convert_type(bits, F32)


def _signed_half(x):
    return (x - 2 ** 15).astype(jnp.int16)


def _attention_kernel(lim_ref, qa_ref, qi_ref, wt_ref, ki_ref, ka_ref, vt_ref, o_ref,
                      s_scr, hi16_scr, lo16_scr, xa_scr, xb_scr, acc_scr,
                      *, qb, tk, n_keys, topk, causal, n_valid_q, index_bits):
    n_kt = pl.program_id(1) * (qb // tk) + (qb // tk) if causal else ki_ref.shape[0] // tk
    lim = lim_ref[0]
    lane = lax.broadcasted_iota(jnp.int32, (qb, LANES), 1)
    key_iota = lax.broadcasted_iota(jnp.int32, (tk, qb), 0)

    def head_operand(ref, hd, scale, shift_to_low):
        pair = ref[:, (hd // 2) * LANES:(hd // 2 + 1) * LANES] * scale
        if shift_to_low and hd % 2:
            pair = pltpu.roll(pair, 64, 1)
        keep = (lane < 64) if (shift_to_low or hd % 2 == 0) else (lane >= 64)
        return jnp.where(keep, pair, 0.0).T.astype(BF16)

    qi_heads = [head_operand(qi_ref, hd, 1.0, True) for hd in range(IDX_HEADS)]
    wt = wt_ref[0]

    def score_body(kt, carry, some_inadmissible):
        r0 = pl.multiple_of(kt * tk, tk)
        keys = ki_ref[pl.ds(r0, tk), :]
        acc = jnp.zeros((tk, qb), F32)
        for hd in range(IDX_HEADS):
            d = jnp.dot(keys, qi_heads[hd], preferred_element_type=F32)
            acc = acc + jnp.maximum(d, 0.0) * wt[hd:hd + 1, :]
        s = jnp.where(acc == 0.0, 0.0, acc)
        if some_inadmissible:
            s = jnp.where(key_iota + r0 < lim, s, -jnp.inf)
        s_scr[pl.ds(r0, tk), :] = s
        bits = lax.bitcast_convert_type(s, jnp.int32)
        ordered = bits ^ (lax.shift_right_arithmetic(bits, 31) | jnp.int32(-2 ** 31))
        hi16_scr[pl.ds(r0, tk), :] = _signed_half(lax.shift_right_logical(ordered, 16))
        lo16_scr[pl.ds(r0, tk), :] = _signed_half(ordered & 0xFFFF)
        return carry

    n_all_admissible = jnp.minimum(jnp.min(lim) // tk, n_kt)
    lax.fori_loop(0, n_all_admissible, functools.partial(score_body, some_inadmissible=False), 0)
    lax.fori_loop(n_all_admissible, n_kt, functools.partial(score_body, some_inadmissible=True), 0)

    def count(src_ref, indicator, dtype):
        def one_tile(kt, acc):
            for part in range(tk // COUNT_ROWS):
                r0 = pl.multiple_of(kt * tk + part * COUNT_ROWS, COUNT_ROWS)
                acc = acc + indicator(src_ref[pl.ds(r0, COUNT_ROWS), :], r0)
            return acc
        acc = lax.fori_loop(0, n_kt // 2, lambda pair, a: one_tile(2 * pair + 1, one_tile(2 * pair, a)),
                            jnp.zeros((COUNT_ROWS, qb), dtype))
        acc = lax.cond(n_kt % 2 == 1, lambda a: one_tile(n_kt - 1, a), lambda a: a, acc)
        return jnp.sum(acc.astype(F32), axis=0, keepdims=True)

    n_masked = (n_keys - lim).astype(F32)
    kf = float(topk)
    q_lane = lax.broadcasted_iota(jnp.int32, (1, qb), 1)
    one16 = jnp.ones((), BF16)
    zero16 = jnp.zeros((), BF16)

    def bisect_step(it, carry, half_ref):
        cur, cnt = carry
        cand = cur | lax.shift_left(jnp.int32(1), 31 - it)
        if half_ref is hi16_scr:
            half = _signed_half(lax.shift_right_logical(cand, 16))
        else:
            half = _signed_half(cand & 0xFFFF)
        c = count(half_ref, lambda h, r0: jnp.where(h >= half, one16, zero16), BF16)
        ok = c + jnp.where(_ordered_bits_to_float(cand) <= NEG, n_masked, 0.0) >= kf
        return jnp.where(ok, cand, cur), jnp.where(ok, c, cnt)

    carry = (jnp.zeros((1, qb), jnp.int32), jnp.zeros((1, qb), F32))
    carry = lax.fori_loop(0, 16, lambda it, st: bisect_step(it, st, hi16_scr), carry)

    top = _signed_half(lax.shift_right_logical(carry[0], 16))

    def narrow_body(kt, c):
        r0 = pl.multiple_of(kt * tk, tk)
        hi = hi16_scr[pl.ds(r0, tk), :]
        lo = lo16_scr[pl.ds(r0, tk), :]
        lo16_scr[pl.ds(r0, tk), :] = jnp.where(hi > top, jnp.int16(2 ** 15 - 1),
                                               jnp.where(hi == top, lo, jnp.int16(-2 ** 15)))
        return c

    lax.fori_loop(0, n_kt, narrow_body, 0)
    cur, cnt_ge = lax.fori_loop(16, 32, lambda it, st: bisect_step(it, st, lo16_scr), carry)
    theta = _ordered_bits_to_float(cur)

    overflow = jnp.where((cnt_ge > kf) & (q_lane < n_valid_q), 1.0, 0.0)
    all_idx = jnp.full((1, qb), 2 ** index_bits - 1, jnp.int32)

    def tie_cut():
        need = kf - count(s_scr, lambda s, r0: jnp.where(s > theta, 1.0, 0.0), F32)

        def mark_body(kt, c):
            r0 = pl.multiple_of(kt * tk, tk)
            tied = jnp.where(s_scr[pl.ds(r0, tk), :] == theta, key_iota + r0, 2 ** 15 - 1)
            hi16_scr[pl.ds(r0, tk), :] = tied.astype(jnp.int16)
            return c

        lax.fori_loop(0, n_kt, mark_body, 0)

        def body(it, cut):
            cand = cut | lax.shift_left(jnp.int32(1), index_bits - 1 - it)
            cand16 = cand.astype(jnp.int16)
            c = count(hi16_scr, lambda idx, r0: jnp.where(idx < cand16, one16, zero16), BF16)
            return jnp.where(c <= need, cand, cut)

        return lax.fori_loop(0, index_bits, body, jnp.zeros((1, qb), jnp.int32))

    any_overflow = jnp.max(overflow) > 0.0
    idx_cut = lax.cond(any_overflow, tie_cut, lambda: all_idx)

    def bias_body(kt, carry, with_ties):
        r0 = pl.multiple_of(kt * tk, tk)
        s = s_scr[pl.ds(r0, tk), :]
        if with_ties:
            tie = jnp.where(key_iota + r0 < idx_cut, 0.0, MASKED)
            s_scr[pl.ds(r0, tk), :] = jnp.where(s > theta, 0.0, jnp.where(s == theta, tie, MASKED))
        else:
            s_scr[pl.ds(r0, tk), :] = jnp.where(s >= theta, 0.0, MASKED)
        return carry

    lax.cond(any_overflow,
             lambda: lax.fori_loop(0, n_kt, functools.partial(bias_body, with_ties=True), 0),
             lambda: lax.fori_loop(0, n_kt, functools.partial(bias_body, with_ties=False), 0))

    lp = ki_ref.shape[0]
    s_scr[pl.ds(lp, tk), :] = jnp.full((tk, qb), MASKED, F32)
    qa_heads = [head_operand(qa_ref, hd, ATT_DH ** -0.5 * LOG2_E, False) for hd in range(ATT_HEADS)]
    acc_scr[...] = jnp.zeros(acc_scr.shape, F32)

    def logits_stage(kt, x_ref):
        kt_in = jnp.minimum(kt, n_kt - 1)
        r0 = pl.multiple_of(kt_in * tk, tk)
        b0 = pl.multiple_of(jnp.where(kt < n_kt, kt_in * tk, lp), tk)
        tops = []
        for hd in range(ATT_HEADS):
            keys = ka_ref[pl.ds(r0, tk), (hd // 2) * LANES:(hd // 2 + 1) * LANES]
            x = jnp.dot(keys, qa_heads[hd], preferred_element_type=F32) + s_scr[pl.ds(b0, tk), :]
            x_ref[hd] = x.astype(x_ref.dtype)
            tops.append(jnp.max(x, axis=0, keepdims=True))
        return jnp.concatenate(tops, axis=0).astype(x_ref.dtype).astype(F32)

    def accumulate(kt, x_ref, tile_max, m_run, l_run):
        kt_in = jnp.minimum(kt, n_kt - 1)
        m_new = jnp.maximum(m_run, tile_max)
        alpha = jnp.exp2(m_run - m_new)
        m_staged = m_new.astype(x_ref.dtype)
        sums = []
        for hd in range(ATT_HEADS):
            p = jnp.exp2(x_ref[hd] - m_staged[hd:hd + 1, :]).astype(BF16)
            sums.append(jnp.sum(p.astype(F32), axis=0, keepdims=True))
            rows = slice(hd * ATT_DH, (hd + 1) * ATT_DH)
            pv = jnp.dot(vt_ref[kt_in, rows, :], p, preferred_element_type=F32)
            acc_scr[rows, :] = alpha[hd:hd + 1, :] * acc_scr[rows, :] + pv
        return m_new, alpha * l_run + jnp.concatenate(sums, axis=0)

    def pair_body(pair, carry):
        max_a, m_run, l_run = carry
        max_b = logits_stage(2 * pair + 1, xb_scr)
        m_run, l_run = accumulate(2 * pair, xa_scr, max_a, m_run, l_run)
        max_a = logits_stage(2 * pair + 2, xa_scr)
        m_run, l_run = accumulate(2 * pair + 1, xb_scr, max_b, m_run, l_run)
        return max_a, m_run, l_run

    first_max = logits_stage(0, xa_scr)
    _, _, l_fin = lax.fori_loop(
        0, (n_kt + 1) // 2, pair_body,
        (first_max, jnp.full((ATT_HEADS, qb), NEG, F32), jnp.zeros((ATT_HEADS, qb), F32)))

    for hd in range(ATT_HEADS):
        rows = slice(hd * ATT_DH, (hd + 1) * ATT_DH)
        acc_scr[rows, :] = acc_scr[rows, :] / l_fin[hd:hd + 1, :]
    o_ref[...] = acc_scr[...].T.astype(o_ref.dtype)


def _attention(q_a, q_i, w_t, lim, k_i, k_a, v_t, *, qb, tk, n_keys, topk, causal, n_valid_q):
    batch, tq, _ = q_a.shape
    lp = k_i.shape[1]
    nq = tq // qb
    index_bits = int(np.ceil(np.log2(lp + 1)))
    assert lp // COUNT_ROWS <= 256, "bfloat16 partial counts are exact only up to 256"
    kernel = functools.partial(_attention_kernel, qb=qb, tk=tk, n_keys=n_keys, topk=topk, causal=causal,
                               n_valid_q=n_valid_q, index_bits=index_bits)
    return pl.pallas_call(
        kernel,
        grid=(batch, nq),
        in_specs=[
            pl.BlockSpec((1, 1, qb), lambda b, i: (i, 0, 0)),
            pl.BlockSpec((None, qb, ATT_WIDTH), lambda b, i: (b, i, 0)),
            pl.BlockSpec((None, qb, IDX_HEADS * IDX_DH), lambda b, i: (b, i, 0)),
            pl.BlockSpec((1, IDX_HEADS, qb), lambda b, i: (b, 0, i)),
            _resident((None, lp, LANES), lambda b, i: (b, 0, 0)),
            _resident((None, lp, ATT_WIDTH), lambda b, i: (b, 0, 0)),
            _resident((None, lp // tk, ATT_WIDTH, tk), lambda b, i: (b, 0, 0, 0)),
        ],
        out_specs=pl.BlockSpec((None, qb, ATT_WIDTH), lambda b, i: (b, i, 0)),
        out_shape=jax.ShapeDtypeStruct((batch, tq, ATT_WIDTH), BF16),
        scratch_shapes=[
            pltpu.VMEM((lp + tk, qb), F32),
            pltpu.VMEM((lp, qb), jnp.int16),
            pltpu.VMEM((lp, qb), jnp.int16),
            pltpu.VMEM((ATT_HEADS, tk, qb), BF16),
            pltpu.VMEM((ATT_HEADS, tk, qb), BF16),
            pltpu.VMEM((ATT_WIDTH, qb), F32),
        ],
        compiler_params=_compiler_params(("parallel", "arbitrary")),
        name="attention",
    )(lim, q_a, q_i, w_t, k_i, k_a, v_t)


FF_CHUNK = 512


def _mlp_kernel(x_ref, yr_ref, ya_ref, wo_ref, g2_ref, wup_ref, wdn_ref, gf_ref, y_ref, x_scr, h_scr, a_scr):
    x = x_ref[...] + jnp.dot(yr_ref[...], wo_ref[:RET_WIDTH, :], preferred_element_type=F32)
    x = x + jnp.dot(ya_ref[...], wo_ref[RET_WIDTH:, :], preferred_element_type=F32)
    x_scr[...] = x
    ms = jnp.mean(x * x, axis=-1, keepdims=True)
    h_scr[...] = ((x * lax.rsqrt(ms + NORM_EPS)) * g2_ref[...]).astype(h_scr.dtype)
    for c in range(D_FF // FF_CHUNK):
        cols = slice(c * FF_CHUNK, (c + 1) * FF_CHUNK)
        u = jnp.dot(h_scr[...], wup_ref[:, cols], preferred_element_type=F32)
        a_scr[:, cols] = jnp.square(jnp.maximum(u, 0.0)).astype(a_scr.dtype)
    x = x_scr[...] + jnp.dot(a_scr[...], wdn_ref[...], preferred_element_type=F32)
    ms = jnp.mean(x * x, axis=-1, keepdims=True)
    y_ref[...] = (x * lax.rsqrt(ms + NORM_EPS)) * gf_ref[...]


def _mlp(x2d, y_r, y_a, w_out_b, norm2, w_up_b, w_down_b, norm_final, tm):
    m = x2d.shape[0]
    row = lambda i: (i, 0)
    const = lambda i: (0, 0)
    return pl.pallas_call(
        _mlp_kernel,
        grid=(m // tm,),
        in_specs=[
            pl.BlockSpec((tm, D_MODEL), row),
            pl.BlockSpec((tm, RET_WIDTH), row),
            pl.BlockSpec((tm, ATT_WIDTH), row),
            _resident((D_MODEL, D_MODEL), const),
            _resident((1, D_MODEL), const),
            _resident((D_MODEL, D_FF), const),
            _resident((D_FF, D_MODEL), const),
            _resident((1, D_MODEL), const),
        ],
        out_specs=pl.BlockSpec((tm, D_MODEL), row),
        out_shape=jax.ShapeDtypeStruct((m, D_MODEL), F32),
        scratch_shapes=[pltpu.VMEM((tm, D_MODEL), F32), pltpu.VMEM((tm, D_MODEL), BF16),
                        pltpu.VMEM((tm, D_FF), BF16)],
        compiler_params=_compiler_params(("parallel",)),
        name="mlp",
    )(x2d, y_r, y_a, w_out_b, norm2, w_up_b, w_down_b, norm_final)


def _pick_tile(n, target):
    t = min(n, target)
    while n % t:
        t //= 2
    return t


def _layer(x, pos, past, s0, ret_blk, weights, *, attn_qb, attn_tk):
    norm1, w_in_b, gn_gain, w_out_b, norm2, w_up_b, w_down_b, norm_final = weights
    batch, seq, _ = x.shape
    m = batch * seq
    x2d = x.reshape(m, D_MODEL)
    tm = _pick_tile(m, 512)

    tabs = _rope_tables(pos)
    if seq < tm:
        tabs = jnp.tile(tabs, (1, tm // seq, 1))
    ret_in, q_a, q_i, kw, k_a, v_a, ka16, vt16, kw16 = _proj(x2d, norm1, w_in_b, tabs, tm)

    y_r, s_new = _retention(ret_in, s0, gn_gain, batch, seq, ret_blk)

    three = lambda a: a.reshape(batch, seq, -1)
    k_i = three(kw[:, :IDX_DH])
    w_t = jnp.swapaxes(three(kw[:, IDX_DH:IDX_DH + IDX_HEADS]) * IDX_HEADS ** -0.5 * IDX_DH ** -0.5, 1, 2)
    q_a3, q_i3 = three(q_a), three(q_i)
    n_keys = seq if past is None else past[0].shape[1] + seq
    topk = min(TOPK_MAX, n_keys // 4)
    tq = -(-seq // attn_qb) * attn_qb
    lp = -(-n_keys // attn_tk) * attn_tk
    padq = lambda a: jnp.pad(a, ((0, 0), (0, tq - seq), (0, 0)))
    padk = lambda a: jnp.pad(a, ((0, 0), (0, lp - n_keys), (0, 0)))
    if past is None and attn_tk == vt16.shape[2] and lp == n_keys:
        keys_i, keys_a = three(kw16), three(ka16)
        v_t = vt16.reshape(batch, lp // attn_tk, ATT_WIDTH, attn_tk)
        n_valid_q = attn_qb
    else:
        keys_i, keys_a, vals = three(kw16), three(ka16), three(v_a).astype(BF16)
        n_valid_q = attn_qb
        if past is not None:
            past_k, past_v, past_ki = past
            pad_i = jnp.pad(past_ki, ((0, 0), (0, 0), (0, LANES - IDX_DH))).astype(BF16)
            keys_i = jnp.concatenate([pad_i, keys_i], axis=1)
            keys_a = jnp.concatenate([past_k.reshape(batch, -1, ATT_WIDTH).astype(BF16), keys_a], axis=1)
            vals = jnp.concatenate([past_v.reshape(batch, -1, ATT_WIDTH).astype(BF16), vals], axis=1)
            n_valid_q = seq
        keys_i, keys_a = padk(keys_i), padk(keys_a)
        v_t = jnp.swapaxes(padk(vals).reshape(batch, lp // attn_tk, attn_tk, ATT_WIDTH), 2, 3)
    lim = jnp.minimum((pos // CHUNK + 1) * CHUNK, n_keys).astype(jnp.int32)
    lim = jnp.pad(lim, (0, tq - seq), constant_values=n_keys).reshape(tq // attn_qb, 1, attn_qb)
    y_a = _attention(
        padq(q_a3), padq(q_i3), jnp.pad(w_t, ((0, 0), (0, 0), (0, tq - seq))), lim, keys_i, keys_a, v_t,
        qb=attn_qb, tk=attn_tk, n_keys=n_keys, topk=topk, causal=past is None, n_valid_q=n_valid_q)
    y_a = y_a[:, :seq].reshape(m, ATT_WIDTH)

    y = _mlp(x2d, y_r, y_a, w_out_b, norm2, w_up_b, w_down_b, norm_final, tm)
    return (y.reshape(batch, seq, D_MODEL),
            k_a.reshape(batch, seq, ATT_HEADS, ATT_DH),
            v_a.reshape(batch, seq, ATT_HEADS, ATT_DH),
            k_i, s_new)


def kernel(x_prompt, x_sample, cache_k_att, cache_v_att, cache_k_idx, state_ret, norm1, w_in, gn_gain, w_out,
           norm2, w_up, w_down, norm_final):
    depth = norm1.shape[0]
    assert depth == 1, "the final norm is fused into the layer's last kernel"
    past_len = cache_k_att.shape[2]
    pos_p = jnp.arange(x_prompt.shape[1], dtype=jnp.int32)
    pos_s = past_len + jnp.arange(x_sample.shape[1], dtype=jnp.int32)
    l = 0
    w_in_b = jnp.pad(w_in[l], ((0, 0), (0, IN_WIDTH_PADDED - IN_WIDTH))).astype(BF16)
    weights = (norm1[l][None], w_in_b, gn_gain[l][None], w_out[l].astype(BF16), norm2[l][None],
               w_up[l].astype(BF16), w_down[l].astype(BF16), norm_final[None])
    s0_p = jnp.zeros((x_prompt.shape[0], RET_HEADS, RET_DK, RET_DV), F32)
    yp, kap, vap, kip, sp = _layer(x_prompt, pos_p, None, s0_p, 256, weights, attn_qb=256, attn_tk=256)
    past = (cache_k_att[l], cache_v_att[l], cache_k_idx[l])
    ys, kas, vas, kis, ss = _layer(x_sample, pos_s, past, state_ret[l].astype(F32), x_sample.shape[1], weights,
                                   attn_qb=128, attn_tk=256)
    return (yp, ys, kap[None], vap[None], kip[None], sp[None], kas[None], vas[None], kis[None], ss[None])
```

```python
import functools

import jax
import jax.numpy as jnp
import numpy as np
from jax import lax
from jax.experimental import pallas as pl
from jax.experimental.pallas import tpu as pltpu

D_MODEL = 1024
CHUNK = 64
ROPE_THETA = 10000.0
NORM_EPS = 1e-6
NEG = -1e30
RET_WIDTH = 512
ATT_WIDTH = 512
RET_HEADS = 4
RET_DK = 128
RET_DV = 128
ATT_HEADS = 8
ATT_DH = 64
IDX_HEADS = 8
IDX_DH = 64
TOPK_MAX = 256
D_FF = 4 * D_MODEL
IN_WIDTH = 4 * RET_WIDTH + 3 * ATT_WIDTH + IDX_HEADS * IDX_DH + IDX_DH + IDX_HEADS

LANES = 128
IN_WIDTH_PADDED = ((IN_WIDTH + LANES - 1) // LANES) * LANES
VMEM_LIMIT_BYTES = 56 * 1024 * 1024

F32 = jnp.float32
BF16 = jnp.bfloat16

_NT = (((1,), (1,)), ((), ()))


def _compiler_params(semantics):
    return pltpu.CompilerParams(dimension_semantics=semantics, vmem_limit_bytes=VMEM_LIMIT_BYTES)


def _resident(block_shape, index_map):
    return pl.BlockSpec(block_shape, index_map, pipeline_mode=pl.Buffered(1))


def _rope_tables(pos):
    posf = pos.astype(F32)[:, None]

    def angles(d):
        half = d // 2
        inv = ROPE_THETA ** (-jnp.arange(half, dtype=F32) * 2.0 / d)
        ang = posf * inv[None, :]
        return jnp.cos(ang), jnp.sin(ang)

    c128, s128 = angles(128)
    zero64 = jnp.zeros_like(s128)
    cos128 = jnp.concatenate([c128, c128], axis=1)
    sin128 = jnp.concatenate([-s128, s128], axis=1)
    c64, s64 = angles(64)
    zero32 = jnp.zeros_like(s64)
    cos64 = jnp.concatenate([c64, c64, c64, c64], axis=1)
    sin64_up = jnp.concatenate([zero32, s64, zero32, s64], axis=1)
    sin64_dn = jnp.concatenate([-s64, zero32, -s64, zero32], axis=1)
    one64 = jnp.ones_like(zero64)
    cosk = jnp.concatenate([c64, c64, one64], axis=1)
    sink_up = jnp.concatenate([zero32, s64, zero64], axis=1)
    sink_dn = jnp.concatenate([-s64, zero32, zero64], axis=1)
    return jnp.stack([cos128, sin128, cos64, sin64_up, sin64_dn, cosk, sink_up, sink_dn])


V_TILE = 256


def _proj_kernel(x_ref, g_ref, w_ref, tab_ref, ret_ref, qa_ref, qi_ref, kw_ref, ka_ref, va_ref,
                 ka16_ref, vt16_ref, kw16_ref):
    x = x_ref[...]
    ms = jnp.mean(x * x, axis=-1, keepdims=True)
    h = ((x * lax.rsqrt(ms + NORM_EPS)) * g_ref[...]).astype(BF16)

    def mm(c0, n):
        return jnp.dot(h, w_ref[:, c0:c0 + n], preferred_element_type=F32)

    def rope128(z):
        return z * tab_ref[0] + pltpu.roll(z, 64, 1) * tab_ref[1]

    def rope64(z):
        return z * tab_ref[2] + pltpu.roll(z, 32, 1) * tab_ref[3] + pltpu.roll(z, 96, 1) * tab_ref[4]

    def ropek(z):
        return z * tab_ref[5] + pltpu.roll(z, 32, 1) * tab_ref[6] + pltpu.roll(z, 96, 1) * tab_ref[7]

    def emit(out_ref, out_c0, w_c0, rope):
        z = mm(w_c0, 512)
        for t in range(4):
            zt = z[:, t * LANES:(t + 1) * LANES]
            if rope is not None:
                zt = rope(zt)
            out_ref[:, out_c0 + t * LANES:out_c0 + (t + 1) * LANES] = zt.astype(out_ref.dtype)

    emit(ret_ref, 0, 0, rope128)
    emit(ret_ref, 512, 512, rope128)
    emit(ret_ref, 1024, 1024, None)
    emit(ret_ref, 1536, 1536, None)
    emit(qa_ref, 0, 2048, rope64)
    emit(qi_ref, 0, 3584, rope64)
    k_a = mm(2560, 512)
    k_a = jnp.concatenate([rope64(k_a[:, t * LANES:(t + 1) * LANES]) for t in range(4)], axis=1)
    ka_ref[...] = k_a
    ka16_ref[...] = k_a.astype(ka16_ref.dtype)
    v_a = mm(3072, 512)
    va_ref[...] = v_a
    v_tile = vt16_ref.shape[2]
    for part in range(vt16_ref.shape[0]):
        vt16_ref[part] = v_a[part * v_tile:(part + 1) * v_tile, :].T.astype(vt16_ref.dtype)
    kw = ropek(mm(4096, LANES))
    kw_ref[...] = kw
    kw16_ref[...] = kw.astype(kw16_ref.dtype)


def _proj(x2d, norm_g, w_in_b, tabs, tm):
    m = x2d.shape[0]
    n_tab = tabs.shape[1] // tm
    row = lambda i: (i, 0)
    row3 = lambda i: (i, 0, 0)
    assert tm % V_TILE == 0 or m == tm, (tm, m)
    n_vt = max(tm // V_TILE, 1)
    v_tile = min(V_TILE, tm)
    outs = (
        jax.ShapeDtypeStruct((m, 4 * RET_WIDTH), F32),
        jax.ShapeDtypeStruct((m, ATT_WIDTH), F32),
        jax.ShapeDtypeStruct((m, IDX_HEADS * IDX_DH), F32),
        jax.ShapeDtypeStruct((m, LANES), F32),
        jax.ShapeDtypeStruct((m, ATT_WIDTH), F32),
        jax.ShapeDtypeStruct((m, ATT_WIDTH), F32),
        jax.ShapeDtypeStruct((m, ATT_WIDTH), BF16),
        jax.ShapeDtypeStruct((m // v_tile, ATT_WIDTH, v_tile), BF16),
        jax.ShapeDtypeStruct((m, LANES), BF16),
    )
    return pl.pallas_call(
        _proj_kernel,
        grid=(m // tm,),
        in_specs=[
            pl.BlockSpec((tm, D_MODEL), row),
            _resident((1, D_MODEL), lambda i: (0, 0)),
            _resident((D_MODEL, IN_WIDTH_PADDED), lambda i: (0, 0)),
            pl.BlockSpec((8, tm, LANES), lambda i: (0, i % n_tab, 0)),
        ],
        out_specs=[
            pl.BlockSpec((tm, 4 * RET_WIDTH), row),
            pl.BlockSpec((tm, ATT_WIDTH), row),
            pl.BlockSpec((tm, IDX_HEADS * IDX_DH), row),
            pl.BlockSpec((tm, LANES), row),
            pl.BlockSpec((tm, ATT_WIDTH), row),
            pl.BlockSpec((tm, ATT_WIDTH), row),
            pl.BlockSpec((tm, ATT_WIDTH), row),
            pl.BlockSpec((n_vt, ATT_WIDTH, v_tile), row3),
            pl.BlockSpec((tm, LANES), row),
        ],
        out_shape=outs,
        compiler_params=_compiler_params(("parallel",)),
        name="proj",
    )(x2d, norm_g, w_in_b, tabs)


def _retention_tables(blk):
    lg = jnp.log1p(-jnp.exp2(-5.0 - jnp.arange(RET_HEADS, dtype=F32)))
    i = jnp.arange(blk, dtype=F32)
    diff = i[:, None] - i[None, :]
    dmat = jnp.where(diff[None] >= 0, jnp.exp(lg[:, None, None] * jnp.maximum(diff, 0.0)[None]), 0.0)
    w_end = jnp.exp(lg[:, None] * (blk - 1.0 - i)[None, :])
    w_q = jnp.exp(lg[:, None] * (i + 1.0)[None, :])
    g_block = jnp.exp(lg * blk)
    lane = lambda a: jnp.broadcast_to(a[:, :, None], (RET_HEADS, blk, LANES))
    return dmat, lane(w_q), lane(w_end), g_block


def _retention_kernel(gblk_ref, q_ref, k_ref, v_ref, g_ref, s0_ref, dmat_ref, wq_ref, wend_ref, gain_ref,
                      y_ref, sfin_ref, s_scr):
    t = pl.program_id(1)

    @pl.when(t == 0)
    def _():
        s_scr[...] = s0_ref[0]

    for hd in range(RET_HEADS):
        cols = slice(hd * RET_DK, (hd + 1) * RET_DK)
        q = q_ref[:, cols]
        k = k_ref[:, cols] * RET_DK ** -0.5
        vb = v_ref[:, cols].astype(BF16)
        state = s_scr[hd]
        scores = lax.dot_general(q.astype(BF16), k.astype(BF16), _NT, preferred_element_type=F32) * dmat_ref[hd]
        o = jnp.dot(scores.astype(BF16), vb, preferred_element_type=F32)
        o = o + jnp.dot((q * wq_ref[hd]).astype(BF16), state.astype(BF16), preferred_element_type=F32)
        k_end_t = (k * wend_ref[hd]).T.astype(BF16)
        s_scr[hd] = gblk_ref[hd] * state + jnp.dot(k_end_t, vb, preferred_element_type=F32)
        mu = jnp.mean(o, axis=-1, keepdims=True)
        var = jnp.mean(jnp.square(o - mu), axis=-1, keepdims=True)
        on = ((o - mu) * lax.rsqrt(var + NORM_EPS)) * gain_ref[:, cols]
        g = g_ref[:, cols]
        y_ref[:, cols] = ((g * jax.nn.sigmoid(g)) * on).astype(y_ref.dtype)

    @pl.when(t == pl.num_programs(1) - 1)
    def _():
        sfin_ref[0] = s_scr[...]


def _retention(ret_in, s0, gn_gain, batch, seq, blk):
    nt = seq // blk
    dmat, w_q, w_end, g_block = _retention_tables(blk)

    def col(c):
        return pl.BlockSpec((blk, RET_WIDTH), lambda b, t: (b * nt + t, c))

    const3 = lambda b, t: (0, 0, 0)
    return pl.pallas_call(
        _retention_kernel,
        grid=(batch, nt),
        in_specs=[
            pl.BlockSpec(memory_space=pltpu.SMEM),
            col(0), col(1), col(2), col(3),
            pl.BlockSpec((1, RET_HEADS, RET_DK, RET_DV), lambda b, t: (b, 0, 0, 0)),
            pl.BlockSpec((RET_HEADS, blk, blk), const3),
            pl.BlockSpec((RET_HEADS, blk, LANES), const3),
            pl.BlockSpec((RET_HEADS, blk, LANES), const3),
            pl.BlockSpec((1, RET_WIDTH), lambda b, t: (0, 0)),
        ],
        out_specs=[
            pl.BlockSpec((blk, RET_WIDTH), lambda b, t: (b * nt + t, 0)),
            pl.BlockSpec((1, RET_HEADS, RET_DK, RET_DV), lambda b, t: (b, 0, 0, 0)),
        ],
        out_shape=(
            jax.ShapeDtypeStruct((batch * seq, RET_WIDTH), BF16),
            jax.ShapeDtypeStruct((batch, RET_HEADS, RET_DK, RET_DV), F32),
        ),
        scratch_shapes=[pltpu.VMEM((RET_HEADS, RET_DK, RET_DV), F32)],
        compiler_params=_compiler_params(("parallel", "arbitrary")),
        name="retention",
    )(g_block, ret_in, ret_in, ret_in, ret_in, s0, dmat, w_q, w_end, gn_gain)


COUNT_ROWS = 32
COUNT_UNROLL = 4
MASKED = 2.0 * NEG
LOG2_E = 1.4426950408889634


def _ordered_bits_to_float(u):
    bits = jnp.where(u < 0, u ^ jnp.int32(-2 ** 31), ~u)
    return lax.bitcast_convert_type(bits, F32)


def _signed_half(x):
    return (x - 2 ** 15).astype(jnp.int16)


def _attention_kernel(lim_ref, qa_ref, qi_ref, wt_ref, ki_ref, ka_ref, vt_ref, o_ref,
                      s_scr, hi16_scr, lo16_scr, xa_scr, xb_scr, acc_scr,
                      *, qb, tk, n_keys, topk, causal, n_valid_q, index_bits):
    n_kt = pl.program_id(1) * (qb // tk) + (qb // tk) if causal else ki_ref.shape[0] // tk
    lim = lim_ref[0]
    lane = lax.broadcasted_iota(jnp.int32, (qb, LANES), 1)
    key_iota = lax.broadcasted_iota(jnp.int32, (tk, qb), 0)

    def head_operand(ref, hd, scale, shift_to_low):
        pair = ref[:, (hd // 2) * LANES:(hd // 2 + 1) * LANES] * scale
        if shift_to_low and hd % 2:
            pair = pltpu.roll(pair, 64, 1)
        keep = (lane < 64) if (shift_to_low or hd % 2 == 0) else (lane >= 64)
        return jnp.where(keep, pair, 0.0).T.astype(BF16)

    qi_heads = [head_operand(qi_ref, hd, 1.0, True) for hd in range(IDX_HEADS)]
    wt = wt_ref[0]

    def score_body(kt, carry, some_inadmissible):
        r0 = pl.multiple_of(kt * tk, tk)
        keys = ki_ref[pl.ds(r0, tk), :]
        acc = jnp.zeros((tk, qb), F32)
        for hd in range(IDX_HEADS):
            d = jnp.dot(keys, qi_heads[hd], preferred_element_type=F32)
            acc = acc + jnp.maximum(d, 0.0) * wt[hd:hd + 1, :]
        s = jnp.where(acc == 0.0, 0.0, acc)
        if some_inadmissible:
            s = jnp.where(key_iota + r0 < lim, s, -jnp.inf)
        s_scr[pl.ds(r0, tk), :] = s
        bits = lax.bitcast_convert_type(s, jnp.int32)
        ordered = bits ^ (lax.shift_right_arithmetic(bits, 31) | jnp.int32(-2 ** 31))
        hi16_scr[pl.ds(r0, tk), :] = _signed_half(lax.shift_right_logical(ordered, 16))
        lo16_scr[pl.ds(r0, tk), :] = _signed_half(ordered & 0xFFFF)
        return carry

    n_all_admissible = jnp.minimum(jnp.min(lim) // tk, n_kt)
    lax.fori_loop(0, n_all_admissible, functools.partial(score_body, some_inadmissible=False), 0)
    lax.fori_loop(n_all_admissible, n_kt, functools.partial(score_body, some_inadmissible=True), 0)

    def count(src_ref, indicator, dtype):
        def one_tile(kt, acc):
            for part in range(tk // COUNT_ROWS):
                r0 = pl.multiple_of(kt * tk + part * COUNT_ROWS, COUNT_ROWS)
                acc = acc + indicator(src_ref[pl.ds(r0, COUNT_ROWS), :], r0)
            return acc
        def tile_group(g, acc):
            for j in range(COUNT_UNROLL):
                acc = one_tile(COUNT_UNROLL * g + j, acc)
            return acc

        acc = lax.fori_loop(0, n_kt // COUNT_UNROLL, tile_group, jnp.zeros((COUNT_ROWS, qb), dtype))
        acc = lax.fori_loop(n_kt - n_kt % COUNT_UNROLL, n_kt, one_tile, acc)
        return jnp.sum(acc.astype(F32), axis=0, keepdims=True)

    n_masked = (n_keys - lim).astype(F32)
    kf = float(topk)
    q_lane = lax.broadcasted_iota(jnp.int32, (1, qb), 1)
    one16 = jnp.ones((), BF16)
    zero16 = jnp.zeros((), BF16)

    def bisect_step(it, carry, half_ref):
        cur, cnt = carry
        cand = cur | lax.shift_left(jnp.int32(1), 31 - it)
        if half_ref is hi16_scr:
            half = _signed_half(lax.shift_right_logical(cand, 16))
        else:
            half = _signed_half(cand & 0xFFFF)
        c = count(half_ref, lambda h, r0: jnp.where(h >= half, one16, zero16), BF16)
        ok = c + jnp.where(_ordered_bits_to_float(cand) <= NEG, n_masked, 0.0) >= kf
        return jnp.where(ok, cand, cur), jnp.where(ok, c, cnt)

    carry = (jnp.zeros((1, qb), jnp.int32), jnp.zeros((1, qb), F32))
    carry = lax.fori_loop(0, 16, lambda it, st: bisect_step(it, st, hi16_scr), carry)

    top = _signed_half(lax.shift_right_logical(carry[0], 16))

    def narrow_body(kt, c):
        r0 = pl.multiple_of(kt * tk, tk)
        hi = hi16_scr[pl.ds(r0, tk), :]
        lo = lo16_scr[pl.ds(r0, tk), :]
        lo16_scr[pl.ds(r0, tk), :] = jnp.where(hi > top, jnp.int16(2 ** 15 - 1),
                                               jnp.where(hi == top, lo, jnp.int16(-2 ** 15)))
        return c

    lax.fori_loop(0, n_kt, narrow_body, 0)
    cur, cnt_ge = lax.fori_loop(16, 32, lambda it, st: bisect_step(it, st, lo16_scr), carry)
    theta = _ordered_bits_to_float(cur)

    overflow = jnp.where((cnt_ge > kf) & (q_lane < n_valid_q), 1.0, 0.0)
    all_idx = jnp.full((1, qb), 2 ** index_bits - 1, jnp.int32)

    def tie_cut():
        need = kf - count(s_scr, lambda s, r0: jnp.where(s > theta, 1.0, 0.0), F32)

        def mark_body(kt, c):
            r0 = pl.multiple_of(kt * tk, tk)
            tied = jnp.where(s_scr[pl.ds(r0, tk), :] == theta, key_iota + r0, 2 ** 15 - 1)
            hi16_scr[pl.ds(r0, tk), :] = tied.astype(jnp.int16)
            return c

        lax.fori_loop(0, n_kt, mark_body, 0)

        def body(it, cut):
            cand = cut | lax.shift_left(jnp.int32(1), index_bits - 1 - it)
            cand16 = cand.astype(jnp.int16)
            c = count(hi16_scr, lambda idx, r0: jnp.where(idx < cand16, one16, zero16), BF16)
            return jnp.where(c <= need, cand, cut)

        return lax.fori_loop(0, index_bits, body, jnp.zeros((1, qb), jnp.int32))

    any_overflow = jnp.max(overflow) > 0.0
    idx_cut = lax.cond(any_overflow, tie_cut, lambda: all_idx)

    def bias_body(kt, carry, with_ties):
        r0 = pl.multiple_of(kt * tk, tk)
        s = s_scr[pl.ds(r0, tk), :]
        if with_ties:
            tie = jnp.where(key_iota + r0 < idx_cut, 0.0, MASKED)
            s_scr[pl.ds(r0, tk), :] = jnp.where(s > theta, 0.0, jnp.where(s == theta, tie, MASKED))
        else:
            s_scr[pl.ds(r0, tk), :] = jnp.where(s >= theta, 0.0, MASKED)
        return carry

    lax.cond(any_overflow,
             lambda: lax.fori_loop(0, n_kt, functools.partial(bias_body, with_ties=True), 0),
             lambda: lax.fori_loop(0, n_kt, functools.partial(bias_body, with_ties=False), 0))

    lp = ki_ref.shape[0]
    s_scr[pl.ds(lp, tk), :] = jnp.full((tk, qb), MASKED, F32)
    qa_heads = [head_operand(qa_ref, hd, ATT_DH ** -0.5 * LOG2_E, False) for hd in range(ATT_HEADS)]
    acc_scr[...] = jnp.zeros(acc_scr.shape, F32)

    def logits_stage(kt, x_ref):
        kt_in = jnp.minimum(kt, n_kt - 1)
        r0 = pl.multiple_of(kt_in * tk, tk)
        b0 = pl.multiple_of(jnp.where(kt < n_kt, kt_in * tk, lp), tk)
        tops = []
        for hd in range(ATT_HEADS):
            keys = ka_ref[pl.ds(r0, tk), (hd // 2) * LANES:(hd // 2 + 1) * LANES]
            x = jnp.dot(keys, qa_heads[hd], preferred_element_type=F32) + s_scr[pl.ds(b0, tk), :]
            x_ref[hd] = x.astype(x_ref.dtype)
            tops.append(jnp.max(x, axis=0, keepdims=True))
        return jnp.concatenate(tops, axis=0).astype(x_ref.dtype).astype(F32)

    def accumulate(kt, x_ref, tile_max, m_run, l_run):
        kt_in = jnp.minimum(kt, n_kt - 1)
        m_new = jnp.maximum(m_run, tile_max)
        alpha = jnp.exp2(m_run - m_new)
        m_staged = m_new.astype(x_ref.dtype)
        sums = []
        for hd in range(ATT_HEADS):
            p = jnp.exp2(x_ref[hd] - m_staged[hd:hd + 1, :]).astype(BF16)
            sums.append(jnp.sum(p.astype(F32), axis=0, keepdims=True))
            rows = slice(hd * ATT_DH, (hd + 1) * ATT_DH)
            pv = jnp.dot(vt_ref[kt_in, rows, :], p, preferred_element_type=F32)
            acc_scr[rows, :] = alpha[hd:hd + 1, :] * acc_scr[rows, :] + pv
        return m_new, alpha * l_run + jnp.concatenate(sums, axis=0)

    def pair_body(pair, carry):
        max_a, m_run, l_run = carry
        max_b = logits_stage(2 * pair + 1, xb_scr)
        m_run, l_run = accumulate(2 * pair, xa_scr, max_a, m_run, l_run)
        max_a = logits_stage(2 * pair + 2, xa_scr)
        m_run, l_run = accumulate(2 * pair + 1, xb_scr, max_b, m_run, l_run)
        return max_a, m_run, l_run

    first_max = logits_stage(0, xa_scr)
    _, _, l_fin = lax.fori_loop(
        0, (n_kt + 1) // 2, pair_body,
        (first_max, jnp.full((ATT_HEADS, qb), NEG, F32), jnp.zeros((ATT_HEADS, qb), F32)))

    for hd in range(ATT_HEADS):
        rows = slice(hd * ATT_DH, (hd + 1) * ATT_DH)
        acc_scr[rows, :] = acc_scr[rows, :] / l_fin[hd:hd + 1, :]
    o_ref[...] = acc_scr[...].T.astype(o_ref.dtype)


def _attention(q_a, q_i, w_t, lim, k_i, k_a, v_t, *, qb, tk, n_keys, topk, causal, n_valid_q):
    batch, tq, _ = q_a.shape
    lp = k_i.shape[1]
    nq = tq // qb
    index_bits = int(np.ceil(np.log2(lp + 1)))
    assert lp // COUNT_ROWS <= 256, "bfloat16 partial counts are exact only up to 256"
    kernel = functools.partial(_attention_kernel, qb=qb, tk=tk, n_keys=n_keys, topk=topk, causal=causal,
                               n_valid_q=n_valid_q, index_bits=index_bits)
    return pl.pallas_call(
        kernel,
        grid=(batch, nq),
        in_specs=[
            pl.BlockSpec((1, 1, qb), lambda b, i: (i, 0, 0)),
            pl.BlockSpec((None, qb, ATT_WIDTH), lambda b, i: (b, i, 0)),
            pl.BlockSpec((None, qb, IDX_HEADS * IDX_DH), lambda b, i: (b, i, 0)),
            pl.BlockSpec((1, IDX_HEADS, qb), lambda b, i: (b, 0, i)),
            _resident((None, lp, LANES), lambda b, i: (b, 0, 0)),
            _resident((None, lp, ATT_WIDTH), lambda b, i: (b, 0, 0)),
            _resident((None, lp // tk, ATT_WIDTH, tk), lambda b, i: (b, 0, 0, 0)),
        ],
        out_specs=pl.BlockSpec((None, qb, ATT_WIDTH), lambda b, i: (b, i, 0)),
        out_shape=jax.ShapeDtypeStruct((batch, tq, ATT_WIDTH), BF16),
        scratch_shapes=[
            pltpu.VMEM((lp + tk, qb), F32),
            pltpu.VMEM((lp, qb), jnp.int16),
            pltpu.VMEM((lp, qb), jnp.int16),
            pltpu.VMEM((ATT_HEADS, tk, qb), BF16),
            pltpu.VMEM((ATT_HEADS, tk, qb), BF16),
            pltpu.VMEM((ATT_WIDTH, qb), F32),
        ],
        compiler_params=_compiler_params(("parallel", "arbitrary")),
        name="attention",
    )(lim, q_a, q_i, w_t, k_i, k_a, v_t)


FF_CHUNK = 512


def _mlp_kernel(x_ref, yr_ref, ya_ref, wo_ref, g2_ref, wup_ref, wdn_ref, gf_ref, y_ref, x_scr, h_scr, a_scr):
    x = x_ref[...] + jnp.dot(yr_ref[...], wo_ref[:RET_WIDTH, :], preferred_element_type=F32)
    x = x + jnp.dot(ya_ref[...], wo_ref[RET_WIDTH:, :], preferred_element_type=F32)
    x_scr[...] = x
    ms = jnp.mean(x * x, axis=-1, keepdims=True)
    h_scr[...] = ((x * lax.rsqrt(ms + NORM_EPS)) * g2_ref[...]).astype(h_scr.dtype)
    for c in range(D_FF // FF_CHUNK):
        cols = slice(c * FF_CHUNK, (c + 1) * FF_CHUNK)
        u = jnp.dot(h_scr[...], wup_ref[:, cols], preferred_element_type=F32)
        a_scr[:, cols] = jnp.square(jnp.maximum(u, 0.0)).astype(a_scr.dtype)
    x = x_scr[...] + jnp.dot(a_scr[...], wdn_ref[...], preferred_element_type=F32)
    ms = jnp.mean(x * x, axis=-1, keepdims=True)
    y_ref[...] = (x * lax.rsqrt(ms + NORM_EPS)) * gf_ref[...]


def _mlp(x2d, y_r, y_a, w_out_b, norm2, w_up_b, w_down_b, norm_final, tm):
    m = x2d.shape[0]
    row = lambda i: (i, 0)
    const = lambda i: (0, 0)
    return pl.pallas_call(
        _mlp_kernel,
        grid=(m // tm,),
        in_specs=[
            pl.BlockSpec((tm, D_MODEL), row),
            pl.BlockSpec((tm, RET_WIDTH), row),
            pl.BlockSpec((tm, ATT_WIDTH), row),
            _resident((D_MODEL, D_MODEL), const),
            _resident((1, D_MODEL), const),
            _resident((D_MODEL, D_FF), const),
            _resident((D_FF, D_MODEL), const),
            _resident((1, D_MODEL), const),
        ],
        out_specs=pl.BlockSpec((tm, D_MODEL), row),
        out_shape=jax.ShapeDtypeStruct((m, D_MODEL), F32),
        scratch_shapes=[pltpu.VMEM((tm, D_MODEL), F32), pltpu.VMEM((tm, D_MODEL), BF16),
                        pltpu.VMEM((tm, D_FF), BF16)],
        compiler_params=_compiler_params(("parallel",)),
        name="mlp",
    )(x2d, y_r, y_a, w_out_b, norm2, w_up_b, w_down_b, norm_final)


def _pick_tile(n, target):
    t = min(n, target)
    while n % t:
        t //= 2
    return t


def _layer(x, pos, past, s0, ret_blk, weights, *, attn_qb, attn_tk):
    norm1, w_in_b, gn_gain, w_out_b, norm2, w_up_b, w_down_b, norm_final = weights
    batch, seq, _ = x.shape
    m = batch * seq
    x2d = x.reshape(m, D_MODEL)
    tm = _pick_tile(m, 512)

    tabs = _rope_tables(pos)
    if seq < tm:
        tabs = jnp.tile(tabs, (1, tm // seq, 1))
    ret_in, q_a, q_i, kw, k_a, v_a, ka16, vt16, kw16 = _proj(x2d, norm1, w_in_b, tabs, tm)

    y_r, s_new = _retention(ret_in, s0, gn_gain, batch, seq, ret_blk)

    three = lambda a: a.reshape(batch, seq, -1)
    k_i = three(kw[:, :IDX_DH])
    w_t = jnp.swapaxes(three(kw[:, IDX_DH:IDX_DH + IDX_HEADS]) * IDX_HEADS ** -0.5 * IDX_DH ** -0.5, 1, 2)
    q_a3, q_i3 = three(q_a), three(q_i)
    n_keys = seq if past is None else past[0].shape[1] + seq
    topk = min(TOPK_MAX, n_keys // 4)
    tq = -(-seq // attn_qb) * attn_qb
    lp = -(-n_keys // attn_tk) * attn_tk
    padq = lambda a: jnp.pad(a, ((0, 0), (0, tq - seq), (0, 0)))
    padk = lambda a: jnp.pad(a, ((0, 0), (0, lp - n_keys), (0, 0)))
    if past is None and attn_tk == vt16.shape[2] and lp == n_keys:
        keys_i, keys_a = three(kw16), three(ka16)
        v_t = vt16.reshape(batch, lp // attn_tk, ATT_WIDTH, attn_tk)
        n_valid_q = attn_qb
    else:
        keys_i, keys_a, vals = three(kw16), three(ka16), three(v_a).astype(BF16)
        n_valid_q = attn_qb
        if past is not None:
            past_k, past_v, past_ki = past
            pad_i = jnp.pad(past_ki, ((0, 0), (0, 0), (0, LANES - IDX_DH))).astype(BF16)
            keys_i = jnp.concatenate([pad_i, keys_i], axis=1)
            keys_a = jnp.concatenate([past_k.reshape(batch, -1, ATT_WIDTH).astype(BF16), keys_a], axis=1)
            vals = jnp.concatenate([past_v.reshape(batch, -1, ATT_WIDTH).astype(BF16), vals], axis=1)
            n_valid_q = seq
        keys_i, keys_a = padk(keys_i), padk(keys_a)
        v_t = jnp.swapaxes(padk(vals).reshape(batch, lp // attn_tk, attn_tk, ATT_WIDTH), 2, 3)
    lim = jnp.minimum((pos // CHUNK + 1) * CHUNK, n_keys).astype(jnp.int32)
    lim = jnp.pad(lim, (0, tq - seq), constant_values=n_keys).reshape(tq // attn_qb, 1, attn_qb)
    y_a = _attention(
        padq(q_a3), padq(q_i3), jnp.pad(w_t, ((0, 0), (0, 0), (0, tq - seq))), lim, keys_i, keys_a, v_t,
        qb=attn_qb, tk=attn_tk, n_keys=n_keys, topk=topk, causal=past is None, n_valid_q=n_valid_q)
    y_a = y_a[:, :seq].reshape(m, ATT_WIDTH)

    y = _mlp(x2d, y_r, y_a, w_out_b, norm2, w_up_b, w_down_b, norm_final, tm)
    return (y.reshape(batch, seq, D_MODEL),
            k_a.reshape(batch, seq, ATT_HEADS, ATT_DH),
            v_a.reshape(batch, seq, ATT_HEADS, ATT_DH),
            k_i, s_new)


def kernel(x_prompt, x_sample, cache_k_att, cache_v_att, cache_k_idx, state_ret, norm1, w_in, gn_gain, w_out,
           norm2, w_up, w_down, norm_final):
    depth = norm1.shape[0]
    assert depth == 1, "the final norm is fused into the layer's last kernel"
    past_len = cache_k_att.shape[2]
    pos_p = jnp.arange(x_prompt.shape[1], dtype=jnp.int32)
    pos_s = past_len + jnp.arange(x_sample.shape[1], dtype=jnp.int32)
    l = 0
    w_in_b = jnp.pad(w_in[l], ((0, 0), (0, IN_WIDTH_PADDED - IN_WIDTH))).astype(BF16)
    weights = (norm1[l][None], w_in_b, gn_gain[l][None], w_out[l].astype(BF16), norm2[l][None],
               w_up[l].astype(BF16), w_down[l].astype(BF16), norm_final[None])
    s0_p = jnp.zeros((x_prompt.shape[0], RET_HEADS, RET_DK, RET_DV), F32)
    yp, kap, vap, kip, sp = _layer(x_prompt, pos_p, None, s0_p, 256, weights, attn_qb=256, attn_tk=256)
    past = (cache_k_att[l], cache_v_att[l], cache_k_idx[l])
    ys, kas, vas, kis, ss = _layer(x_sample, pos_s, past, state_ret[l].astype(F32), x_sample.shape[1], weights,
                                   attn_qb=128, attn_tk=256)
    return (yp, ys, kap[None], vap[None], kip[None], sp[None], kas[None], vas[None], kis[None], ss[None])
```

```python
import functools

import jax
import jax.numpy as jnp
import numpy as np
from jax import lax
from jax.experimental import pallas as pl
from jax.experimental.pallas import tpu as pltpu

D_MODEL = 1024
CHUNK = 64
ROPE_THETA = 10000.0
NORM_EPS = 1e-6
NEG = -1e30
RET_WIDTH = 512
ATT_WIDTH = 512
RET_HEADS = 4
RET_DK = 128
RET_DV = 128
ATT_HEADS = 8
ATT_DH = 64
IDX_HEADS = 8
IDX_DH = 64
TOPK_MAX = 256
D_FF = 4 * D_MODEL
IN_WIDTH = 4 * RET_WIDTH + 3 * ATT_WIDTH + IDX_HEADS * IDX_DH + IDX_DH + IDX_HEADS

LANES = 128
IN_WIDTH_PADDED = ((IN_WIDTH + LANES - 1) // LANES) * LANES
VMEM_LIMIT_BYTES = 56 * 1024 * 1024

ROW_TILE = 512
KEY_TILE = 256
PROMPT_QUERY_BLOCK = 256
SAMPLE_QUERY_BLOCK = 128
PROMPT_RETENTION_BLOCK = 256

F32 = jnp.float32
BF16 = jnp.bfloat16

_NT = (((1,), (1,)), ((), ()))


def _compiler_params(semantics):
    return pltpu.CompilerParams(dimension_semantics=semantics, vmem_limit_bytes=VMEM_LIMIT_BYTES)


def _resident(block_shape, index_map):
    return pl.BlockSpec(block_shape, index_map, pipeline_mode=pl.Buffered(1))


def _rope_tables(pos):
    posf = pos.astype(F32)[:, None]

    def angles(d):
        half = d // 2
        inv = ROPE_THETA ** (-jnp.arange(half, dtype=F32) * 2.0 / d)
        ang = posf * inv[None, :]
        return jnp.cos(ang), jnp.sin(ang)

    c128, s128 = angles(128)
    zero64 = jnp.zeros_like(s128)
    cos128 = jnp.concatenate([c128, c128], axis=1)
    sin128 = jnp.concatenate([-s128, s128], axis=1)
    c64, s64 = angles(64)
    zero32 = jnp.zeros_like(s64)
    cos64 = jnp.concatenate([c64, c64, c64, c64], axis=1)
    sin64_up = jnp.concatenate([zero32, s64, zero32, s64], axis=1)
    sin64_dn = jnp.concatenate([-s64, zero32, -s64, zero32], axis=1)
    one64 = jnp.ones_like(zero64)
    cosk = jnp.concatenate([c64, c64, one64], axis=1)
    sink_up = jnp.concatenate([zero32, s64, zero64], axis=1)
    sink_dn = jnp.concatenate([-s64, zero32, zero64], axis=1)
    return jnp.stack([cos128, sin128, cos64, sin64_up, sin64_dn, cosk, sink_up, sink_dn])


V_TILE = KEY_TILE


def _proj_kernel(x_ref, g_ref, w_ref, tab_ref, ret_ref, qa_ref, qi_ref, kw_ref, ka_ref, va_ref,
                 ka16_ref, vt16_ref, kw16_ref):
    x = x_ref[...]
    ms = jnp.mean(x * x, axis=-1, keepdims=True)
    h = ((x * lax.rsqrt(ms + NORM_EPS)) * g_ref[...]).astype(BF16)

    def mm(c0, n):
        return jnp.dot(h, w_ref[:, c0:c0 + n], preferred_element_type=F32)

    def rope128(z):
        return z * tab_ref[0] + pltpu.roll(z, 64, 1) * tab_ref[1]

    def rope64(z):
        return z * tab_ref[2] + pltpu.roll(z, 32, 1) * tab_ref[3] + pltpu.roll(z, 96, 1) * tab_ref[4]

    def ropek(z):
        return z * tab_ref[5] + pltpu.roll(z, 32, 1) * tab_ref[6] + pltpu.roll(z, 96, 1) * tab_ref[7]

    def emit(out_ref, out_c0, w_c0, rope):
        z = mm(w_c0, 512)
        for t in range(4):
            zt = z[:, t * LANES:(t + 1) * LANES]
            if rope is not None:
                zt = rope(zt)
            out_ref[:, out_c0 + t * LANES:out_c0 + (t + 1) * LANES] = zt.astype(out_ref.dtype)

    emit(ret_ref, 0, 0, rope128)
    emit(ret_ref, 512, 512, rope128)
    emit(ret_ref, 1024, 1024, None)
    emit(ret_ref, 1536, 1536, None)
    emit(qa_ref, 0, 2048, rope64)
    emit(qi_ref, 0, 3584, rope64)
    k_a = mm(2560, 512)
    k_a = jnp.concatenate([rope64(k_a[:, t * LANES:(t + 1) * LANES]) for t in range(4)], axis=1)
    ka_ref[...] = k_a
    ka16_ref[...] = k_a.astype(ka16_ref.dtype)
    v_a = mm(3072, 512)
    va_ref[...] = v_a
    v_tile = vt16_ref.shape[2]
    for part in range(vt16_ref.shape[0]):
        vt16_ref[part] = v_a[part * v_tile:(part + 1) * v_tile, :].T.astype(vt16_ref.dtype)
    kw = ropek(mm(4096, LANES))
    kw_ref[...] = kw
    kw16_ref[...] = kw.astype(kw16_ref.dtype)


def _proj(x2d, norm_g, w_in_b, tabs, tm):
    m = x2d.shape[0]
    n_tab = tabs.shape[1] // tm
    row = lambda i: (i, 0)
    row3 = lambda i: (i, 0, 0)
    assert tm % V_TILE == 0 or m == tm, (tm, m)
    n_vt = max(tm // V_TILE, 1)
    v_tile = min(V_TILE, tm)
    outs = (
        jax.ShapeDtypeStruct((m, 4 * RET_WIDTH), F32),
        jax.ShapeDtypeStruct((m, ATT_WIDTH), F32),
        jax.ShapeDtypeStruct((m, IDX_HEADS * IDX_DH), F32),
        jax.ShapeDtypeStruct((m, LANES), F32),
        jax.ShapeDtypeStruct((m, ATT_WIDTH), F32),
        jax.ShapeDtypeStruct((m, ATT_WIDTH), F32),
        jax.ShapeDtypeStruct((m, ATT_WIDTH), BF16),
        jax.ShapeDtypeStruct((m // v_tile, ATT_WIDTH, v_tile), BF16),
        jax.ShapeDtypeStruct((m, LANES), BF16),
    )
    return pl.pallas_call(
        _proj_kernel,
        grid=(m // tm,),
        in_specs=[
            pl.BlockSpec((tm, D_MODEL), row),
            _resident((1, D_MODEL), lambda i: (0, 0)),
            _resident((D_MODEL, IN_WIDTH_PADDED), lambda i: (0, 0)),
            pl.BlockSpec((8, tm, LANES), lambda i: (0, i % n_tab, 0)),
        ],
        out_specs=[
            pl.BlockSpec((tm, 4 * RET_WIDTH), row),
            pl.BlockSpec((tm, ATT_WIDTH), row),
            pl.BlockSpec((tm, IDX_HEADS * IDX_DH), row),
            pl.BlockSpec((tm, LANES), row),
            pl.BlockSpec((tm, ATT_WIDTH), row),
            pl.BlockSpec((tm, ATT_WIDTH), row),
            pl.BlockSpec((tm, ATT_WIDTH), row),
            pl.BlockSpec((n_vt, ATT_WIDTH, v_tile), row3),
            pl.BlockSpec((tm, LANES), row),
        ],
        out_shape=outs,
        compiler_params=_compiler_params(("parallel",)),
        name="proj",
    )(x2d, norm_g, w_in_b, tabs)


def _retention_tables(blk):
    lg = jnp.log1p(-jnp.exp2(-5.0 - jnp.arange(RET_HEADS, dtype=F32)))
    i = jnp.arange(blk, dtype=F32)
    diff = i[:, None] - i[None, :]
    dmat = jnp.where(diff[None] >= 0, jnp.exp(lg[:, None, None] * jnp.maximum(diff, 0.0)[None]), 0.0)
    w_end = jnp.exp(lg[:, None] * (blk - 1.0 - i)[None, :])
    w_q = jnp.exp(lg[:, None] * (i + 1.0)[None, :])
    g_block = jnp.exp(lg * blk)
    lane = lambda a: jnp.broadcast_to(a[:, :, None], (RET_HEADS, blk, LANES))
    return dmat, lane(w_q), lane(w_end), g_block


def _retention_kernel(gblk_ref, q_ref, k_ref, v_ref, g_ref, s0_ref, dmat_ref, wq_ref, wend_ref, gain_ref,
                      y_ref, sfin_ref, s_scr):
    t = pl.program_id(1)

    @pl.when(t == 0)
    def _():
        s_scr[...] = s0_ref[0]

    for hd in range(RET_HEADS):
        cols = slice(hd * RET_DK, (hd + 1) * RET_DK)
        q = q_ref[:, cols]
        k = k_ref[:, cols] * RET_DK ** -0.5
        vb = v_ref[:, cols].astype(BF16)
        state = s_scr[hd]
        scores = lax.dot_general(q.astype(BF16), k.astype(BF16), _NT, preferred_element_type=F32) * dmat_ref[hd]
        o = jnp.dot(scores.astype(BF16), vb, preferred_element_type=F32)
        o = o + jnp.dot((q * wq_ref[hd]).astype(BF16), state.astype(BF16), preferred_element_type=F32)
        k_end_t = (k * wend_ref[hd]).T.astype(BF16)
        s_scr[hd] = gblk_ref[hd] * state + jnp.dot(k_end_t, vb, preferred_element_type=F32)
        mu = jnp.mean(o, axis=-1, keepdims=True)
        var = jnp.mean(jnp.square(o - mu), axis=-1, keepdims=True)
        on = ((o - mu) * lax.rsqrt(var + NORM_EPS)) * gain_ref[:, cols]
        g = g_ref[:, cols]
        y_ref[:, cols] = ((g * jax.nn.sigmoid(g)) * on).astype(y_ref.dtype)

    @pl.when(t == pl.num_programs(1) - 1)
    def _():
        sfin_ref[0] = s_scr[...]


def _retention(ret_in, s0, gn_gain, batch, seq, blk):
    nt = seq // blk
    dmat, w_q, w_end, g_block = _retention_tables(blk)

    def col(c):
        return pl.BlockSpec((blk, RET_WIDTH), lambda b, t: (b * nt + t, c))

    const3 = lambda b, t: (0, 0, 0)
    return pl.pallas_call(
        _retention_kernel,
        grid=(batch, nt),
        in_specs=[
            pl.BlockSpec(memory_space=pltpu.SMEM),
            col(0), col(1), col(2), col(3),
            pl.BlockSpec((1, RET_HEADS, RET_DK, RET_DV), lambda b, t: (b, 0, 0, 0)),
            pl.BlockSpec((RET_HEADS, blk, blk), const3),
            pl.BlockSpec((RET_HEADS, blk, LANES), const3),
            pl.BlockSpec((RET_HEADS, blk, LANES), const3),
            pl.BlockSpec((1, RET_WIDTH), lambda b, t: (0, 0)),
        ],
        out_specs=[
            pl.BlockSpec((blk, RET_WIDTH), lambda b, t: (b * nt + t, 0)),
            pl.BlockSpec((1, RET_HEADS, RET_DK, RET_DV), lambda b, t: (b, 0, 0, 0)),
        ],
        out_shape=(
            jax.ShapeDtypeStruct((batch * seq, RET_WIDTH), BF16),
            jax.ShapeDtypeStruct((batch, RET_HEADS, RET_DK, RET_DV), F32),
        ),
        scratch_shapes=[pltpu.VMEM((RET_HEADS, RET_DK, RET_DV), F32)],
        compiler_params=_compiler_params(("parallel", "arbitrary")),
        name="retention",
    )(g_block, ret_in, ret_in, ret_in, ret_in, s0, dmat, w_q, w_end, gn_gain)


COUNT_ROWS = 32
COUNT_UNROLL = 8
MASKED = 2.0 * NEG
LOG2_E = 1.4426950408889634


def _ordered_bits_to_float(u):
    bits = jnp.where(u < 0, u ^ jnp.int32(-2 ** 31), ~u)
    return lax.bitcast_convert_type(bits, F32)


def _signed_half(x):
    return (x - 2 ** 15).astype(jnp.int16)


def _attention_kernel(lim_ref, qa_ref, qi_ref, wt_ref, ki_ref, ka_ref, vt_ref, o_ref,
                      s_scr, hi16_scr, lo16_scr, xa_scr, xb_scr, acc_scr,
                      *, qb, tk, n_keys, topk, causal, n_valid_q, index_bits):
    n_kt = pl.program_id(1) * (qb // tk) + (qb // tk) if causal else ki_ref.shape[0] // tk
    lim = lim_ref[0]
    lane = lax.broadcasted_iota(jnp.int32, (qb, LANES), 1)
    key_iota = lax.broadcasted_iota(jnp.int32, (tk, qb), 0)

    def head_operand(ref, hd, scale, shift_to_low):
        pair = ref[:, (hd // 2) * LANES:(hd // 2 + 1) * LANES] * scale
        if shift_to_low and hd % 2:
            pair = pltpu.roll(pair, 64, 1)
        keep = (lane < 64) if (shift_to_low or hd % 2 == 0) else (lane >= 64)
        return jnp.where(keep, pair, 0.0).T.astype(BF16)

    qi_heads = [head_operand(qi_ref, hd, 1.0, True) for hd in range(IDX_HEADS)]
    wt = wt_ref[0]

    def score_body(kt, carry, some_inadmissible):
        r0 = pl.multiple_of(kt * tk, tk)
        keys = ki_ref[pl.ds(r0, tk), :]
        acc = jnp.zeros((tk, qb), F32)
        for hd in range(IDX_HEADS):
            d = jnp.dot(keys, qi_heads[hd], preferred_element_type=F32)
            acc = acc + jnp.maximum(d, 0.0) * wt[hd:hd + 1, :]
        s = jnp.where(acc == 0.0, 0.0, acc)
        if some_inadmissible:
            s = jnp.where(key_iota + r0 < lim, s, -jnp.inf)
        s_scr[pl.ds(r0, tk), :] = s
        bits = lax.bitcast_convert_type(s, jnp.int32)
        ordered = bits ^ (lax.shift_right_arithmetic(bits, 31) | jnp.int32(-2 ** 31))
        hi16_scr[pl.ds(r0, tk), :] = _signed_half(lax.shift_right_logical(ordered, 16))
        lo16_scr[pl.ds(r0, tk), :] = _signed_half(ordered & 0xFFFF)
        return carry

    n_all_admissible = jnp.minimum(jnp.min(lim) // tk, n_kt)
    lax.fori_loop(0, n_all_admissible, functools.partial(score_body, some_inadmissible=False), 0)
    lax.fori_loop(n_all_admissible, n_kt, functools.partial(score_body, some_inadmissible=True), 0)

    def count(src_ref, indicator, dtype):
        def one_tile(kt, acc):
            for part in range(tk // COUNT_ROWS):
                r0 = pl.multiple_of(kt * tk + part * COUNT_ROWS, COUNT_ROWS)
                acc = acc + indicator(src_ref[pl.ds(r0, COUNT_ROWS), :], r0)
            return acc
        def tile_group(g, acc):
            for j in range(COUNT_UNROLL):
                acc = one_tile(COUNT_UNROLL * g + j, acc)
            return acc

        acc = lax.fori_loop(0, n_kt // COUNT_UNROLL, tile_group, jnp.zeros((COUNT_ROWS, qb), dtype))
        acc = lax.fori_loop(n_kt - n_kt % COUNT_UNROLL, n_kt, one_tile, acc)
        return jnp.sum(acc.astype(F32), axis=0, keepdims=True)

    n_masked = (n_keys - lim).astype(F32)
    kf = float(topk)
    q_lane = lax.broadcasted_iota(jnp.int32, (1, qb), 1)
    one16 = jnp.ones((), BF16)
    zero16 = jnp.zeros((), BF16)

    def bisect_step(it, carry, half_ref):
        cur, cnt = carry
        cand = cur | lax.shift_left(jnp.int32(1), 31 - it)
        if half_ref is hi16_scr:
            half = _signed_half(lax.shift_right_logical(cand, 16))
        else:
            half = _signed_half(cand & 0xFFFF)
        c = count(half_ref, lambda h, r0: jnp.where(h >= half, one16, zero16), BF16)
        ok = c + jnp.where(_ordered_bits_to_float(cand) <= NEG, n_masked, 0.0) >= kf
        return jnp.where(ok, cand, cur), jnp.where(ok, c, cnt)

    carry = (jnp.zeros((1, qb), jnp.int32), jnp.zeros((1, qb), F32))
    carry = lax.fori_loop(0, 16, lambda it, st: bisect_step(it, st, hi16_scr), carry)

    top = _signed_half(lax.shift_right_logical(carry[0], 16))

    def narrow_body(kt, c):
        r0 = pl.multiple_of(kt * tk, tk)
        hi = hi16_scr[pl.ds(r0, tk), :]
        lo = lo16_scr[pl.ds(r0, tk), :]
        lo16_scr[pl.ds(r0, tk), :] = jnp.where(hi > top, jnp.int16(2 ** 15 - 1),
                                               jnp.where(hi == top, lo, jnp.int16(-2 ** 15)))
        return c

    lax.fori_loop(0, n_kt, narrow_body, 0)
    cur, cnt_ge = lax.fori_loop(16, 32, lambda it, st: bisect_step(it, st, lo16_scr), carry)
    theta = _ordered_bits_to_float(cur)

    overflow = jnp.where((cnt_ge > kf) & (q_lane < n_valid_q), 1.0, 0.0)
    all_idx = jnp.full((1, qb), 2 ** index_bits - 1, jnp.int32)

    def tie_cut():
        need = kf - count(s_scr, lambda s, r0: jnp.where(s > theta, 1.0, 0.0), F32)

        def mark_body(kt, c):
            r0 = pl.multiple_of(kt * tk, tk)
            tied = jnp.where(s_scr[pl.ds(r0, tk), :] == theta, key_iota + r0, 2 ** 15 - 1)
            hi16_scr[pl.ds(r0, tk), :] = tied.astype(jnp.int16)
            return c

        lax.fori_loop(0, n_kt, mark_body, 0)

        def body(it, cut):
            cand = cut | lax.shift_left(jnp.int32(1), index_bits - 1 - it)
            cand16 = cand.astype(jnp.int16)
            c = count(hi16_scr, lambda idx, r0: jnp.where(idx < cand16, one16, zero16), BF16)
            return jnp.where(c <= need, cand, cut)

        return lax.fori_loop(0, index_bits, body, jnp.zeros((1, qb), jnp.int32))

    any_overflow = jnp.max(overflow) > 0.0
    idx_cut = lax.cond(any_overflow, tie_cut, lambda: all_idx)

    def bias_body(kt, carry, with_ties):
        r0 = pl.multiple_of(kt * tk, tk)
        s = s_scr[pl.ds(r0, tk), :]
        if with_ties:
            tie = jnp.where(key_iota + r0 < idx_cut, 0.0, MASKED)
            s_scr[pl.ds(r0, tk), :] = jnp.where(s > theta, 0.0, jnp.where(s == theta, tie, MASKED))
        else:
            s_scr[pl.ds(r0, tk), :] = jnp.where(s >= theta, 0.0, MASKED)
        return carry

    lax.cond(any_overflow,
             lambda: lax.fori_loop(0, n_kt, functools.partial(bias_body, with_ties=True), 0),
             lambda: lax.fori_loop(0, n_kt, functools.partial(bias_body, with_ties=False), 0))

    lp = ki_ref.shape[0]
    s_scr[pl.ds(lp, tk), :] = jnp.full((tk, qb), MASKED, F32)
    qa_heads = [head_operand(qa_ref, hd, ATT_DH ** -0.5 * LOG2_E, False) for hd in range(ATT_HEADS)]
    acc_scr[...] = jnp.zeros(acc_scr.shape, F32)

    def logits_stage(kt, x_ref):
        kt_in = jnp.minimum(kt, n_kt - 1)
        r0 = pl.multiple_of(kt_in * tk, tk)
        b0 = pl.multiple_of(jnp.where(kt < n_kt, kt_in * tk, lp), tk)
        tops = []
        for hd in range(ATT_HEADS):
            keys = ka_ref[pl.ds(r0, tk), (hd // 2) * LANES:(hd // 2 + 1) * LANES]
            x = jnp.dot(keys, qa_heads[hd], preferred_element_type=F32) + s_scr[pl.ds(b0, tk), :]
            x_ref[hd] = x.astype(x_ref.dtype)
            tops.append(jnp.max(x, axis=0, keepdims=True))
        return jnp.concatenate(tops, axis=0).astype(x_ref.dtype).astype(F32)

    def accumulate(kt, x_ref, tile_max, m_run, l_run):
        kt_in = jnp.minimum(kt, n_kt - 1)
        m_new = jnp.maximum(m_run, tile_max)
        alpha = jnp.exp2(m_run - m_new)
        m_staged = m_new.astype(x_ref.dtype)
        sums = []
        for hd in range(ATT_HEADS):
            p = jnp.exp2(x_ref[hd] - m_staged[hd:hd + 1, :]).astype(BF16)
            sums.append(jnp.sum(p.astype(F32), axis=0, keepdims=True))
            rows = slice(hd * ATT_DH, (hd + 1) * ATT_DH)
            pv = jnp.dot(vt_ref[kt_in, rows, :], p, preferred_element_type=F32)
            acc_scr[rows, :] = alpha[hd:hd + 1, :] * acc_scr[rows, :] + pv
        return m_new, alpha * l_run + jnp.concatenate(sums, axis=0)

    def pair_body(pair, carry, stage_next=True):
        max_a, m_run, l_run = carry
        max_b = logits_stage(2 * pair + 1, xb_scr)
        m_run, l_run = accumulate(2 * pair, xa_scr, max_a, m_run, l_run)
        if stage_next:
            max_a = logits_stage(2 * pair + 2, xa_scr)
        m_run, l_run = accumulate(2 * pair + 1, xb_scr, max_b, m_run, l_run)
        return max_a, m_run, l_run

    n_pairs = (n_kt + 1) // 2
    first_max = logits_stage(0, xa_scr)
    carry = lax.fori_loop(
        0, n_pairs - 1, pair_body,
        (first_max, jnp.full((ATT_HEADS, qb), NEG, F32), jnp.zeros((ATT_HEADS, qb), F32)))
    _, _, l_fin = pair_body(n_pairs - 1, carry, stage_next=False)

    for hd in range(ATT_HEADS):
        rows = slice(hd * ATT_DH, (hd + 1) * ATT_DH)
        acc_scr[rows, :] = acc_scr[rows, :] / l_fin[hd:hd + 1, :]
    o_ref[...] = acc_scr[...].T.astype(o_ref.dtype)


def _attention(q_a, q_i, w_t, lim, k_i, k_a, v_t, *, qb, tk, n_keys, topk, causal, n_valid_q):
    batch, tq, _ = q_a.shape
    lp = k_i.shape[1]
    nq = tq // qb
    index_bits = int(np.ceil(np.log2(lp + 1)))
    assert lp // COUNT_ROWS <= 256, "bfloat16 partial counts are exact only up to 256"
    kernel = functools.partial(_attention_kernel, qb=qb, tk=tk, n_keys=n_keys, topk=topk, causal=causal,
                               n_valid_q=n_valid_q, index_bits=index_bits)
    return pl.pallas_call(
        kernel,
        grid=(batch, nq),
        in_specs=[
            pl.BlockSpec((1, 1, qb), lambda b, i: (i, 0, 0)),
            pl.BlockSpec((None, qb, ATT_WIDTH), lambda b, i: (b, i, 0)),
            pl.BlockSpec((None, qb, IDX_HEADS * IDX_DH), lambda b, i: (b, i, 0)),
            pl.BlockSpec((1, IDX_HEADS, qb), lambda b, i: (b, 0, i)),
            _resident((None, lp, LANES), lambda b, i: (b, 0, 0)),
            _resident((None, lp, ATT_WIDTH), lambda b, i: (b, 0, 0)),
            _resident((None, lp // tk, ATT_WIDTH, tk), lambda b, i: (b, 0, 0, 0)),
        ],
        out_specs=pl.BlockSpec((None, qb, ATT_WIDTH), lambda b, i: (b, i, 0)),
        out_shape=jax.ShapeDtypeStruct((batch, tq, ATT_WIDTH), BF16),
        scratch_shapes=[
            pltpu.VMEM((lp + tk, qb), F32),
            pltpu.VMEM((lp, qb), jnp.int16),
            pltpu.VMEM((lp, qb), jnp.int16),
            pltpu.VMEM((ATT_HEADS, tk, qb), BF16),
            pltpu.VMEM((ATT_HEADS, tk, qb), BF16),
            pltpu.VMEM((ATT_WIDTH, qb), F32),
        ],
        compiler_params=_compiler_params(("parallel", "arbitrary")),
        name="attention",
    )(lim, q_a, q_i, w_t, k_i, k_a, v_t)


FF_CHUNK = 512


def _mlp_kernel(x_ref, yr_ref, ya_ref, wo_ref, g2_ref, wup_ref, wdn_ref, gf_ref, y_ref, x_scr, h_scr, a_scr):
    x = x_ref[...] + jnp.dot(yr_ref[...], wo_ref[:RET_WIDTH, :], preferred_element_type=F32)
    x = x + jnp.dot(ya_ref[...], wo_ref[RET_WIDTH:, :], preferred_element_type=F32)
    x_scr[...] = x
    ms = jnp.mean(x * x, axis=-1, keepdims=True)
    h_scr[...] = ((x * lax.rsqrt(ms + NORM_EPS)) * g2_ref[...]).astype(h_scr.dtype)
    for c in range(D_FF // FF_CHUNK):
        cols = slice(c * FF_CHUNK, (c + 1) * FF_CHUNK)
        u = jnp.dot(h_scr[...], wup_ref[:, cols], preferred_element_type=F32)
        a_scr[:, cols] = jnp.square(jnp.maximum(u, 0.0)).astype(a_scr.dtype)
    x = x_scr[...] + jnp.dot(a_scr[...], wdn_ref[...], preferred_element_type=F32)
    ms = jnp.mean(x * x, axis=-1, keepdims=True)
    y_ref[...] = (x * lax.rsqrt(ms + NORM_EPS)) * gf_ref[...]


def _mlp(x2d, y_r, y_a, w_out_b, norm2, w_up_b, w_down_b, norm_final, tm):
    m = x2d.shape[0]
    row = lambda i: (i, 0)
    const = lambda i: (0, 0)
    return pl.pallas_call(
        _mlp_kernel,
        grid=(m // tm,),
        in_specs=[
            pl.BlockSpec((tm, D_MODEL), row),
            pl.BlockSpec((tm, RET_WIDTH), row),
            pl.BlockSpec((tm, ATT_WIDTH), row),
            _resident((D_MODEL, D_MODEL), const),
            _resident((1, D_MODEL), const),
            _resident((D_MODEL, D_FF), const),
            _resident((D_FF, D_MODEL), const),
            _resident((1, D_MODEL), const),
        ],
        out_specs=pl.BlockSpec((tm, D_MODEL), row),
        out_shape=jax.ShapeDtypeStruct((m, D_MODEL), F32),
        scratch_shapes=[pltpu.VMEM((tm, D_MODEL), F32), pltpu.VMEM((tm, D_MODEL), BF16),
                        pltpu.VMEM((tm, D_FF), BF16)],
        compiler_params=_compiler_params(("parallel",)),
        name="mlp",
    )(x2d, y_r, y_a, w_out_b, norm2, w_up_b, w_down_b, norm_final)


def _pick_tile(n, target):
    t = min(n, target)
    while n % t:
        t //= 2
    return t


def _layer(x, pos, past, s0, ret_blk, weights, *, attn_qb, attn_tk):
    norm1, w_in_b, gn_gain, w_out_b, norm2, w_up_b, w_down_b, norm_final = weights
    batch, seq, _ = x.shape
    m = batch * seq
    x2d = x.reshape(m, D_MODEL)
    tm = _pick_tile(m, ROW_TILE)

    tabs = _rope_tables(pos)
    if seq < tm:
        tabs = jnp.tile(tabs, (1, tm // seq, 1))
    ret_in, q_a, q_i, kw, k_a, v_a, ka16, vt16, kw16 = _proj(x2d, norm1, w_in_b, tabs, tm)

    y_r, s_new = _retention(ret_in, s0, gn_gain, batch, seq, ret_blk)

    three = lambda a: a.reshape(batch, seq, -1)
    k_i = three(kw[:, :IDX_DH])
    w_t = jnp.swapaxes(three(kw[:, IDX_DH:IDX_DH + IDX_HEADS]) * IDX_HEADS ** -0.5 * IDX_DH ** -0.5, 1, 2)
    q_a3, q_i3 = three(q_a), three(q_i)
    n_keys = seq if past is None else past[0].shape[1] + seq
    topk = min(TOPK_MAX, n_keys // 4)
    tq = -(-seq // attn_qb) * attn_qb
    lp = -(-n_keys // attn_tk) * attn_tk
    padq = lambda a: jnp.pad(a, ((0, 0), (0, tq - seq), (0, 0)))
    padk = lambda a: jnp.pad(a, ((0, 0), (0, lp - n_keys), (0, 0)))
    if past is None and attn_tk == vt16.shape[2] and lp == n_keys:
        keys_i, keys_a = three(kw16), three(ka16)
        v_t = vt16.reshape(batch, lp // attn_tk, ATT_WIDTH, attn_tk)
        n_valid_q = attn_qb
    else:
        keys_i, keys_a, vals = three(kw16), three(ka16), three(v_a).astype(BF16)
        n_valid_q = attn_qb
        if past is not None:
            past_k, past_v, past_ki = past
            pad_i = jnp.pad(past_ki, ((0, 0), (0, 0), (0, LANES - IDX_DH))).astype(BF16)
            keys_i = jnp.concatenate([pad_i, keys_i], axis=1)
            keys_a = jnp.concatenate([past_k.reshape(batch, -1, ATT_WIDTH).astype(BF16), keys_a], axis=1)
            vals = jnp.concatenate([past_v.reshape(batch, -1, ATT_WIDTH).astype(BF16), vals], axis=1)
            n_valid_q = seq
        keys_i, keys_a = padk(keys_i), padk(keys_a)
        v_t = jnp.swapaxes(padk(vals).reshape(batch, lp // attn_tk, attn_tk, ATT_WIDTH), 2, 3)
    lim = jnp.minimum((pos // CHUNK + 1) * CHUNK, n_keys).astype(jnp.int32)
    lim = jnp.pad(lim, (0, tq - seq), constant_values=n_keys).reshape(tq // attn_qb, 1, attn_qb)
    y_a = _attention(
        padq(q_a3), padq(q_i3), jnp.pad(w_t, ((0, 0), (0, 0), (0, tq - seq))), lim, keys_i, keys_a, v_t,
        qb=attn_qb, tk=attn_tk, n_keys=n_keys, topk=topk, causal=past is None, n_valid_q=n_valid_q)
    y_a = y_a[:, :seq].reshape(m, ATT_WIDTH)

    y = _mlp(x2d, y_r, y_a, w_out_b, norm2, w_up_b, w_down_b, norm_final, tm)
    return (y.reshape(batch, seq, D_MODEL),
            k_a.reshape(batch, seq, ATT_HEADS, ATT_DH),
            v_a.reshape(batch, seq, ATT_HEADS, ATT_DH),
            k_i, s_new)


def kernel(x_prompt, x_sample, cache_k_att, cache_v_att, cache_k_idx, state_ret, norm1, w_in, gn_gain, w_out,
           norm2, w_up, w_down, norm_final):
    depth = norm1.shape[0]
    assert depth == 1, "the final norm is fused into the layer's last kernel"
    past_len = cache_k_att.shape[2]
    pos_p = jnp.arange(x_prompt.shape[1], dtype=jnp.int32)
    pos_s = past_len + jnp.arange(x_sample.shape[1], dtype=jnp.int32)
    l = 0
    w_in_b = jnp.pad(w_in[l], ((0, 0), (0, IN_WIDTH_PADDED - IN_WIDTH))).astype(BF16)
    weights = (norm1[l][None], w_in_b, gn_gain[l][None], w_out[l].astype(BF16), norm2[l][None],
               w_up[l].astype(BF16), w_down[l].astype(BF16), norm_final[None])
    s0_p = jnp.zeros((x_prompt.shape[0], RET_HEADS, RET_DK, RET_DV), F32)
    yp, kap, vap, kip, sp = _layer(x_prompt, pos_p, None, s0_p, PROMPT_RETENTION_BLOCK, weights,
                                   attn_qb=PROMPT_QUERY_BLOCK, attn_tk=KEY_TILE)
    past = (cache_k_att[l], cache_v_att[l], cache_k_idx[l])
    ys, kas, vas, kis, ss = _layer(x_sample, pos_s, past, state_ret[l].astype(F32), x_sample.shape[1], weights,
                                   attn_qb=SAMPLE_QUERY_BLOCK, attn_tk=KEY_TILE)
    return (yp, ys, kap[None], vap[None], kip[None], sp[None], kas[None], vas[None], kis[None], ss[None])
```

```python
import functools

import jax
import jax.numpy as jnp
import numpy as np
from jax import lax
from jax.experimental import pallas as pl
from jax.experimental.pallas import tpu as pltpu

D_MODEL = 1024
CHUNK = 64
ROPE_THETA = 10000.0
NORM_EPS = 1e-6
NEG = -1e30
RET_WIDTH = 512
ATT_WIDTH = 512
RET_HEADS = 4
RET_DK = 128
RET_DV = 128
ATT_HEADS = 8
ATT_DH = 64
IDX_HEADS = 8
IDX_DH = 64
TOPK_MAX = 256
D_FF = 4 * D_MODEL
IN_WIDTH = 4 * RET_WIDTH + 3 * ATT_WIDTH + IDX_HEADS * IDX_DH + IDX_DH + IDX_HEADS

LANES = 128
IN_WIDTH_PADDED = ((IN_WIDTH + LANES - 1) // LANES) * LANES
VMEM_LIMIT_BYTES = 56 * 1024 * 1024

ROW_TILE = 512
KEY_TILE = 256
PROMPT_QUERY_BLOCK = 256
SAMPLE_QUERY_BLOCK = 128
PROMPT_RETENTION_BLOCK = 256

F32 = jnp.float32
BF16 = jnp.bfloat16

_NT = (((1,), (1,)), ((), ()))


def _compiler_params(semantics):
    return pltpu.CompilerParams(dimension_semantics=semantics, vmem_limit_bytes=VMEM_LIMIT_BYTES)


def _resident(block_shape, index_map):
    return pl.BlockSpec(block_shape, index_map, pipeline_mode=pl.Buffered(1))


def _rope_tables(pos):
    posf = pos.astype(F32)[:, None]

    def angles(d):
        half = d // 2
        inv = ROPE_THETA ** (-jnp.arange(half, dtype=F32) * 2.0 / d)
        ang = posf * inv[None, :]
        return jnp.cos(ang), jnp.sin(ang)

    c128, s128 = angles(128)
    zero64 = jnp.zeros_like(s128)
    cos128 = jnp.concatenate([c128, c128], axis=1)
    sin128 = jnp.concatenate([-s128, s128], axis=1)
    c64, s64 = angles(64)
    zero32 = jnp.zeros_like(s64)
    cos64 = jnp.concatenate([c64, c64, c64, c64], axis=1)
    sin64_up = jnp.concatenate([zero32, s64, zero32, s64], axis=1)
    sin64_dn = jnp.concatenate([-s64, zero32, -s64, zero32], axis=1)
    one64 = jnp.ones_like(zero64)
    cosk = jnp.concatenate([c64, c64, one64], axis=1)
    sink_up = jnp.concatenate([zero32, s64, zero64], axis=1)
    sink_dn = jnp.concatenate([-s64, zero32, zero64], axis=1)
    return jnp.stack([cos128, sin128, cos64, sin64_up, sin64_dn, cosk, sink_up, sink_dn])


V_TILE = KEY_TILE


def _proj_kernel(x_ref, g_ref, w_ref, tab_ref, ret_ref, qa_ref, qi_ref, kw_ref, ka_ref, va_ref,
                 ka16_ref, vt16_ref, kw16_ref):
    x = x_ref[...]
    ms = jnp.mean(x * x, axis=-1, keepdims=True)
    h = ((x * lax.rsqrt(ms + NORM_EPS)) * g_ref[...]).astype(BF16)

    def mm(c0, n):
        return jnp.dot(h, w_ref[:, c0:c0 + n], preferred_element_type=F32)

    def rope128(z):
        return z * tab_ref[0] + pltpu.roll(z, 64, 1) * tab_ref[1]

    def rope64(z):
        return z * tab_ref[2] + pltpu.roll(z, 32, 1) * tab_ref[3] + pltpu.roll(z, 96, 1) * tab_ref[4]

    def ropek(z):
        return z * tab_ref[5] + pltpu.roll(z, 32, 1) * tab_ref[6] + pltpu.roll(z, 96, 1) * tab_ref[7]

    def emit(out_ref, out_c0, w_c0, rope):
        z = mm(w_c0, 512)
        for t in range(4):
            zt = z[:, t * LANES:(t + 1) * LANES]
            if rope is not None:
                zt = rope(zt)
            out_ref[:, out_c0 + t * LANES:out_c0 + (t + 1) * LANES] = zt.astype(out_ref.dtype)

    emit(ret_ref, 0, 0, rope128)
    emit(ret_ref, 512, 512, rope128)
    emit(ret_ref, 1024, 1024, None)
    emit(ret_ref, 1536, 1536, None)
    emit(qa_ref, 0, 2048, rope64)
    emit(qi_ref, 0, 3584, rope64)
    k_a = mm(2560, 512)
    k_a = jnp.concatenate([rope64(k_a[:, t * LANES:(t + 1) * LANES]) for t in range(4)], axis=1)
    ka_ref[...] = k_a
    ka16_ref[...] = k_a.astype(ka16_ref.dtype)
    v_a = mm(3072, 512)
    va_ref[...] = v_a
    v_tile = vt16_ref.shape[2]
    for part in range(vt16_ref.shape[0]):
        vt16_ref[part] = v_a[part * v_tile:(part + 1) * v_tile, :].T.astype(vt16_ref.dtype)
    kw = ropek(mm(4096, LANES))
    kw_ref[...] = kw
    kw16_ref[...] = kw.astype(kw16_ref.dtype)


def _proj(x2d, norm_g, w_in_b, tabs, tm):
    m = x2d.shape[0]
    n_tab = tabs.shape[1] // tm
    row = lambda i: (i, 0)
    row3 = lambda i: (i, 0, 0)
    assert tm % V_TILE == 0 or m == tm, (tm, m)
    n_vt = max(tm // V_TILE, 1)
    v_tile = min(V_TILE, tm)
    outs = (
        jax.ShapeDtypeStruct((m, 4 * RET_WIDTH), F32),
        jax.ShapeDtypeStruct((m, ATT_WIDTH), F32),
        jax.ShapeDtypeStruct((m, IDX_HEADS * IDX_DH), F32),
        jax.ShapeDtypeStruct((m, LANES), F32),
        jax.ShapeDtypeStruct((m, ATT_WIDTH), F32),
        jax.ShapeDtypeStruct((m, ATT_WIDTH), F32),
        jax.ShapeDtypeStruct((m, ATT_WIDTH), BF16),
        jax.ShapeDtypeStruct((m // v_tile, ATT_WIDTH, v_tile), BF16),
        jax.ShapeDtypeStruct((m, LANES), BF16),
    )
    return pl.pallas_call(
        _proj_kernel,
        grid=(m // tm,),
        in_specs=[
            pl.BlockSpec((tm, D_MODEL), row),
            _resident((1, D_MODEL), lambda i: (0, 0)),
            _resident((D_MODEL, IN_WIDTH_PADDED), lambda i: (0, 0)),
            pl.BlockSpec((8, tm, LANES), lambda i: (0, i % n_tab, 0)),
        ],
        out_specs=[
            pl.BlockSpec((tm, 4 * RET_WIDTH), row),
            pl.BlockSpec((tm, ATT_WIDTH), row),
            pl.BlockSpec((tm, IDX_HEADS * IDX_DH), row),
            pl.BlockSpec((tm, LANES), row),
            pl.BlockSpec((tm, ATT_WIDTH), row),
            pl.BlockSpec((tm, ATT_WIDTH), row),
            pl.BlockSpec((tm, ATT_WIDTH), row),
            pl.BlockSpec((n_vt, ATT_WIDTH, v_tile), row3),
            pl.BlockSpec((tm, LANES), row),
        ],
        out_shape=outs,
        compiler_params=_compiler_params(("parallel",)),
        name="proj",
    )(x2d, norm_g, w_in_b, tabs)


def _retention_tables(blk):
    lg = jnp.log1p(-jnp.exp2(-5.0 - jnp.arange(RET_HEADS, dtype=F32)))
    i = jnp.arange(blk, dtype=F32)
    diff = i[:, None] - i[None, :]
    dmat = jnp.where(diff[None] >= 0, jnp.exp(lg[:, None, None] * jnp.maximum(diff, 0.0)[None]), 0.0)
    w_end = jnp.exp(lg[:, None] * (blk - 1.0 - i)[None, :])
    w_q = jnp.exp(lg[:, None] * (i + 1.0)[None, :])
    g_block = jnp.exp(lg * blk)
    lane = lambda a: jnp.broadcast_to(a[:, :, None], (RET_HEADS, blk, LANES))
    return dmat, lane(w_q), lane(w_end), g_block


def _retention_kernel(gblk_ref, q_ref, k_ref, v_ref, g_ref, s0_ref, dmat_ref, wq_ref, wend_ref, gain_ref,
                      y_ref, sfin_ref, s_scr):
    t = pl.program_id(1)

    @pl.when(t == 0)
    def _():
        s_scr[...] = s0_ref[0]

    for hd in range(RET_HEADS):
        cols = slice(hd * RET_DK, (hd + 1) * RET_DK)
        q = q_ref[:, cols]
        k = k_ref[:, cols] * RET_DK ** -0.5
        vb = v_ref[:, cols].astype(BF16)
        state = s_scr[hd]
        scores = lax.dot_general(q.astype(BF16), k.astype(BF16), _NT, preferred_element_type=F32) * dmat_ref[hd]
        o = jnp.dot(scores.astype(BF16), vb, preferred_element_type=F32)
        o = o + jnp.dot((q * wq_ref[hd]).astype(BF16), state.astype(BF16), preferred_element_type=F32)
        k_end_t = (k * wend_ref[hd]).T.astype(BF16)
        s_scr[hd] = gblk_ref[hd] * state + jnp.dot(k_end_t, vb, preferred_element_type=F32)
        mu = jnp.mean(o, axis=-1, keepdims=True)
        var = jnp.mean(jnp.square(o - mu), axis=-1, keepdims=True)
        on = ((o - mu) * lax.rsqrt(var + NORM_EPS)) * gain_ref[:, cols]
        g = g_ref[:, cols]
        y_ref[:, cols] = ((g * jax.nn.sigmoid(g)) * on).astype(y_ref.dtype)

    @pl.when(t == pl.num_programs(1) - 1)
    def _():
        sfin_ref[0] = s_scr[...]


def _retention(ret_in, s0, gn_gain, batch, seq, blk):
    nt = seq // blk
    dmat, w_q, w_end, g_block = _retention_tables(blk)

    def col(c):
        return pl.BlockSpec((blk, RET_WIDTH), lambda b, t: (b * nt + t, c))

    const3 = lambda b, t: (0, 0, 0)
    return pl.pallas_call(
        _retention_kernel,
        grid=(batch, nt),
        in_specs=[
            pl.BlockSpec(memory_space=pltpu.SMEM),
            col(0), col(1), col(2), col(3),
            pl.BlockSpec((1, RET_HEADS, RET_DK, RET_DV), lambda b, t: (b, 0, 0, 0)),
            pl.BlockSpec((RET_HEADS, blk, blk), const3),
            pl.BlockSpec((RET_HEADS, blk, LANES), const3),
            pl.BlockSpec((RET_HEADS, blk, LANES), const3),
            pl.BlockSpec((1, RET_WIDTH), lambda b, t: (0, 0)),
        ],
        out_specs=[
            pl.BlockSpec((blk, RET_WIDTH), lambda b, t: (b * nt + t, 0)),
            pl.BlockSpec((1, RET_HEADS, RET_DK, RET_DV), lambda b, t: (b, 0, 0, 0)),
        ],
        out_shape=(
            jax.ShapeDtypeStruct((batch * seq, RET_WIDTH), BF16),
            jax.ShapeDtypeStruct((batch, RET_HEADS, RET_DK, RET_DV), F32),
        ),
        scratch_shapes=[pltpu.VMEM((RET_HEADS, RET_DK, RET_DV), F32)],
        compiler_params=_compiler_params(("parallel", "arbitrary")),
        name="retention",
    )(g_block, ret_in, ret_in, ret_in, ret_in, s0, dmat, w_q, w_end, gn_gain)


COUNT_ROWS = 32
COUNT_UNROLL = 8
TILE_UNROLL = 4
MASKED = 2.0 * NEG
LOG2_E = 1.4426950408889634


def _ordered_bits_to_float(u):
    bits = jnp.where(u < 0, u ^ jnp.int32(-2 ** 31), ~u)
    return lax.bitcast_convert_type(bits, F32)


def _signed_half(x):
    return (x - 2 ** 15).astype(jnp.int16)


def _attention_kernel(lim_ref, qa_ref, qi_ref, wt_ref, ki_ref, ka_ref, vt_ref, o_ref,
                      s_scr, hi16_scr, lo16_scr, xa_scr, xb_scr, acc_scr,
                      *, qb, tk, n_keys, topk, causal, n_valid_q, index_bits):
    n_kt = pl.program_id(1) * (qb // tk) + (qb // tk) if causal else ki_ref.shape[0] // tk
    lim = lim_ref[0]
    lane = lax.broadcasted_iota(jnp.int32, (qb, LANES), 1)
    key_iota = lax.broadcasted_iota(jnp.int32, (tk, qb), 0)

    def head_operand(ref, hd, scale, shift_to_low):
        pair = ref[:, (hd // 2) * LANES:(hd // 2 + 1) * LANES] * scale
        if shift_to_low and hd % 2:
            pair = pltpu.roll(pair, 64, 1)
        keep = (lane < 64) if (shift_to_low or hd % 2 == 0) else (lane >= 64)
        return jnp.where(keep, pair, 0.0).T.astype(BF16)

    qi_heads = [head_operand(qi_ref, hd, 1.0, True) for hd in range(IDX_HEADS)]
    wt = wt_ref[0]

    def score_body(kt, carry, some_inadmissible):
        r0 = pl.multiple_of(kt * tk, tk)
        keys = ki_ref[pl.ds(r0, tk), :]
        acc = jnp.zeros((tk, qb), F32)
        for hd in range(IDX_HEADS):
            d = jnp.dot(keys, qi_heads[hd], preferred_element_type=F32)
            acc = acc + jnp.maximum(d, 0.0) * wt[hd:hd + 1, :]
        s = jnp.where(acc == 0.0, 0.0, acc)
        if some_inadmissible:
            s = jnp.where(key_iota + r0 < lim, s, -jnp.inf)
        s_scr[pl.ds(r0, tk), :] = s
        bits = lax.bitcast_convert_type(s, jnp.int32)
        ordered = bits ^ (lax.shift_right_arithmetic(bits, 31) | jnp.int32(-2 ** 31))
        hi16_scr[pl.ds(r0, tk), :] = _signed_half(lax.shift_right_logical(ordered, 16))
        lo16_scr[pl.ds(r0, tk), :] = _signed_half(ordered & 0xFFFF)
        return carry

    n_all_admissible = jnp.minimum(jnp.min(lim) // tk, n_kt)
    lax.fori_loop(0, n_all_admissible, functools.partial(score_body, some_inadmissible=False), 0)
    lax.fori_loop(n_all_admissible, n_kt, functools.partial(score_body, some_inadmissible=True), 0)

    def count(src_ref, indicator, dtype):
        def one_tile(kt, acc):
            for part in range(tk // COUNT_ROWS):
                r0 = pl.multiple_of(kt * tk + part * COUNT_ROWS, COUNT_ROWS)
                acc = acc + indicator(src_ref[pl.ds(r0, COUNT_ROWS), :], r0)
            return acc
        def tile_group(g, acc):
            for j in range(COUNT_UNROLL):
                acc = one_tile(COUNT_UNROLL * g + j, acc)
            return acc

        acc = lax.fori_loop(0, n_kt // COUNT_UNROLL, tile_group, jnp.zeros((COUNT_ROWS, qb), dtype))
        acc = lax.fori_loop(n_kt - n_kt % COUNT_UNROLL, n_kt, one_tile, acc)
        return jnp.sum(acc.astype(F32), axis=0, keepdims=True)

    n_masked = (n_keys - lim).astype(F32)
    kf = float(topk)
    q_lane = lax.broadcasted_iota(jnp.int32, (1, qb), 1)
    one16 = jnp.ones((), BF16)
    zero16 = jnp.zeros((), BF16)

    def bisect_step(it, carry, half_ref):
        cur, cnt = carry
        cand = cur | lax.shift_left(jnp.int32(1), 31 - it)
        if half_ref is hi16_scr:
            half = _signed_half(lax.shift_right_logical(cand, 16))
        else:
            half = _signed_half(cand & 0xFFFF)
        c = count(half_ref, lambda h, r0: jnp.where(h >= half, one16, zero16), BF16)
        ok = c + jnp.where(_ordered_bits_to_float(cand) <= NEG, n_masked, 0.0) >= kf
        return jnp.where(ok, cand, cur), jnp.where(ok, c, cnt)

    carry = (jnp.zeros((1, qb), jnp.int32), jnp.zeros((1, qb), F32))
    carry = lax.fori_loop(0, 16, lambda it, st: bisect_step(it, st, hi16_scr), carry)

    top = _signed_half(lax.shift_right_logical(carry[0], 16))

    def for_each_tile(body):
        def group(g, c):
            for j in range(TILE_UNROLL):
                body(TILE_UNROLL * g + j)
            return c

        def single(kt, c):
            body(kt)
            return c

        lax.fori_loop(0, n_kt // TILE_UNROLL, group, 0)
        lax.fori_loop(n_kt - n_kt % TILE_UNROLL, n_kt, single, 0)

    def narrow_body(kt):
        r0 = pl.multiple_of(kt * tk, tk)
        hi = hi16_scr[pl.ds(r0, tk), :]
        lo = lo16_scr[pl.ds(r0, tk), :]
        lo16_scr[pl.ds(r0, tk), :] = jnp.where(hi > top, jnp.int16(2 ** 15 - 1),
                                               jnp.where(hi == top, lo, jnp.int16(-2 ** 15)))

    for_each_tile(narrow_body)
    cur, cnt_ge = lax.fori_loop(16, 32, lambda it, st: bisect_step(it, st, lo16_scr), carry)
    theta = _ordered_bits_to_float(cur)

    overflow = jnp.where((cnt_ge > kf) & (q_lane < n_valid_q), 1.0, 0.0)
    all_idx = jnp.full((1, qb), 2 ** index_bits - 1, jnp.int32)

    def tie_cut():
        need = kf - count(s_scr, lambda s, r0: jnp.where(s > theta, 1.0, 0.0), F32)

        def mark_body(kt):
            r0 = pl.multiple_of(kt * tk, tk)
            tied = jnp.where(s_scr[pl.ds(r0, tk), :] == theta, key_iota + r0, 2 ** 15 - 1)
            hi16_scr[pl.ds(r0, tk), :] = tied.astype(jnp.int16)

        for_each_tile(mark_body)

        def body(it, cut):
            cand = cut | lax.shift_left(jnp.int32(1), index_bits - 1 - it)
            cand16 = cand.astype(jnp.int16)
            c = count(hi16_scr, lambda idx, r0: jnp.where(idx < cand16, one16, zero16), BF16)
            return jnp.where(c <= need, cand, cut)

        return lax.fori_loop(0, index_bits, body, jnp.zeros((1, qb), jnp.int32))

    any_overflow = jnp.max(overflow) > 0.0
    idx_cut = lax.cond(any_overflow, tie_cut, lambda: all_idx)

    def bias_body(kt, with_ties):
        r0 = pl.multiple_of(kt * tk, tk)
        s = s_scr[pl.ds(r0, tk), :]
        if with_ties:
            tie = jnp.where(key_iota + r0 < idx_cut, 0.0, MASKED)
            s_scr[pl.ds(r0, tk), :] = jnp.where(s > theta, 0.0, jnp.where(s == theta, tie, MASKED))
        else:
            s_scr[pl.ds(r0, tk), :] = jnp.where(s >= theta, 0.0, MASKED)

    lax.cond(any_overflow,
             lambda: for_each_tile(functools.partial(bias_body, with_ties=True)),
             lambda: for_each_tile(functools.partial(bias_body, with_ties=False)))

    qa_heads = [head_operand(qa_ref, hd, ATT_DH ** -0.5 * LOG2_E, False) for hd in range(ATT_HEADS)]
    acc_scr[...] = jnp.zeros(acc_scr.shape, F32)

    def logits_stage(kt, x_ref):
        r0 = pl.multiple_of(kt * tk, tk)
        tops = []
        for hd in range(ATT_HEADS):
            keys = ka_ref[pl.ds(r0, tk), (hd // 2) * LANES:(hd // 2 + 1) * LANES]
            x = jnp.dot(keys, qa_heads[hd], preferred_element_type=F32) + s_scr[pl.ds(r0, tk), :]
            x_ref[hd] = x.astype(x_ref.dtype)
            tops.append(jnp.max(x, axis=0, keepdims=True))
        return jnp.concatenate(tops, axis=0).astype(x_ref.dtype).astype(F32)

    def accumulate(kt_in, x_ref, tile_max, m_run, l_run):
        m_new = jnp.maximum(m_run, tile_max)
        alpha = jnp.exp2(m_run - m_new)
        m_staged = m_new.astype(x_ref.dtype)
        sums = []
        for hd in range(ATT_HEADS):
            p = jnp.exp2(x_ref[hd] - m_staged[hd:hd + 1, :]).astype(BF16)
            sums.append(jnp.sum(p.astype(F32), axis=0, keepdims=True))
            rows = slice(hd * ATT_DH, (hd + 1) * ATT_DH)
            pv = jnp.dot(vt_ref[kt_in, rows, :], p, preferred_element_type=F32)
            acc_scr[rows, :] = alpha[hd:hd + 1, :] * acc_scr[rows, :] + pv
        return m_new, alpha * l_run + jnp.concatenate(sums, axis=0)

    def pair_body(pair, carry, stage_next=True):
        max_a, m_run, l_run = carry
        max_b = logits_stage(2 * pair + 1, xb_scr)
        m_run, l_run = accumulate(2 * pair, xa_scr, max_a, m_run, l_run)
        if stage_next:
            max_a = logits_stage(2 * pair + 2, xa_scr)
        m_run, l_run = accumulate(2 * pair + 1, xb_scr, max_b, m_run, l_run)
        return max_a, m_run, l_run

    n_pairs = (n_kt + 1) // 2
    first_max = logits_stage(0, xa_scr)
    carry = lax.fori_loop(
        0, n_pairs - 1, pair_body,
        (first_max, jnp.full((ATT_HEADS, qb), NEG, F32), jnp.zeros((ATT_HEADS, qb), F32)))
    l_fin = lax.cond(
        n_kt % 2 == 0,
        lambda c: pair_body(n_pairs - 1, c, stage_next=False)[2],
        lambda c: accumulate(2 * (n_pairs - 1), xa_scr, *c)[1],
        carry)

    for hd in range(ATT_HEADS):
        rows = slice(hd * ATT_DH, (hd + 1) * ATT_DH)
        acc_scr[rows, :] = acc_scr[rows, :] / l_fin[hd:hd + 1, :]
    o_ref[...] = acc_scr[...].T.astype(o_ref.dtype)


def _attention(q_a, q_i, w_t, lim, k_i, k_a, v_t, *, qb, tk, n_keys, topk, causal, n_valid_q):
    batch, tq, _ = q_a.shape
    lp = k_i.shape[1]
    nq = tq // qb
    index_bits = int(np.ceil(np.log2(lp + 1)))
    assert lp // COUNT_ROWS <= 256, "bfloat16 partial counts are exact only up to 256"
    kernel = functools.partial(_attention_kernel, qb=qb, tk=tk, n_keys=n_keys, topk=topk, causal=causal,
                               n_valid_q=n_valid_q, index_bits=index_bits)
    return pl.pallas_call(
        kernel,
        grid=(batch, nq),
        in_specs=[
            pl.BlockSpec((1, 1, qb), lambda b, i: (i, 0, 0)),
            pl.BlockSpec((None, qb, ATT_WIDTH), lambda b, i: (b, i, 0)),
            pl.BlockSpec((None, qb, IDX_HEADS * IDX_DH), lambda b, i: (b, i, 0)),
            pl.BlockSpec((1, IDX_HEADS, qb), lambda b, i: (b, 0, i)),
            _resident((None, lp, LANES), lambda b, i: (b, 0, 0)),
            _resident((None, lp, ATT_WIDTH), lambda b, i: (b, 0, 0)),
            _resident((None, lp // tk, ATT_WIDTH, tk), lambda b, i: (b, 0, 0, 0)),
        ],
        out_specs=pl.BlockSpec((None, qb, ATT_WIDTH), lambda b, i: (b, i, 0)),
        out_shape=jax.ShapeDtypeStruct((batch, tq, ATT_WIDTH), BF16),
        scratch_shapes=[
            pltpu.VMEM((lp, qb), F32),
            pltpu.VMEM((lp, qb), jnp.int16),
            pltpu.VMEM((lp, qb), jnp.int16),
            pltpu.VMEM((ATT_HEADS, tk, qb), BF16),
            pltpu.VMEM((ATT_HEADS, tk, qb), BF16),
            pltpu.VMEM((ATT_WIDTH, qb), F32),
        ],
        compiler_params=_compiler_params(("parallel", "arbitrary")),
        name="attention",
    )(lim, q_a, q_i, w_t, k_i, k_a, v_t)


FF_CHUNK = 512


def _mlp_kernel(x_ref, yr_ref, ya_ref, wo_ref, g2_ref, wup_ref, wdn_ref, gf_ref, y_ref, x_scr, h_scr, a_scr):
    x = x_ref[...] + jnp.dot(yr_ref[...], wo_ref[:RET_WIDTH, :], preferred_element_type=F32)
    x = x + jnp.dot(ya_ref[...], wo_ref[RET_WIDTH:, :], preferred_element_type=F32)
    x_scr[...] = x
    ms = jnp.mean(x * x, axis=-1, keepdims=True)
    h_scr[...] = ((x * lax.rsqrt(ms + NORM_EPS)) * g2_ref[...]).astype(h_scr.dtype)
    for c in range(D_FF // FF_CHUNK):
        cols = slice(c * FF_CHUNK, (c + 1) * FF_CHUNK)
        u = jnp.dot(h_scr[...], wup_ref[:, cols], preferred_element_type=F32)
        a_scr[:, cols] = jnp.square(jnp.maximum(u, 0.0)).astype(a_scr.dtype)
    x = x_scr[...] + jnp.dot(a_scr[...], wdn_ref[...], preferred_element_type=F32)
    ms = jnp.mean(x * x, axis=-1, keepdims=True)
    y_ref[...] = (x * lax.rsqrt(ms + NORM_EPS)) * gf_ref[...]


def _mlp(x2d, y_r, y_a, w_out_b, norm2, w_up_b, w_down_b, norm_final, tm):
    m = x2d.shape[0]
    row = lambda i: (i, 0)
    const = lambda i: (0, 0)
    return pl.pallas_call(
        _mlp_kernel,
        grid=(m // tm,),
        in_specs=[
            pl.BlockSpec((tm, D_MODEL), row),
            pl.BlockSpec((tm, RET_WIDTH), row),
            pl.BlockSpec((tm, ATT_WIDTH), row),
            _resident((D_MODEL, D_MODEL), const),
            _resident((1, D_MODEL), const),
            _resident((D_MODEL, D_FF), const),
            _resident((D_FF, D_MODEL), const),
            _resident((1, D_MODEL), const),
        ],
        out_specs=pl.BlockSpec((tm, D_MODEL), row),
        out_shape=jax.ShapeDtypeStruct((m, D_MODEL), F32),
        scratch_shapes=[pltpu.VMEM((tm, D_MODEL), F32), pltpu.VMEM((tm, D_MODEL), BF16),
                        pltpu.VMEM((tm, D_FF), BF16)],
        compiler_params=_compiler_params(("parallel",)),
        name="mlp",
    )(x2d, y_r, y_a, w_out_b, norm2, w_up_b, w_down_b, norm_final)


def _pick_tile(n, target):
    t = min(n, target)
    while n % t:
        t //= 2
    return t


def _layer(x, pos, past, s0, ret_blk, weights, *, attn_qb, attn_tk):
    norm1, w_in_b, gn_gain, w_out_b, norm2, w_up_b, w_down_b, norm_final = weights
    batch, seq, _ = x.shape
    m = batch * seq
    x2d = x.reshape(m, D_MODEL)
    tm = _pick_tile(m, ROW_TILE)

    tabs = _rope_tables(pos)
    if seq < tm:
        tabs = jnp.tile(tabs, (1, tm // seq, 1))
    ret_in, q_a, q_i, kw, k_a, v_a, ka16, vt16, kw16 = _proj(x2d, norm1, w_in_b, tabs, tm)

    y_r, s_new = _retention(ret_in, s0, gn_gain, batch, seq, ret_blk)

    three = lambda a: a.reshape(batch, seq, -1)
    k_i = three(kw[:, :IDX_DH])
    w_t = jnp.swapaxes(three(kw[:, IDX_DH:IDX_DH + IDX_HEADS]) * IDX_HEADS ** -0.5 * IDX_DH ** -0.5, 1, 2)
    q_a3, q_i3 = three(q_a), three(q_i)
    n_keys = seq if past is None else past[0].shape[1] + seq
    topk = min(TOPK_MAX, n_keys // 4)
    tq = -(-seq // attn_qb) * attn_qb
    lp = -(-n_keys // attn_tk) * attn_tk
    padq = lambda a: jnp.pad(a, ((0, 0), (0, tq - seq), (0, 0)))
    padk = lambda a: jnp.pad(a, ((0, 0), (0, lp - n_keys), (0, 0)))
    if past is None and attn_tk == vt16.shape[2] and lp == n_keys:
        keys_i, keys_a = three(kw16), three(ka16)
        v_t = vt16.reshape(batch, lp // attn_tk, ATT_WIDTH, attn_tk)
        n_valid_q = attn_qb
    else:
        keys_i, keys_a, vals = three(kw16), three(ka16), three(v_a).astype(BF16)
        n_valid_q = attn_qb
        if past is not None:
            past_k, past_v, past_ki = past
            pad_i = jnp.pad(past_ki, ((0, 0), (0, 0), (0, LANES - IDX_DH))).astype(BF16)
            keys_i = jnp.concatenate([pad_i, keys_i], axis=1)
            keys_a = jnp.concatenate([past_k.reshape(batch, -1, ATT_WIDTH).astype(BF16), keys_a], axis=1)
            vals = jnp.concatenate([past_v.reshape(batch, -1, ATT_WIDTH).astype(BF16), vals], axis=1)
            n_valid_q = seq
        keys_i, keys_a = padk(keys_i), padk(keys_a)
        v_t = jnp.swapaxes(padk(vals).reshape(batch, lp // attn_tk, attn_tk, ATT_WIDTH), 2, 3)
    lim = jnp.minimum((pos // CHUNK + 1) * CHUNK, n_keys).astype(jnp.int32)
    lim = jnp.pad(lim, (0, tq - seq), constant_values=n_keys).reshape(tq // attn_qb, 1, attn_qb)
    y_a = _attention(
        padq(q_a3), padq(q_i3), jnp.pad(w_t, ((0, 0), (0, 0), (0, tq - seq))), lim, keys_i, keys_a, v_t,
        qb=attn_qb, tk=attn_tk, n_keys=n_keys, topk=topk, causal=past is None, n_valid_q=n_valid_q)
    y_a = y_a[:, :seq].reshape(m, ATT_WIDTH)

    y = _mlp(x2d, y_r, y_a, w_out_b, norm2, w_up_b, w_down_b, norm_final, tm)
    return (y.reshape(batch, seq, D_MODEL),
            k_a.reshape(batch, seq, ATT_HEADS, ATT_DH),
            v_a.reshape(batch, seq, ATT_HEADS, ATT_DH),
            k_i, s_new)


def kernel(x_prompt, x_sample, cache_k_att, cache_v_att, cache_k_idx, state_ret, norm1, w_in, gn_gain, w_out,
           norm2, w_up, w_down, norm_final):
    depth = norm1.shape[0]
    assert depth == 1, "the final norm is fused into the layer's last kernel"
    past_len = cache_k_att.shape[2]
    pos_p = jnp.arange(x_prompt.shape[1], dtype=jnp.int32)
    pos_s = past_len + jnp.arange(x_sample.shape[1], dtype=jnp.int32)
    l = 0
    w_in_b = jnp.pad(w_in[l], ((0, 0), (0, IN_WIDTH_PADDED - IN_WIDTH))).astype(BF16)
    weights = (norm1[l][None], w_in_b, gn_gain[l][None], w_out[l].astype(BF16), norm2[l][None],
               w_up[l].astype(BF16), w_down[l].astype(BF16), norm_final[None])
    s0_p = jnp.zeros((x_prompt.shape[0], RET_HEADS, RET_DK, RET_DV), F32)
    yp, kap, vap, kip, sp = _layer(x_prompt, pos_p, None, s0_p, PROMPT_RETENTION_BLOCK, weights,
                                   attn_qb=PROMPT_QUERY_BLOCK, attn_tk=KEY_TILE)
    past = (cache_k_att[l], cache_v_att[l], cache_k_idx[l])
    ys, kas, vas, kis, ss = _layer(x_sample, pos_s, past, state_ret[l].astype(F32), x_sample.shape[1], weights,
                                   attn_qb=SAMPLE_QUERY_BLOCK, attn_tk=KEY_TILE)
    return (yp, ys, kap[None], vap[None], kip[None], sp[None], kas[None], vas[None], kis[None], ss[None])
```

```python
import functools

import jax
import jax.numpy as jnp
import numpy as np
from jax import lax
from jax.experimental import pallas as pl
from jax.experimental.pallas import tpu as pltpu

D_MODEL = 1024
CHUNK = 64
ROPE_THETA = 10000.0
NORM_EPS = 1e-6
NEG = -1e30
RET_WIDTH = 512
ATT_WIDTH = 512
RET_HEADS = 4
RET_DK = 128
RET_DV = 128
ATT_HEADS = 8
ATT_DH = 64
IDX_HEADS = 8
IDX_DH = 64
TOPK_MAX = 256
D_FF = 4 * D_MODEL
IN_WIDTH = 4 * RET_WIDTH + 3 * ATT_WIDTH + IDX_HEADS * IDX_DH + IDX_DH + IDX_HEADS

LANES = 128
IN_WIDTH_PADDED = ((IN_WIDTH + LANES - 1) // LANES) * LANES
VMEM_LIMIT_BYTES = 56 * 1024 * 1024

ROW_TILE = 512
KEY_TILE = 256
PROMPT_QUERY_BLOCK = 256
SAMPLE_QUERY_BLOCK = 128
PROMPT_RETENTION_BLOCK = 256

F32 = jnp.float32
BF16 = jnp.bfloat16

_NT = (((1,), (1,)), ((), ()))


def _compiler_params(semantics):
    return pltpu.CompilerParams(dimension_semantics=semantics, vmem_limit_bytes=VMEM_LIMIT_BYTES)


def _resident(block_shape, index_map):
    return pl.BlockSpec(block_shape, index_map, pipeline_mode=pl.Buffered(1))


def _rope_tables(pos):
    posf = pos.astype(F32)[:, None]

    def angles(d):
        half = d // 2
        inv = ROPE_THETA ** (-jnp.arange(half, dtype=F32) * 2.0 / d)
        ang = posf * inv[None, :]
        return jnp.cos(ang), jnp.sin(ang)

    c128, s128 = angles(128)
    zero64 = jnp.zeros_like(s128)
    cos128 = jnp.concatenate([c128, c128], axis=1)
    sin128 = jnp.concatenate([-s128, s128], axis=1)
    c64, s64 = angles(64)
    zero32 = jnp.zeros_like(s64)
    cos64 = jnp.concatenate([c64, c64, c64, c64], axis=1)
    sin64_up = jnp.concatenate([zero32, s64, zero32, s64], axis=1)
    sin64_dn = jnp.concatenate([-s64, zero32, -s64, zero32], axis=1)
    one64 = jnp.ones_like(zero64)
    cosk = jnp.concatenate([c64, c64, one64], axis=1)
    sink_up = jnp.concatenate([zero32, s64, zero64], axis=1)
    sink_dn = jnp.concatenate([-s64, zero32, zero64], axis=1)
    return jnp.stack([cos128, sin128, cos64, sin64_up, sin64_dn, cosk, sink_up, sink_dn])


V_TILE = KEY_TILE


def _proj_kernel(x_ref, g_ref, w_ref, tab_ref, ret_ref, qa_ref, qi_ref, kw_ref, ka_ref, va_ref,
                 ka16_ref, vt16_ref, kw16_ref):
    x = x_ref[...]
    ms = jnp.mean(x * x, axis=-1, keepdims=True)
    h = ((x * lax.rsqrt(ms + NORM_EPS)) * g_ref[...]).astype(BF16)

    def mm(c0, n):
        return jnp.dot(h, w_ref[:, c0:c0 + n], preferred_element_type=F32)

    def rope128(z):
        return z * tab_ref[0] + pltpu.roll(z, 64, 1) * tab_ref[1]

    def rope64(z):
        return z * tab_ref[2] + pltpu.roll(z, 32, 1) * tab_ref[3] + pltpu.roll(z, 96, 1) * tab_ref[4]

    def ropek(z):
        return z * tab_ref[5] + pltpu.roll(z, 32, 1) * tab_ref[6] + pltpu.roll(z, 96, 1) * tab_ref[7]

    def emit(out_ref, out_c0, w_c0, rope):
        z = mm(w_c0, 512)
        for t in range(4):
            zt = z[:, t * LANES:(t + 1) * LANES]
            if rope is not None:
                zt = rope(zt)
            out_ref[:, out_c0 + t * LANES:out_c0 + (t + 1) * LANES] = zt.astype(out_ref.dtype)

    emit(ret_ref, 0, 0, rope128)
    emit(ret_ref, 512, 512, rope128)
    emit(ret_ref, 1024, 1024, None)
    emit(ret_ref, 1536, 1536, None)
    emit(qa_ref, 0, 2048, rope64)
    emit(qi_ref, 0, 3584, rope64)
    k_a = mm(2560, 512)
    k_a = jnp.concatenate([rope64(k_a[:, t * LANES:(t + 1) * LANES]) for t in range(4)], axis=1)
    ka_ref[...] = k_a
    ka16_ref[...] = k_a.astype(ka16_ref.dtype)
    v_a = mm(3072, 512)
    va_ref[...] = v_a
    v_tile = vt16_ref.shape[2]
    for part in range(vt16_ref.shape[0]):
        vt16_ref[part] = v_a[part * v_tile:(part + 1) * v_tile, :].T.astype(vt16_ref.dtype)
    kw = ropek(mm(4096, LANES))
    kw_ref[...] = kw
    kw16_ref[...] = kw.astype(kw16_ref.dtype)


def _proj(x2d, norm_g, w_in_b, tabs, tm):
    m = x2d.shape[0]
    n_tab = tabs.shape[1] // tm
    row = lambda i: (i, 0)
    row3 = lambda i: (i, 0, 0)
    assert tm % V_TILE == 0 or m == tm, (tm, m)
    n_vt = max(tm // V_TILE, 1)
    v_tile = min(V_TILE, tm)
    outs = (
        jax.ShapeDtypeStruct((m, 4 * RET_WIDTH), F32),
        jax.ShapeDtypeStruct((m, ATT_WIDTH), F32),
        jax.ShapeDtypeStruct((m, IDX_HEADS * IDX_DH), F32),
        jax.ShapeDtypeStruct((m, LANES), F32),
        jax.ShapeDtypeStruct((m, ATT_WIDTH), F32),
        jax.ShapeDtypeStruct((m, ATT_WIDTH), F32),
        jax.ShapeDtypeStruct((m, ATT_WIDTH), BF16),
        jax.ShapeDtypeStruct((m // v_tile, ATT_WIDTH, v_tile), BF16),
        jax.ShapeDtypeStruct((m, LANES), BF16),
    )
    return pl.pallas_call(
        _proj_kernel,
        grid=(m // tm,),
        in_specs=[
            pl.BlockSpec((tm, D_MODEL), row),
            _resident((1, D_MODEL), lambda i: (0, 0)),
            _resident((D_MODEL, IN_WIDTH_PADDED), lambda i: (0, 0)),
            pl.BlockSpec((8, tm, LANES), lambda i: (0, i % n_tab, 0)),
        ],
        out_specs=[
            pl.BlockSpec((tm, 4 * RET_WIDTH), row),
            pl.BlockSpec((tm, ATT_WIDTH), row),
            pl.BlockSpec((tm, IDX_HEADS * IDX_DH), row),
            pl.BlockSpec((tm, LANES), row),
            pl.BlockSpec((tm, ATT_WIDTH), row),
            pl.BlockSpec((tm, ATT_WIDTH), row),
            pl.BlockSpec((tm, ATT_WIDTH), row),
            pl.BlockSpec((n_vt, ATT_WIDTH, v_tile), row3),
            pl.BlockSpec((tm, LANES), row),
        ],
        out_shape=outs,
        compiler_params=_compiler_params(("parallel",)),
        name="proj",
    )(x2d, norm_g, w_in_b, tabs)


def _retention_tables(blk):
    lg = jnp.log1p(-jnp.exp2(-5.0 - jnp.arange(RET_HEADS, dtype=F32)))
    i = jnp.arange(blk, dtype=F32)
    diff = i[:, None] - i[None, :]
    dmat = jnp.where(diff[None] >= 0, jnp.exp(lg[:, None, None] * jnp.maximum(diff, 0.0)[None]), 0.0)
    w_end = jnp.exp(lg[:, None] * (blk - 1.0 - i)[None, :])
    w_q = jnp.exp(lg[:, None] * (i + 1.0)[None, :])
    g_block = jnp.exp(lg * blk)
    lane = lambda a: jnp.broadcast_to(a[:, :, None], (RET_HEADS, blk, LANES))
    return dmat, lane(w_q), lane(w_end), g_block


def _retention_kernel(gblk_ref, q_ref, k_ref, v_ref, g_ref, s0_ref, dmat_ref, wq_ref, wend_ref, gain_ref,
                      y_ref, sfin_ref, s_scr):
    t = pl.program_id(1)

    @pl.when(t == 0)
    def _():
        s_scr[...] = s0_ref[0]

    for hd in range(RET_HEADS):
        cols = slice(hd * RET_DK, (hd + 1) * RET_DK)
        q = q_ref[:, cols]
        k = k_ref[:, cols] * RET_DK ** -0.5
        vb = v_ref[:, cols].astype(BF16)
        state = s_scr[hd]
        scores = lax.dot_general(q.astype(BF16), k.astype(BF16), _NT, preferred_element_type=F32) * dmat_ref[hd]
        o = jnp.dot(scores.astype(BF16), vb, preferred_element_type=F32)
        o = o + jnp.dot((q * wq_ref[hd]).astype(BF16), state.astype(BF16), preferred_element_type=F32)
        k_end_t = (k * wend_ref[hd]).T.astype(BF16)
        s_scr[hd] = gblk_ref[hd] * state + jnp.dot(k_end_t, vb, preferred_element_type=F32)
        mu = jnp.mean(o, axis=-1, keepdims=True)
        var = jnp.mean(jnp.square(o - mu), axis=-1, keepdims=True)
        on = ((o - mu) * lax.rsqrt(var + NORM_EPS)) * gain_ref[:, cols]
        g = g_ref[:, cols]
        y_ref[:, cols] = ((g * jax.nn.sigmoid(g)) * on).astype(y_ref.dtype)

    @pl.when(t == pl.num_programs(1) - 1)
    def _():
        sfin_ref[0] = s_scr[...]


def _retention(ret_in, s0, gn_gain, batch, seq, blk):
    nt = seq // blk
    dmat, w_q, w_end, g_block = _retention_tables(blk)

    def col(c):
        return pl.BlockSpec((blk, RET_WIDTH), lambda b, t: (b * nt + t, c))

    const3 = lambda b, t: (0, 0, 0)
    return pl.pallas_call(
        _retention_kernel,
        grid=(batch, nt),
        in_specs=[
            pl.BlockSpec(memory_space=pltpu.SMEM),
            col(0), col(1), col(2), col(3),
            pl.BlockSpec((1, RET_HEADS, RET_DK, RET_DV), lambda b, t: (b, 0, 0, 0)),
            pl.BlockSpec((RET_HEADS, blk, blk), const3),
            pl.BlockSpec((RET_HEADS, blk, LANES), const3),
            pl.BlockSpec((RET_HEADS, blk, LANES), const3),
            pl.BlockSpec((1, RET_WIDTH), lambda b, t: (0, 0)),
        ],
        out_specs=[
            pl.BlockSpec((blk, RET_WIDTH), lambda b, t: (b * nt + t, 0)),
            pl.BlockSpec((1, RET_HEADS, RET_DK, RET_DV), lambda b, t: (b, 0, 0, 0)),
        ],
        out_shape=(
            jax.ShapeDtypeStruct((batch * seq, RET_WIDTH), BF16),
            jax.ShapeDtypeStruct((batch, RET_HEADS, RET_DK, RET_DV), F32),
        ),
        scratch_shapes=[pltpu.VMEM((RET_HEADS, RET_DK, RET_DV), F32)],
        compiler_params=_compiler_params(("parallel", "arbitrary")),
        name="retention",
    )(g_block, ret_in, ret_in, ret_in, ret_in, s0, dmat, w_q, w_end, gn_gain)


COUNT_ROWS = 64
COUNT_UNROLL = 8
TILE_UNROLL = 4
MASKED = 2.0 * NEG
LOG2_E = 1.4426950408889634


def _ordered_bits_to_float(u):
    bits = jnp.where(u < 0, u ^ jnp.int32(-2 ** 31), ~u)
    return lax.bitcast_convert_type(bits, F32)


def _signed_half(x):
    return (x - 2 ** 15).astype(jnp.int16)


def _attention_kernel(lim_ref, qa_ref, qi_ref, wt_ref, ki_ref, ka_ref, vt_ref, o_ref,
                      s_scr, hi16_scr, lo16_scr, xa_scr, xb_scr, acc_scr,
                      *, qb, tk, n_keys, topk, causal, n_valid_q, index_bits):
    n_kt = pl.program_id(1) * (qb // tk) + (qb // tk) if causal else ki_ref.shape[0] // tk
    lim = lim_ref[0]
    lane = lax.broadcasted_iota(jnp.int32, (qb, LANES), 1)
    key_iota = lax.broadcasted_iota(jnp.int32, (tk, qb), 0)

    def head_operand(ref, hd, scale, shift_to_low):
        pair = ref[:, (hd // 2) * LANES:(hd // 2 + 1) * LANES] * scale
        if shift_to_low and hd % 2:
            pair = pltpu.roll(pair, 64, 1)
        keep = (lane < 64) if (shift_to_low or hd % 2 == 0) else (lane >= 64)
        return jnp.where(keep, pair, 0.0).T.astype(BF16)

    qi_heads = [head_operand(qi_ref, hd, 1.0, True) for hd in range(IDX_HEADS)]
    wt = wt_ref[0]

    def score_body(kt, carry, some_inadmissible):
        r0 = pl.multiple_of(kt * tk, tk)
        keys = ki_ref[pl.ds(r0, tk), :]
        acc = jnp.zeros((tk, qb), F32)
        for hd in range(IDX_HEADS):
            d = jnp.dot(keys, qi_heads[hd], preferred_element_type=F32)
            acc = acc + jnp.maximum(d, 0.0) * wt[hd:hd + 1, :]
        s = jnp.where(acc == 0.0, 0.0, acc)
        if some_inadmissible:
            s = jnp.where(key_iota + r0 < lim, s, -jnp.inf)
        s_scr[pl.ds(r0, tk), :] = s
        bits = lax.bitcast_convert_type(s, jnp.int32)
        ordered = bits ^ (lax.shift_right_arithmetic(bits, 31) | jnp.int32(-2 ** 31))
        hi16_scr[pl.ds(r0, tk), :] = _signed_half(lax.shift_right_logical(ordered, 16))
        lo16_scr[pl.ds(r0, tk), :] = _signed_half(ordered & 0xFFFF)
        return carry

    n_all_admissible = jnp.minimum(jnp.min(lim) // tk, n_kt)
    lax.fori_loop(0, n_all_admissible, functools.partial(score_body, some_inadmissible=False), 0)
    lax.fori_loop(n_all_admissible, n_kt, functools.partial(score_body, some_inadmissible=True), 0)

    def count(src_ref, indicator, dtype):
        def one_tile(kt, acc):
            for part in range(tk // COUNT_ROWS):
                r0 = pl.multiple_of(kt * tk + part * COUNT_ROWS, COUNT_ROWS)
                acc = acc + indicator(src_ref[pl.ds(r0, COUNT_ROWS), :], r0)
            return acc
        def tile_group(g, acc):
            for j in range(COUNT_UNROLL):
                acc = one_tile(COUNT_UNROLL * g + j, acc)
            return acc

        acc = lax.fori_loop(0, n_kt // COUNT_UNROLL, tile_group, jnp.zeros((COUNT_ROWS, qb), dtype))
        acc = lax.fori_loop(n_kt - n_kt % COUNT_UNROLL, n_kt, one_tile, acc)
        return jnp.sum(acc.astype(F32), axis=0, keepdims=True)

    n_masked = (n_keys - lim).astype(F32)
    kf = float(topk)
    q_lane = lax.broadcasted_iota(jnp.int32, (1, qb), 1)
    one16 = jnp.ones((), BF16)
    zero16 = jnp.zeros((), BF16)

    def bisect_step(it, carry, half_ref):
        cur, cnt = carry
        cand = cur | lax.shift_left(jnp.int32(1), 31 - it)
        if half_ref is hi16_scr:
            half = _signed_half(lax.shift_right_logical(cand, 16))
        else:
            half = _signed_half(cand & 0xFFFF)
        c = count(half_ref, lambda h, r0: jnp.where(h >= half, one16, zero16), BF16)
        ok = c + jnp.where(_ordered_bits_to_float(cand) <= NEG, n_masked, 0.0) >= kf
        return jnp.where(ok, cand, cur), jnp.where(ok, c, cnt)

    carry = (jnp.zeros((1, qb), jnp.int32), jnp.zeros((1, qb), F32))
    carry = lax.fori_loop(0, 16, lambda it, st: bisect_step(it, st, hi16_scr), carry)

    top = _signed_half(lax.shift_right_logical(carry[0], 16))

    def for_each_tile(body):
        def group(g, c):
            for j in range(TILE_UNROLL):
                body(TILE_UNROLL * g + j)
            return c

        def single(kt, c):
            body(kt)
            return c

        lax.fori_loop(0, n_kt // TILE_UNROLL, group, 0)
        lax.fori_loop(n_kt - n_kt % TILE_UNROLL, n_kt, single, 0)

    def narrow_body(kt):
        r0 = pl.multiple_of(kt * tk, tk)
        hi = hi16_scr[pl.ds(r0, tk), :]
        lo = lo16_scr[pl.ds(r0, tk), :]
        lo16_scr[pl.ds(r0, tk), :] = jnp.where(hi > top, jnp.int16(2 ** 15 - 1),
                                               jnp.where(hi == top, lo, jnp.int16(-2 ** 15)))

    for_each_tile(narrow_body)
    cur, cnt_ge = lax.fori_loop(16, 32, lambda it, st: bisect_step(it, st, lo16_scr), carry)
    theta = _ordered_bits_to_float(cur)

    overflow = jnp.where((cnt_ge > kf) & (q_lane < n_valid_q), 1.0, 0.0)
    all_idx = jnp.full((1, qb), 2 ** index_bits - 1, jnp.int32)

    def tie_cut():
        need = kf - count(s_scr, lambda s, r0: jnp.where(s > theta, 1.0, 0.0), F32)

        def mark_body(kt):
            r0 = pl.multiple_of(kt * tk, tk)
            tied = jnp.where(s_scr[pl.ds(r0, tk), :] == theta, key_iota + r0, 2 ** 15 - 1)
            hi16_scr[pl.ds(r0, tk), :] = tied.astype(jnp.int16)

        for_each_tile(mark_body)

        def body(it, cut):
            cand = cut | lax.shift_left(jnp.int32(1), index_bits - 1 - it)
            cand16 = cand.astype(jnp.int16)
            c = count(hi16_scr, lambda idx, r0: jnp.where(idx < cand16, one16, zero16), BF16)
            return jnp.where(c <= need, cand, cut)

        return lax.fori_loop(0, index_bits, body, jnp.zeros((1, qb), jnp.int32))

    any_overflow = jnp.max(overflow) > 0.0
    idx_cut = lax.cond(any_overflow, tie_cut, lambda: all_idx)

    def bias_body(kt, with_ties):
        r0 = pl.multiple_of(kt * tk, tk)
        s = s_scr[pl.ds(r0, tk), :]
        if with_ties:
            tie = jnp.where(key_iota + r0 < idx_cut, 0.0, MASKED)
            s_scr[pl.ds(r0, tk), :] = jnp.where(s > theta, 0.0, jnp.where(s == theta, tie, MASKED))
        else:
            s_scr[pl.ds(r0, tk), :] = jnp.where(s >= theta, 0.0, MASKED)

    lax.cond(any_overflow,
             lambda: for_each_tile(functools.partial(bias_body, with_ties=True)),
             lambda: for_each_tile(functools.partial(bias_body, with_ties=False)))

    qa_heads = [head_operand(qa_ref, hd, ATT_DH ** -0.5 * LOG2_E, False) for hd in range(ATT_HEADS)]
    acc_scr[...] = jnp.zeros(acc_scr.shape, F32)

    def logits_stage(kt, x_ref):
        r0 = pl.multiple_of(kt * tk, tk)
        tops = []
        for hd in range(ATT_HEADS):
            keys = ka_ref[pl.ds(r0, tk), (hd // 2) * LANES:(hd // 2 + 1) * LANES]
            x = jnp.dot(keys, qa_heads[hd], preferred_element_type=F32) + s_scr[pl.ds(r0, tk), :]
            x_ref[hd] = x.astype(x_ref.dtype)
            tops.append(jnp.max(x, axis=0, keepdims=True))
        return jnp.concatenate(tops, axis=0).astype(x_ref.dtype).astype(F32)

    def accumulate(kt_in, x_ref, tile_max, m_run, l_run):
        m_new = jnp.maximum(m_run, tile_max)
        alpha = jnp.exp2(m_run - m_new)
        m_staged = m_new.astype(x_ref.dtype)
        sums = []
        for hd in range(ATT_HEADS):
            p = jnp.exp2(x_ref[hd] - m_staged[hd:hd + 1, :]).astype(BF16)
            sums.append(jnp.sum(p.astype(F32), axis=0, keepdims=True))
            rows = slice(hd * ATT_DH, (hd + 1) * ATT_DH)
            pv = jnp.dot(vt_ref[kt_in, rows, :], p, preferred_element_type=F32)
            acc_scr[rows, :] = alpha[hd:hd + 1, :] * acc_scr[rows, :] + pv
        return m_new, alpha * l_run + jnp.concatenate(sums, axis=0)

    def pair_body(pair, carry, stage_next=True):
        max_a, m_run, l_run = carry
        max_b = logits_stage(2 * pair + 1, xb_scr)
        m_run, l_run = accumulate(2 * pair, xa_scr, max_a, m_run, l_run)
        if stage_next:
            max_a = logits_stage(2 * pair + 2, xa_scr)
        m_run, l_run = accumulate(2 * pair + 1, xb_scr, max_b, m_run, l_run)
        return max_a, m_run, l_run

    n_pairs = (n_kt + 1) // 2
    first_max = logits_stage(0, xa_scr)
    carry = lax.fori_loop(
        0, n_pairs - 1, pair_body,
        (first_max, jnp.full((ATT_HEADS, qb), NEG, F32), jnp.zeros((ATT_HEADS, qb), F32)))
    l_fin = lax.cond(
        n_kt % 2 == 0,
        lambda c: pair_body(n_pairs - 1, c, stage_next=False)[2],
        lambda c: accumulate(2 * (n_pairs - 1), xa_scr, *c)[1],
        carry)

    for hd in range(ATT_HEADS):
        rows = slice(hd * ATT_DH, (hd + 1) * ATT_DH)
        acc_scr[rows, :] = acc_scr[rows, :] / l_fin[hd:hd + 1, :]
    o_ref[...] = acc_scr[...].T.astype(o_ref.dtype)


def _attention(q_a, q_i, w_t, lim, k_i, k_a, v_t, *, qb, tk, n_keys, topk, causal, n_valid_q):
    batch, tq, _ = q_a.shape
    lp = k_i.shape[1]
    nq = tq // qb
    index_bits = int(np.ceil(np.log2(lp + 1)))
    assert lp // COUNT_ROWS <= 256, "bfloat16 partial counts are exact only up to 256"
    kernel = functools.partial(_attention_kernel, qb=qb, tk=tk, n_keys=n_keys, topk=topk, causal=causal,
                               n_valid_q=n_valid_q, index_bits=index_bits)
    return pl.pallas_call(
        kernel,
        grid=(batch, nq),
        in_specs=[
            pl.BlockSpec((1, 1, qb), lambda b, i: (i, 0, 0)),
            pl.BlockSpec((None, qb, ATT_WIDTH), lambda b, i: (b, i, 0)),
            pl.BlockSpec((None, qb, IDX_HEADS * IDX_DH), lambda b, i: (b, i, 0)),
            pl.BlockSpec((1, IDX_HEADS, qb), lambda b, i: (b, 0, i)),
            _resident((None, lp, LANES), lambda b, i: (b, 0, 0)),
            _resident((None, lp, ATT_WIDTH), lambda b, i: (b, 0, 0)),
            _resident((None, lp // tk, ATT_WIDTH, tk), lambda b, i: (b, 0, 0, 0)),
        ],
        out_specs=pl.BlockSpec((None, qb, ATT_WIDTH), lambda b, i: (b, i, 0)),
        out_shape=jax.ShapeDtypeStruct((batch, tq, ATT_WIDTH), BF16),
        scratch_shapes=[
            pltpu.VMEM((lp, qb), F32),
            pltpu.VMEM((lp, qb), jnp.int16),
            pltpu.VMEM((lp, qb), jnp.int16),
            pltpu.VMEM((ATT_HEADS, tk, qb), BF16),
            pltpu.VMEM((ATT_HEADS, tk, qb), BF16),
            pltpu.VMEM((ATT_WIDTH, qb), F32),
        ],
        compiler_params=_compiler_params(("parallel", "arbitrary")),
        name="attention",
    )(lim, q_a, q_i, w_t, k_i, k_a, v_t)


FF_CHUNK = 512


def _mlp_kernel(x_ref, yr_ref, ya_ref, wo_ref, g2_ref, wup_ref, wdn_ref, gf_ref, y_ref, x_scr, h_scr, a_scr):
    x = x_ref[...] + jnp.dot(yr_ref[...], wo_ref[:RET_WIDTH, :], preferred_element_type=F32)
    x = x + jnp.dot(ya_ref[...], wo_ref[RET_WIDTH:, :], preferred_element_type=F32)
    x_scr[...] = x
    ms = jnp.mean(x * x, axis=-1, keepdims=True)
    h_scr[...] = ((x * lax.rsqrt(ms + NORM_EPS)) * g2_ref[...]).astype(h_scr.dtype)
    for c in range(D_FF // FF_CHUNK):
        cols = slice(c * FF_CHUNK, (c + 1) * FF_CHUNK)
        u = jnp.dot(h_scr[...], wup_ref[:, cols], preferred_element_type=F32)
        a_scr[:, cols] = jnp.square(jnp.maximum(u, 0.0)).astype(a_scr.dtype)
    x = x_scr[...] + jnp.dot(a_scr[...], wdn_ref[...], preferred_element_type=F32)
    ms = jnp.mean(x * x, axis=-1, keepdims=True)
    y_ref[...] = (x * lax.rsqrt(ms + NORM_EPS)) * gf_ref[...]


def _mlp(x2d, y_r, y_a, w_out_b, norm2, w_up_b, w_down_b, norm_final, tm):
    m = x2d.shape[0]
    row = lambda i: (i, 0)
    const = lambda i: (0, 0)
    return pl.pallas_call(
        _mlp_kernel,
        grid=(m // tm,),
        in_specs=[
            pl.BlockSpec((tm, D_MODEL), row),
            pl.BlockSpec((tm, RET_WIDTH), row),
            pl.BlockSpec((tm, ATT_WIDTH), row),
            _resident((D_MODEL, D_MODEL), const),
            _resident((1, D_MODEL), const),
            _resident((D_MODEL, D_FF), const),
            _resident((D_FF, D_MODEL), const),
            _resident((1, D_MODEL), const),
        ],
        out_specs=pl.BlockSpec((tm, D_MODEL), row),
        out_shape=jax.ShapeDtypeStruct((m, D_MODEL), F32),
        scratch_shapes=[pltpu.VMEM((tm, D_MODEL), F32), pltpu.VMEM((tm, D_MODEL), BF16),
                        pltpu.VMEM((tm, D_FF), BF16)],
        compiler_params=_compiler_params(("parallel",)),
        name="mlp",
    )(x2d, y_r, y_a, w_out_b, norm2, w_up_b, w_down_b, norm_final)


def _pick_tile(n, target):
    t = min(n, target)
    while n % t:
        t //= 2
    return t


def _layer(x, pos, past, s0, ret_blk, weights, *, attn_qb, attn_tk):
    norm1, w_in_b, gn_gain, w_out_b, norm2, w_up_b, w_down_b, norm_final = weights
    batch, seq, _ = x.shape
    m = batch * seq
    x2d = x.reshape(m, D_MODEL)
    tm = _pick_tile(m, ROW_TILE)

    tabs = _rope_tables(pos)
    if seq < tm:
        tabs = jnp.tile(tabs, (1, tm // seq, 1))
    ret_in, q_a, q_i, kw, k_a, v_a, ka16, vt16, kw16 = _proj(x2d, norm1, w_in_b, tabs, tm)

    y_r, s_new = _retention(ret_in, s0, gn_gain, batch, seq, ret_blk)

    three = lambda a: a.reshape(batch, seq, -1)
    k_i = three(kw[:, :IDX_DH])
    w_t = jnp.swapaxes(three(kw[:, IDX_DH:IDX_DH + IDX_HEADS]) * IDX_HEADS ** -0.5 * IDX_DH ** -0.5, 1, 2)
    q_a3, q_i3 = three(q_a), three(q_i)
    n_keys = seq if past is None else past[0].shape[1] + seq
    topk = min(TOPK_MAX, n_keys // 4)
    tq = -(-seq // attn_qb) * attn_qb
    lp = -(-n_keys // attn_tk) * attn_tk
    padq = lambda a: jnp.pad(a, ((0, 0), (0, tq - seq), (0, 0)))
    padk = lambda a: jnp.pad(a, ((0, 0), (0, lp - n_keys), (0, 0)))
    if past is None and attn_tk == vt16.shape[2] and lp == n_keys:
        keys_i, keys_a = three(kw16), three(ka16)
        v_t = vt16.reshape(batch, lp // attn_tk, ATT_WIDTH, attn_tk)
        n_valid_q = attn_qb
    else:
        keys_i, keys_a, vals = three(kw16), three(ka16), three(v_a).astype(BF16)
        n_valid_q = attn_qb
        if past is not None:
            past_k, past_v, past_ki = past
            pad_i = jnp.pad(past_ki, ((0, 0), (0, 0), (0, LANES - IDX_DH))).astype(BF16)
            keys_i = jnp.concatenate([pad_i, keys_i], axis=1)
            keys_a = jnp.concatenate([past_k.reshape(batch, -1, ATT_WIDTH).astype(BF16), keys_a], axis=1)
            vals = jnp.concatenate([past_v.reshape(batch, -1, ATT_WIDTH).astype(BF16), vals], axis=1)
            n_valid_q = seq
        keys_i, keys_a = padk(keys_i), padk(keys_a)
        v_t = jnp.swapaxes(padk(vals).reshape(batch, lp // attn_tk, attn_tk, ATT_WIDTH), 2, 3)
    lim = jnp.minimum((pos // CHUNK + 1) * CHUNK, n_keys).astype(jnp.int32)
    lim = jnp.pad(lim, (0, tq - seq), constant_values=n_keys).reshape(tq // attn_qb, 1, attn_qb)
    y_a = _attention(
        padq(q_a3), padq(q_i3), jnp.pad(w_t, ((0, 0), (0, 0), (0, tq - seq))), lim, keys_i, keys_a, v_t,
        qb=attn_qb, tk=attn_tk, n_keys=n_keys, topk=topk, causal=past is None, n_valid_q=n_valid_q)
    y_a = y_a[:, :seq].reshape(m, ATT_WIDTH)

    y = _mlp(x2d, y_r, y_a, w_out_b, norm2, w_up_b, w_down_b, norm_final, tm)
    return (y.reshape(batch, seq, D_MODEL),
            k_a.reshape(batch, seq, ATT_HEADS, ATT_DH),
            v_a.reshape(batch, seq, ATT_HEADS, ATT_DH),
            k_i, s_new)


def kernel(x_prompt, x_sample, cache_k_att, cache_v_att, cache_k_idx, state_ret, norm1, w_in, gn_gain, w_out,
           norm2, w_up, w_down, norm_final):
    depth = norm1.shape[0]
    assert depth == 1, "the final norm is fused into the layer's last kernel"
    past_len = cache_k_att.shape[2]
    pos_p = jnp.arange(x_prompt.shape[1], dtype=jnp.int32)
    pos_s = past_len + jnp.arange(x_sample.shape[1], dtype=jnp.int32)
    l = 0
    w_in_b = jnp.pad(w_in[l], ((0, 0), (0, IN_WIDTH_PADDED - IN_WIDTH))).astype(BF16)
    weights = (norm1[l][None], w_in_b, gn_gain[l][None], w_out[l].astype(BF16), norm2[l][None],
               w_up[l].astype(BF16), w_down[l].astype(BF16), norm_final[None])
    s0_p = jnp.zeros((x_prompt.shape[0], RET_HEADS, RET_DK, RET_DV), F32)
    yp, kap, vap, kip, sp = _layer(x_prompt, pos_p, None, s0_p, PROMPT_RETENTION_BLOCK, weights,
                                   attn_qb=PROMPT_QUERY_BLOCK, attn_tk=KEY_TILE)
    past = (cache_k_att[l], cache_v_att[l], cache_k_idx[l])
    ys, kas, vas, kis, ss = _layer(x_sample, pos_s, past, state_ret[l].astype(F32), x_sample.shape[1], weights,
                                   attn_qb=SAMPLE_QUERY_BLOCK, attn_tk=KEY_TILE)
    return (yp, ys, kap[None], vap[None], kip[None], sp[None], kas[None], vas[None], kis[None], ss[None])
```

```python
import functools

import jax
import jax.numpy as jnp
import numpy as np
from jax import lax
from jax.experimental import pallas as pl
from jax.experimental.pallas import tpu as pltpu

D_MODEL = 1024
CHUNK = 64
ROPE_THETA = 10000.0
NORM_EPS = 1e-6
NEG = -1e30
RET_WIDTH = 512
ATT_WIDTH = 512
RET_HEADS = 4
RET_DK = 128
RET_DV = 128
ATT_HEADS = 8
ATT_DH = 64
IDX_HEADS = 8
IDX_DH = 64
TOPK_MAX = 256
D_FF = 4 * D_MODEL
IN_WIDTH = 4 * RET_WIDTH + 3 * ATT_WIDTH + IDX_HEADS * IDX_DH + IDX_DH + IDX_HEADS

LANES = 128
IN_WIDTH_PADDED = ((IN_WIDTH + LANES - 1) // LANES) * LANES
VMEM_LIMIT_BYTES = 56 * 1024 * 1024

ROW_TILE = 512
KEY_TILE = 256
PROMPT_QUERY_BLOCK = 256
SAMPLE_QUERY_BLOCK = 128
PROMPT_RETENTION_BLOCK = 256

F32 = jnp.float32
BF16 = jnp.bfloat16

_NT = (((1,), (1,)), ((), ()))


def _compiler_params(semantics):
    return pltpu.CompilerParams(dimension_semantics=semantics, vmem_limit_bytes=VMEM_LIMIT_BYTES)


def _resident(block_shape, index_map):
    return pl.BlockSpec(block_shape, index_map, pipeline_mode=pl.Buffered(1))


def _rope_tables(pos):
    posf = pos.astype(F32)[:, None]

    def angles(d):
        half = d // 2
        inv = ROPE_THETA ** (-jnp.arange(half, dtype=F32) * 2.0 / d)
        ang = posf * inv[None, :]
        return jnp.cos(ang), jnp.sin(ang)

    c128, s128 = angles(128)
    zero64 = jnp.zeros_like(s128)
    cos128 = jnp.concatenate([c128, c128], axis=1)
    sin128 = jnp.concatenate([-s128, s128], axis=1)
    c64, s64 = angles(64)
    zero32 = jnp.zeros_like(s64)
    cos64 = jnp.concatenate([c64, c64, c64, c64], axis=1)
    sin64_up = jnp.concatenate([zero32, s64, zero32, s64], axis=1)
    sin64_dn = jnp.concatenate([-s64, zero32, -s64, zero32], axis=1)
    one64 = jnp.ones_like(zero64)
    cosk = jnp.concatenate([c64, c64, one64], axis=1)
    sink_up = jnp.concatenate([zero32, s64, zero64], axis=1)
    sink_dn = jnp.concatenate([-s64, zero32, zero64], axis=1)
    return jnp.stack([cos128, sin128, cos64, sin64_up, sin64_dn, cosk, sink_up, sink_dn])


V_TILE = KEY_TILE


def _proj_kernel(x_ref, g_ref, w_ref, tab_ref, ret_ref, qa_ref, qi_ref, kw_ref, ka_ref, va_ref,
                 ka16_ref, vt16_ref, kw16_ref):
    x = x_ref[...]
    ms = jnp.mean(x * x, axis=-1, keepdims=True)
    h = ((x * lax.rsqrt(ms + NORM_EPS)) * g_ref[...]).astype(BF16)

    def mm(c0, n):
        return jnp.dot(h, w_ref[:, c0:c0 + n], preferred_element_type=F32)

    def rope128(z):
        return z * tab_ref[0] + pltpu.roll(z, 64, 1) * tab_ref[1]

    def rope64(z):
        return z * tab_ref[2] + pltpu.roll(z, 32, 1) * tab_ref[3] + pltpu.roll(z, 96, 1) * tab_ref[4]

    def ropek(z):
        return z * tab_ref[5] + pltpu.roll(z, 32, 1) * tab_ref[6] + pltpu.roll(z, 96, 1) * tab_ref[7]

    def emit(out_ref, out_c0, w_c0, rope):
        z = mm(w_c0, 512)
        for t in range(4):
            zt = z[:, t * LANES:(t + 1) * LANES]
            if rope is not None:
                zt = rope(zt)
            out_ref[:, out_c0 + t * LANES:out_c0 + (t + 1) * LANES] = zt.astype(out_ref.dtype)

    emit(ret_ref, 0, 0, rope128)
    emit(ret_ref, 512, 512, rope128)
    emit(ret_ref, 1024, 1024, None)
    emit(ret_ref, 1536, 1536, None)
    emit(qa_ref, 0, 2048, rope64)
    emit(qi_ref, 0, 3584, rope64)
    k_a = mm(2560, 512)
    k_a = jnp.concatenate([rope64(k_a[:, t * LANES:(t + 1) * LANES]) for t in range(4)], axis=1)
    ka_ref[...] = k_a
    ka16_ref[...] = k_a.astype(ka16_ref.dtype)
    v_a = mm(3072, 512)
    va_ref[...] = v_a
    v_tile = vt16_ref.shape[2]
    for part in range(vt16_ref.shape[0]):
        vt16_ref[part] = v_a[part * v_tile:(part + 1) * v_tile, :].T.astype(vt16_ref.dtype)
    kw = ropek(mm(4096, LANES))
    kw_ref[...] = kw
    kw16_ref[...] = kw.astype(kw16_ref.dtype)


def _proj(x2d, norm_g, w_in_b, tabs, tm):
    m = x2d.shape[0]
    n_tab = tabs.shape[1] // tm
    row = lambda i: (i, 0)
    row3 = lambda i: (i, 0, 0)
    assert tm % V_TILE == 0 or m == tm, (tm, m)
    n_vt = max(tm // V_TILE, 1)
    v_tile = min(V_TILE, tm)
    outs = (
        jax.ShapeDtypeStruct((m, 4 * RET_WIDTH), F32),
        jax.ShapeDtypeStruct((m, ATT_WIDTH), F32),
        jax.ShapeDtypeStruct((m, IDX_HEADS * IDX_DH), F32),
        jax.ShapeDtypeStruct((m, LANES), F32),
        jax.ShapeDtypeStruct((m, ATT_WIDTH), F32),
        jax.ShapeDtypeStruct((m, ATT_WIDTH), F32),
        jax.ShapeDtypeStruct((m, ATT_WIDTH), BF16),
        jax.ShapeDtypeStruct((m // v_tile, ATT_WIDTH, v_tile), BF16),
        jax.ShapeDtypeStruct((m, LANES), BF16),
    )
    return pl.pallas_call(
        _proj_kernel,
        grid=(m // tm,),
        in_specs=[
            pl.BlockSpec((tm, D_MODEL), row),
            _resident((1, D_MODEL), lambda i: (0, 0)),
            _resident((D_MODEL, IN_WIDTH_PADDED), lambda i: (0, 0)),
            pl.BlockSpec((8, tm, LANES), lambda i: (0, i % n_tab, 0)),
        ],
        out_specs=[
            pl.BlockSpec((tm, 4 * RET_WIDTH), row),
            pl.BlockSpec((tm, ATT_WIDTH), row),
            pl.BlockSpec((tm, IDX_HEADS * IDX_DH), row),
            pl.BlockSpec((tm, LANES), row),
            pl.BlockSpec((tm, ATT_WIDTH), row),
            pl.BlockSpec((tm, ATT_WIDTH), row),
            pl.BlockSpec((tm, ATT_WIDTH), row),
            pl.BlockSpec((n_vt, ATT_WIDTH, v_tile), row3),
            pl.BlockSpec((tm, LANES), row),
        ],
        out_shape=outs,
        compiler_params=_compiler_params(("parallel",)),
        name="proj",
    )(x2d, norm_g, w_in_b, tabs)


def _retention_tables(blk):
    lg = jnp.log1p(-jnp.exp2(-5.0 - jnp.arange(RET_HEADS, dtype=F32)))
    i = jnp.arange(blk, dtype=F32)
    diff = i[:, None] - i[None, :]
    dmat = jnp.where(diff[None] >= 0, jnp.exp(lg[:, None, None] * jnp.maximum(diff, 0.0)[None]), 0.0)
    w_end = jnp.exp(lg[:, None] * (blk - 1.0 - i)[None, :])
    w_q = jnp.exp(lg[:, None] * (i + 1.0)[None, :])
    g_block = jnp.exp(lg * blk)
    lane = lambda a: jnp.broadcast_to(a[:, :, None], (RET_HEADS, blk, LANES))
    return dmat, lane(w_q), lane(w_end), g_block


def _retention_kernel(gblk_ref, q_ref, k_ref, v_ref, g_ref, s0_ref, dmat_ref, wq_ref, wend_ref, gain_ref,
                      y_ref, sfin_ref, s_scr):
    t = pl.program_id(1)

    @pl.when(t == 0)
    def _():
        s_scr[...] = s0_ref[0]

    for hd in range(RET_HEADS):
        cols = slice(hd * RET_DK, (hd + 1) * RET_DK)
        q = q_ref[:, cols]
        k = k_ref[:, cols] * RET_DK ** -0.5
        vb = v_ref[:, cols].astype(BF16)
        state = s_scr[hd]
        scores = lax.dot_general(q.astype(BF16), k.astype(BF16), _NT, preferred_element_type=F32) * dmat_ref[hd]
        o = jnp.dot(scores.astype(BF16), vb, preferred_element_type=F32)
        o = o + jnp.dot((q * wq_ref[hd]).astype(BF16), state.astype(BF16), preferred_element_type=F32)
        k_end_t = (k * wend_ref[hd]).T.astype(BF16)
        s_scr[hd] = gblk_ref[hd] * state + jnp.dot(k_end_t, vb, preferred_element_type=F32)
        mu = jnp.mean(o, axis=-1, keepdims=True)
        var = jnp.mean(jnp.square(o - mu), axis=-1, keepdims=True)
        on = ((o - mu) * lax.rsqrt(var + NORM_EPS)) * gain_ref[:, cols]
        g = g_ref[:, cols]
        y_ref[:, cols] = ((g * jax.nn.sigmoid(g)) * on).astype(y_ref.dtype)

    @pl.when(t == pl.num_programs(1) - 1)
    def _():
        sfin_ref[0] = s_scr[...]


def _retention(ret_in, s0, gn_gain, batch, seq, blk):
    nt = seq // blk
    dmat, w_q, w_end, g_block = _retention_tables(blk)

    def col(c):
        return pl.BlockSpec((blk, RET_WIDTH), lambda b, t: (b * nt + t, c))

    const3 = lambda b, t: (0, 0, 0)
    return pl.pallas_call(
        _retention_kernel,
        grid=(batch, nt),
        in_specs=[
            pl.BlockSpec(memory_space=pltpu.SMEM),
            col(0), col(1), col(2), col(3),
            pl.BlockSpec((1, RET_HEADS, RET_DK, RET_DV), lambda b, t: (b, 0, 0, 0)),
            pl.BlockSpec((RET_HEADS, blk, blk), const3),
            pl.BlockSpec((RET_HEADS, blk, LANES), const3),
            pl.BlockSpec((RET_HEADS, blk, LANES), const3),
            pl.BlockSpec((1, RET_WIDTH), lambda b, t: (0, 0)),
        ],
        out_specs=[
            pl.BlockSpec((blk, RET_WIDTH), lambda b, t: (b * nt + t, 0)),
            pl.BlockSpec((1, RET_HEADS, RET_DK, RET_DV), lambda b, t: (b, 0, 0, 0)),
        ],
        out_shape=(
            jax.ShapeDtypeStruct((batch * seq, RET_WIDTH), BF16),
            jax.ShapeDtypeStruct((batch, RET_HEADS, RET_DK, RET_DV), F32),
        ),
        scratch_shapes=[pltpu.VMEM((RET_HEADS, RET_DK, RET_DV), F32)],
        compiler_params=_compiler_params(("parallel", "arbitrary")),
        name="retention",
    )(g_block, ret_in, ret_in, ret_in, ret_in, s0, dmat, w_q, w_end, gn_gain)


COUNT_ROWS = 32
COUNT_UNROLL = 8
TILE_UNROLL = 4
MASKED = 2.0 * NEG
LOG2_E = 1.4426950408889634


def _ordered_bits_to_float(u):
    bits = jnp.where(u < 0, u ^ jnp.int32(-2 ** 31), ~u)
    return lax.bitcast_convert_type(bits, F32)


def _signed_half(x):
    return (x - 2 ** 15).astype(jnp.int16)


def _attention_kernel(lim_ref, qa_ref, qi_ref, wt_ref, ki_ref, ka_ref, vt_ref, o_ref,
                      s_scr, hi16_scr, lo16_scr, xa_scr, xb_scr, acc_scr,
                      *, qb, tk, n_keys, topk, causal, n_valid_q, index_bits):
    n_kt = pl.program_id(1) * (qb // tk) + (qb // tk) if causal else ki_ref.shape[0] // tk
    lim = lim_ref[0]
    lane = lax.broadcasted_iota(jnp.int32, (qb, LANES), 1)
    key_iota = lax.broadcasted_iota(jnp.int32, (tk, qb), 0)

    def head_operand(ref, hd, scale, shift_to_low):
        pair = ref[:, (hd // 2) * LANES:(hd // 2 + 1) * LANES] * scale
        if shift_to_low and hd % 2:
            pair = pltpu.roll(pair, 64, 1)
        keep = (lane < 64) if (shift_to_low or hd % 2 == 0) else (lane >= 64)
        return jnp.where(keep, pair, 0.0).T.astype(BF16)

    qi_heads = [head_operand(qi_ref, hd, 1.0, True) for hd in range(IDX_HEADS)]
    wt = wt_ref[0]

    def score_body(kt, carry, some_inadmissible):
        r0 = pl.multiple_of(kt * tk, tk)
        keys = ki_ref[pl.ds(r0, tk), :]
        acc = jnp.zeros((tk, qb), F32)
        for hd in range(IDX_HEADS):
            d = jnp.dot(keys, qi_heads[hd], preferred_element_type=F32)
            acc = acc + jnp.maximum(d, 0.0) * wt[hd:hd + 1, :]
        s = jnp.where(acc == 0.0, 0.0, acc)
        if some_inadmissible:
            s = jnp.where(key_iota + r0 < lim, s, -jnp.inf)
        s_scr[pl.ds(r0, tk), :] = s
        bits = lax.bitcast_convert_type(s, jnp.int32)
        ordered = bits ^ (lax.shift_right_arithmetic(bits, 31) | jnp.int32(-2 ** 31))
        hi16_scr[pl.ds(r0, tk), :] = _signed_half(lax.shift_right_logical(ordered, 16))
        lo16_scr[pl.ds(r0, tk), :] = _signed_half(ordered & 0xFFFF)
        return carry

    n_all_admissible = jnp.minimum(jnp.min(lim) // tk, n_kt)
    lax.fori_loop(0, n_all_admissible, functools.partial(score_body, some_inadmissible=False), 0)
    lax.fori_loop(n_all_admissible, n_kt, functools.partial(score_body, some_inadmissible=True), 0)

    def count(src_ref, indicator, dtype):
        def one_tile(kt, acc):
            for part in range(tk // COUNT_ROWS):
                r0 = pl.multiple_of(kt * tk + part * COUNT_ROWS, COUNT_ROWS)
                acc = acc + indicator(src_ref[pl.ds(r0, COUNT_ROWS), :], r0)
            return acc
        def tile_group(g, acc):
            for j in range(COUNT_UNROLL):
                acc = one_tile(COUNT_UNROLL * g + j, acc)
            return acc

        acc = lax.fori_loop(0, n_kt // COUNT_UNROLL, tile_group, jnp.zeros((COUNT_ROWS, qb), dtype))
        acc = lax.fori_loop(n_kt - n_kt % COUNT_UNROLL, n_kt, one_tile, acc)
        return jnp.sum(acc.astype(F32), axis=0, keepdims=True)

    n_masked = (n_keys - lim).astype(F32)
    kf = float(topk)
    q_lane = lax.broadcasted_iota(jnp.int32, (1, qb), 1)
    one16 = jnp.ones((), BF16)
    zero16 = jnp.zeros((), BF16)

    def bisect_step(it, carry, half_ref):
        cur, cnt = carry
        cand = cur | lax.shift_left(jnp.int32(1), 31 - it)
        if half_ref is hi16_scr:
            half = _signed_half(lax.shift_right_logical(cand, 16))
        else:
            half = _signed_half(cand & 0xFFFF)
        c = count(half_ref, lambda h, r0: jnp.where(h >= half, one16, zero16), BF16)
        ok = c + jnp.where(_ordered_bits_to_float(cand) <= NEG, n_masked, 0.0) >= kf
        return jnp.where(ok, cand, cur), jnp.where(ok, c, cnt)

    carry = (jnp.zeros((1, qb), jnp.int32), jnp.zeros((1, qb), F32))
    carry = lax.fori_loop(0, 16, lambda it, st: bisect_step(it, st, hi16_scr), carry)

    top = _signed_half(lax.shift_right_logical(carry[0], 16))

    def for_each_tile(body):
        def group(g, c):
            for j in range(TILE_UNROLL):
                body(TILE_UNROLL * g + j)
            return c

        def single(kt, c):
            body(kt)
            return c

        lax.fori_loop(0, n_kt // TILE_UNROLL, group, 0)
        lax.fori_loop(n_kt - n_kt % TILE_UNROLL, n_kt, single, 0)

    def narrow_body(kt):
        r0 = pl.multiple_of(kt * tk, tk)
        hi = hi16_scr[pl.ds(r0, tk), :]
        lo = lo16_scr[pl.ds(r0, tk), :]
        lo16_scr[pl.ds(r0, tk), :] = jnp.where(hi > top, jnp.int16(2 ** 15 - 1),
                                               jnp.where(hi == top, lo, jnp.int16(-2 ** 15)))

    for_each_tile(narrow_body)
    cur, cnt_ge = lax.fori_loop(16, 32, lambda it, st: bisect_step(it, st, lo16_scr), carry)
    theta = _ordered_bits_to_float(cur)

    overflow = jnp.where((cnt_ge > kf) & (q_lane < n_valid_q), 1.0, 0.0)
    all_idx = jnp.full((1, qb), 2 ** index_bits - 1, jnp.int32)

    def tie_cut():
        need = kf - count(s_scr, lambda s, r0: jnp.where(s > theta, 1.0, 0.0), F32)

        def mark_body(kt):
            r0 = pl.multiple_of(kt * tk, tk)
            tied = jnp.where(s_scr[pl.ds(r0, tk), :] == theta, key_iota + r0, 2 ** 15 - 1)
            hi16_scr[pl.ds(r0, tk), :] = tied.astype(jnp.int16)

        for_each_tile(mark_body)

        def body(it, cut):
            cand = cut | lax.shift_left(jnp.int32(1), index_bits - 1 - it)
            cand16 = cand.astype(jnp.int16)
            c = count(hi16_scr, lambda idx, r0: jnp.where(idx < cand16, one16, zero16), BF16)
            return jnp.where(c <= need, cand, cut)

        return lax.fori_loop(0, index_bits, body, jnp.zeros((1, qb), jnp.int32))

    any_overflow = jnp.max(overflow) > 0.0
    idx_cut = lax.cond(any_overflow, tie_cut, lambda: all_idx)

    def bias_body(kt, with_ties):
        r0 = pl.multiple_of(kt * tk, tk)
        s = s_scr[pl.ds(r0, tk), :]
        if with_ties:
            tie = jnp.where(key_iota + r0 < idx_cut, 0.0, MASKED)
            s_scr[pl.ds(r0, tk), :] = jnp.where(s > theta, 0.0, jnp.where(s == theta, tie, MASKED))
        else:
            s_scr[pl.ds(r0, tk), :] = jnp.where(s >= theta, 0.0, MASKED)

    lax.cond(any_overflow,
             lambda: for_each_tile(functools.partial(bias_body, with_ties=True)),
             lambda: for_each_tile(functools.partial(bias_body, with_ties=False)))

    qa_heads = [head_operand(qa_ref, hd, ATT_DH ** -0.5 * LOG2_E, False) for hd in range(ATT_HEADS)]
    acc_scr[...] = jnp.zeros(acc_scr.shape, F32)

    def logits_stage(kt, x_ref):
        r0 = pl.multiple_of(kt * tk, tk)
        tops = []
        for hd in range(ATT_HEADS):
            keys = ka_ref[pl.ds(r0, tk), (hd // 2) * LANES:(hd // 2 + 1) * LANES]
            x = jnp.dot(keys, qa_heads[hd], preferred_element_type=F32) + s_scr[pl.ds(r0, tk), :]
            x_ref[hd] = x.astype(x_ref.dtype)
            tops.append(jnp.max(x, axis=0, keepdims=True))
        return jnp.concatenate(tops, axis=0).astype(x_ref.dtype).astype(F32)

    def accumulate(kt_in, x_ref, tile_max, m_run, l_run):
        m_new = jnp.maximum(m_run, tile_max)
        alpha = jnp.exp2(m_run - m_new)
        m_staged = m_new.astype(x_ref.dtype)
        sums = []
        for hd in range(ATT_HEADS):
            p = jnp.exp2(x_ref[hd] - m_staged[hd:hd + 1, :]).astype(BF16)
            sums.append(jnp.sum(p.astype(F32), axis=0, keepdims=True))
            rows = slice(hd * ATT_DH, (hd + 1) * ATT_DH)
            pv = jnp.dot(vt_ref[kt_in, rows, :], p, preferred_element_type=F32)
            acc_scr[rows, :] = alpha[hd:hd + 1, :] * acc_scr[rows, :] + pv
        return m_new, alpha * l_run + jnp.concatenate(sums, axis=0)

    def pair_body(pair, carry, stage_next=True):
        max_a, m_run, l_run = carry
        max_b = logits_stage(2 * pair + 1, xb_scr)
        m_run, l_run = accumulate(2 * pair, xa_scr, max_a, m_run, l_run)
        if stage_next:
            max_a = logits_stage(2 * pair + 2, xa_scr)
        m_run, l_run = accumulate(2 * pair + 1, xb_scr, max_b, m_run, l_run)
        return max_a, m_run, l_run

    n_pairs = (n_kt + 1) // 2
    first_max = logits_stage(0, xa_scr)
    carry = lax.fori_loop(
        0, n_pairs - 1, pair_body,
        (first_max, jnp.full((ATT_HEADS, qb), NEG, F32), jnp.zeros((ATT_HEADS, qb), F32)))
    l_fin = lax.cond(
        n_kt % 2 == 0,
        lambda c: pair_body(n_pairs - 1, c, stage_next=False)[2],
        lambda c: accumulate(2 * (n_pairs - 1), xa_scr, *c)[1],
        carry)

    for hd in range(ATT_HEADS):
        rows = slice(hd * ATT_DH, (hd + 1) * ATT_DH)
        acc_scr[rows, :] = acc_scr[rows, :] / l_fin[hd:hd + 1, :]
    o_ref[...] = acc_scr[...].T.astype(o_ref.dtype)


def _attention(q_a, q_i, w_t, lim, k_i, k_a, v_t, *, qb, tk, n_keys, topk, causal, n_valid_q):
    batch, tq, _ = q_a.shape
    lp = k_i.shape[1]
    nq = tq // qb
    index_bits = int(np.ceil(np.log2(lp + 1)))
    assert lp // COUNT_ROWS <= 256, "bfloat16 partial counts are exact only up to 256"
    kernel = functools.partial(_attention_kernel, qb=qb, tk=tk, n_keys=n_keys, topk=topk, causal=causal,
                               n_valid_q=n_valid_q, index_bits=index_bits)
    return pl.pallas_call(
        kernel,
        grid=(batch, nq),
        in_specs=[
            pl.BlockSpec((1, 1, qb), lambda b, i: (i, 0, 0)),
            pl.BlockSpec((None, qb, ATT_WIDTH), lambda b, i: (b, i, 0)),
            pl.BlockSpec((None, qb, IDX_HEADS * IDX_DH), lambda b, i: (b, i, 0)),
            pl.BlockSpec((1, IDX_HEADS, qb), lambda b, i: (b, 0, i)),
            _resident((None, lp, LANES), lambda b, i: (b, 0, 0)),
            _resident((None, lp, ATT_WIDTH), lambda b, i: (b, 0, 0)),
            _resident((None, lp // tk, ATT_WIDTH, tk), lambda b, i: (b, 0, 0, 0)),
        ],
        out_specs=pl.BlockSpec((None, qb, ATT_WIDTH), lambda b, i: (b, i, 0)),
        out_shape=jax.ShapeDtypeStruct((batch, tq, ATT_WIDTH), BF16),
        scratch_shapes=[
            pltpu.VMEM((lp, qb), F32),
            pltpu.VMEM((lp, qb), jnp.int16),
            pltpu.VMEM((lp, qb), jnp.int16),
            pltpu.VMEM((ATT_HEADS, tk, qb), BF16),
            pltpu.VMEM((ATT_HEADS, tk, qb), BF16),
            pltpu.VMEM((ATT_WIDTH, qb), F32),
        ],
        compiler_params=_compiler_params(("parallel", "arbitrary")),
        name="attention",
    )(lim, q_a, q_i, w_t, k_i, k_a, v_t)


FF_CHUNK = 512


def _mlp_kernel(x_ref, yr_ref, ya_ref, wo_ref, g2_ref, wup_ref, wdn_ref, gf_ref, y_ref, x_scr, h_scr, a_scr):
    x = x_ref[...] + jnp.dot(yr_ref[...], wo_ref[:RET_WIDTH, :], preferred_element_type=F32)
    x = x + jnp.dot(ya_ref[...], wo_ref[RET_WIDTH:, :], preferred_element_type=F32)
    x_scr[...] = x
    ms = jnp.mean(x * x, axis=-1, keepdims=True)
    h_scr[...] = ((x * lax.rsqrt(ms + NORM_EPS)) * g2_ref[...]).astype(h_scr.dtype)
    for c in range(D_FF // FF_CHUNK):
        cols = slice(c * FF_CHUNK, (c + 1) * FF_CHUNK)
        u = jnp.dot(h_scr[...], wup_ref[:, cols], preferred_element_type=F32)
        a_scr[:, cols] = jnp.square(jnp.maximum(u, 0.0)).astype(a_scr.dtype)
    x = x_scr[...] + jnp.dot(a_scr[...], wdn_ref[...], preferred_element_type=F32)
    ms = jnp.mean(x * x, axis=-1, keepdims=True)
    y_ref[...] = (x * lax.rsqrt(ms + NORM_EPS)) * gf_ref[...]


def _mlp(x2d, y_r, y_a, w_out_b, norm2, w_up_b, w_down_b, norm_final, tm):
    m = x2d.shape[0]
    row = lambda i: (i, 0)
    const = lambda i: (0, 0)
    return pl.pallas_call(
        _mlp_kernel,
        grid=(m // tm,),
        in_specs=[
            pl.BlockSpec((tm, D_MODEL), row),
            pl.BlockSpec((tm, RET_WIDTH), row),
            pl.BlockSpec((tm, ATT_WIDTH), row),
            _resident((D_MODEL, D_MODEL), const),
            _resident((1, D_MODEL), const),
            _resident((D_MODEL, D_FF), const),
            _resident((D_FF, D_MODEL), const),
            _resident((1, D_MODEL), const),
        ],
        out_specs=pl.BlockSpec((tm, D_MODEL), row),
        out_shape=jax.ShapeDtypeStruct((m, D_MODEL), F32),
        scratch_shapes=[pltpu.VMEM((tm, D_MODEL), F32), pltpu.VMEM((tm, D_MODEL), BF16),
                        pltpu.VMEM((tm, D_FF), BF16)],
        compiler_params=_compiler_params(("parallel",)),
        name="mlp",
    )(x2d, y_r, y_a, w_out_b, norm2, w_up_b, w_down_b, norm_final)


def _pick_tile(n, target):
    t = min(n, target)
    while n % t:
        t //= 2
    return t


def _layer(x, pos, past, s0, ret_blk, weights, *, attn_qb, attn_tk):
    norm1, w_in_b, gn_gain, w_out_b, norm2, w_up_b, w_down_b, norm_final = weights
    batch, seq, _ = x.shape
    m = batch * seq
    x2d = x.reshape(m, D_MODEL)
    tm = _pick_tile(m, ROW_TILE)

    tabs = _rope_tables(pos)
    if seq < tm:
        tabs = jnp.tile(tabs, (1, tm // seq, 1))
    ret_in, q_a, q_i, kw, k_a, v_a, ka16, vt16, kw16 = _proj(x2d, norm1, w_in_b, tabs, tm)

    y_r, s_new = _retention(ret_in, s0, gn_gain, batch, seq, ret_blk)

    three = lambda a: a.reshape(batch, seq, -1)
    k_i = three(kw[:, :IDX_DH])
    w_t = jnp.swapaxes(three(kw[:, IDX_DH:IDX_DH + IDX_HEADS]) * IDX_HEADS ** -0.5 * IDX_DH ** -0.5, 1, 2)
    q_a3, q_i3 = three(q_a), three(q_i)
    n_keys = seq if past is None else past[0].shape[1] + seq
    topk = min(TOPK_MAX, n_keys // 4)
    tq = -(-seq // attn_qb) * attn_qb
    lp = -(-n_keys // attn_tk) * attn_tk
    padq = lambda a: jnp.pad(a, ((0, 0), (0, tq - seq), (0, 0)))
    if past is None and attn_tk == vt16.shape[2] and lp == n_keys:
        keys_i, keys_a = three(kw16), three(ka16)
        v_t = vt16.reshape(batch, lp // attn_tk, ATT_WIDTH, attn_tk)
        n_valid_q = attn_qb
    else:
        parts_i, parts_a, parts_v = [three(kw16)], [three(ka16)], [three(v_a).astype(BF16)]
        n_valid_q = attn_qb
        if past is not None:
            past_k, past_v, past_ki = past
            parts_i.insert(0, jnp.pad(past_ki, ((0, 0), (0, 0), (0, LANES - IDX_DH))).astype(BF16))
            parts_a.insert(0, past_k.reshape(batch, -1, ATT_WIDTH).astype(BF16))
            parts_v.insert(0, past_v.reshape(batch, -1, ATT_WIDTH).astype(BF16))
            n_valid_q = seq
        tail = lambda width: [jnp.zeros((batch, lp - n_keys, width), BF16)] if lp > n_keys else []
        keys_i = jnp.concatenate(parts_i + tail(LANES), axis=1)
        keys_a = jnp.concatenate(parts_a + tail(ATT_WIDTH), axis=1)
        vals = jnp.concatenate(parts_v + tail(ATT_WIDTH), axis=1)
        v_t = jnp.swapaxes(vals.reshape(batch, lp // attn_tk, attn_tk, ATT_WIDTH), 2, 3)
    lim = jnp.minimum((pos // CHUNK + 1) * CHUNK, n_keys).astype(jnp.int32)
    lim = jnp.pad(lim, (0, tq - seq), constant_values=n_keys).reshape(tq // attn_qb, 1, attn_qb)
    y_a = _attention(
        padq(q_a3), padq(q_i3), jnp.pad(w_t, ((0, 0), (0, 0), (0, tq - seq))), lim, keys_i, keys_a, v_t,
        qb=attn_qb, tk=attn_tk, n_keys=n_keys, topk=topk, causal=past is None, n_valid_q=n_valid_q)
    y_a = y_a[:, :seq].reshape(m, ATT_WIDTH)

    y = _mlp(x2d, y_r, y_a, w_out_b, norm2, w_up_b, w_down_b, norm_final, tm)
    return (y.reshape(batch, seq, D_MODEL),
            k_a.reshape(batch, seq, ATT_HEADS, ATT_DH),
            v_a.reshape(batch, seq, ATT_HEADS, ATT_DH),
            k_i, s_new)


def kernel(x_prompt, x_sample, cache_k_att, cache_v_att, cache_k_idx, state_ret, norm1, w_in, gn_gain, w_out,
           norm2, w_up, w_down, norm_final):
    depth = norm1.shape[0]
    assert depth == 1, "the final norm is fused into the layer's last kernel"
    past_len = cache_k_att.shape[2]
    pos_p = jnp.arange(x_prompt.shape[1], dtype=jnp.int32)
    pos_s = past_len + jnp.arange(x_sample.shape[1], dtype=jnp.int32)
    l = 0
    w_in_b = jnp.pad(w_in[l], ((0, 0), (0, IN_WIDTH_PADDED - IN_WIDTH))).astype(BF16)
    weights = (norm1[l][None], w_in_b, gn_gain[l][None], w_out[l].astype(BF16), norm2[l][None],
               w_up[l].astype(BF16), w_down[l].astype(BF16), norm_final[None])
    s0_p = jnp.zeros((x_prompt.shape[0], RET_HEADS, RET_DK, RET_DV), F32)
    yp, kap, vap, kip, sp = _layer(x_prompt, pos_p, None, s0_p, PROMPT_RETENTION_BLOCK, weights,
                                   attn_qb=PROMPT_QUERY_BLOCK, attn_tk=KEY_TILE)
    past = (cache_k_att[l], cache_v_att[l], cache_k_idx[l])
    ys, kas, vas, kis, ss = _layer(x_sample, pos_s, past, state_ret[l].astype(F32), x_sample.shape[1], weights,
                                   attn_qb=SAMPLE_QUERY_BLOCK, attn_tk=KEY_TILE)
    return (yp, ys, kap[None], vap[None], kip[None], sp[None], kas[None], vas[None], kis[None], ss[None])
```

```python
import functools

import jax
import jax.numpy as jnp
import numpy as np
from jax import lax
from jax.experimental import pallas as pl
from jax.experimental.pallas import tpu as pltpu

D_MODEL = 1024
CHUNK = 64
ROPE_THETA = 10000.0
NORM_EPS = 1e-6
NEG = -1e30
RET_WIDTH = 512
ATT_WIDTH = 512
RET_HEADS = 4
RET_DK = 128
RET_DV = 128
ATT_HEADS = 8
ATT_DH = 64
IDX_HEADS = 8
IDX_DH = 64
TOPK_MAX = 256
D_FF = 4 * D_MODEL
IN_WIDTH = 4 * RET_WIDTH + 3 * ATT_WIDTH + IDX_HEADS * IDX_DH + IDX_DH + IDX_HEADS

LANES = 128
IN_WIDTH_PADDED = ((IN_WIDTH + LANES - 1) // LANES) * LANES
VMEM_LIMIT_BYTES = 56 * 1024 * 1024

ROW_TILE = 512
KEY_TILE = 256
PROMPT_QUERY_BLOCK = 256
SAMPLE_QUERY_BLOCK = 128
PROMPT_RETENTION_BLOCK = 256

F32 = jnp.float32
BF16 = jnp.bfloat16

_NT = (((1,), (1,)), ((), ()))


def _compiler_params(semantics):
    return pltpu.CompilerParams(dimension_semantics=semantics, vmem_limit_bytes=VMEM_LIMIT_BYTES)


def _resident(block_shape, index_map):
    return pl.BlockSpec(block_shape, index_map, pipeline_mode=pl.Buffered(1))


def _rope_tables(pos):
    posf = pos.astype(F32)[:, None]

    def angles(d):
        half = d // 2
        inv = ROPE_THETA ** (-jnp.arange(half, dtype=F32) * 2.0 / d)
        ang = posf * inv[None, :]
        return jnp.cos(ang), jnp.sin(ang)

    c128, s128 = angles(128)
    zero64 = jnp.zeros_like(s128)
    cos128 = jnp.concatenate([c128, c128], axis=1)
    sin128 = jnp.concatenate([-s128, s128], axis=1)
    c64, s64 = angles(64)
    zero32 = jnp.zeros_like(s64)
    cos64 = jnp.concatenate([c64, c64, c64, c64], axis=1)
    sin64_up = jnp.concatenate([zero32, s64, zero32, s64], axis=1)
    sin64_dn = jnp.concatenate([-s64, zero32, -s64, zero32], axis=1)
    one64 = jnp.ones_like(zero64)
    cosk = jnp.concatenate([c64, c64, one64], axis=1)
    sink_up = jnp.concatenate([zero32, s64, zero64], axis=1)
    sink_dn = jnp.concatenate([-s64, zero32, zero64], axis=1)
    return jnp.stack([cos128, sin128, cos64, sin64_up, sin64_dn, cosk, sink_up, sink_dn])


V_TILE = KEY_TILE
V_ROWS = ATT_DH + 16


def _proj_kernel(x_ref, g_ref, w_ref, tab_ref, ret_ref, qa_ref, qi_ref, kw_ref, ka_ref, va_ref,
                 ka16_ref, vt16_ref, kw16_ref):
    x = x_ref[...]
    ms = jnp.mean(x * x, axis=-1, keepdims=True)
    h = ((x * lax.rsqrt(ms + NORM_EPS)) * g_ref[...]).astype(BF16)

    def mm(c0, n):
        return jnp.dot(h, w_ref[:, c0:c0 + n], preferred_element_type=F32)

    def rope128(z):
        return z * tab_ref[0] + pltpu.roll(z, 64, 1) * tab_ref[1]

    def rope64(z):
        return z * tab_ref[2] + pltpu.roll(z, 32, 1) * tab_ref[3] + pltpu.roll(z, 96, 1) * tab_ref[4]

    def ropek(z):
        return z * tab_ref[5] + pltpu.roll(z, 32, 1) * tab_ref[6] + pltpu.roll(z, 96, 1) * tab_ref[7]

    def emit(out_ref, out_c0, w_c0, rope):
        z = mm(w_c0, 512)
        for t in range(4):
            zt = z[:, t * LANES:(t + 1) * LANES]
            if rope is not None:
                zt = rope(zt)
            out_ref[:, out_c0 + t * LANES:out_c0 + (t + 1) * LANES] = zt.astype(out_ref.dtype)

    emit(ret_ref, 0, 0, rope128)
    emit(ret_ref, 512, 512, rope128)
    emit(ret_ref, 1024, 1024, None)
    emit(ret_ref, 1536, 1536, None)
    emit(qa_ref, 0, 2048, rope64)
    emit(qi_ref, 0, 3584, rope64)
    k_a = mm(2560, 512)
    k_a = jnp.concatenate([rope64(k_a[:, t * LANES:(t + 1) * LANES]) for t in range(4)], axis=1)
    ka_ref[...] = k_a
    ka16_ref[...] = k_a.astype(ka16_ref.dtype)
    v_a = mm(3072, 512)
    va_ref[...] = v_a
    v_tile = vt16_ref.shape[2]
    ones_rows = jnp.where(lax.broadcasted_iota(jnp.int32, (V_ROWS - ATT_DH, v_tile), 0) == 0, 1.0, 0.0)
    for part in range(vt16_ref.shape[0]):
        v_t = v_a[part * v_tile:(part + 1) * v_tile, :].T.astype(vt16_ref.dtype)
        for hd in range(ATT_HEADS):
            vt16_ref[part, hd * V_ROWS:hd * V_ROWS + ATT_DH, :] = v_t[hd * ATT_DH:(hd + 1) * ATT_DH, :]
            vt16_ref[part, hd * V_ROWS + ATT_DH:(hd + 1) * V_ROWS, :] = ones_rows.astype(vt16_ref.dtype)
    kw = ropek(mm(4096, LANES))
    kw_ref[...] = kw
    kw16_ref[...] = kw.astype(kw16_ref.dtype)


def _proj(x2d, norm_g, w_in_b, tabs, tm):
    m = x2d.shape[0]
    n_tab = tabs.shape[1] // tm
    row = lambda i: (i, 0)
    row3 = lambda i: (i, 0, 0)
    assert tm % V_TILE == 0 or m == tm, (tm, m)
    n_vt = max(tm // V_TILE, 1)
    v_tile = min(V_TILE, tm)
    outs = (
        jax.ShapeDtypeStruct((m, 4 * RET_WIDTH), F32),
        jax.ShapeDtypeStruct((m, ATT_WIDTH), F32),
        jax.ShapeDtypeStruct((m, IDX_HEADS * IDX_DH), F32),
        jax.ShapeDtypeStruct((m, LANES), F32),
        jax.ShapeDtypeStruct((m, ATT_WIDTH), F32),
        jax.ShapeDtypeStruct((m, ATT_WIDTH), F32),
        jax.ShapeDtypeStruct((m, ATT_WIDTH), BF16),
        jax.ShapeDtypeStruct((m // v_tile, ATT_HEADS * V_ROWS, v_tile), BF16),
        jax.ShapeDtypeStruct((m, LANES), BF16),
    )
    return pl.pallas_call(
        _proj_kernel,
        grid=(m // tm,),
        in_specs=[
            pl.BlockSpec((tm, D_MODEL), row),
            _resident((1, D_MODEL), lambda i: (0, 0)),
            _resident((D_MODEL, IN_WIDTH_PADDED), lambda i: (0, 0)),
            pl.BlockSpec((8, tm, LANES), lambda i: (0, i % n_tab, 0)),
        ],
        out_specs=[
            pl.BlockSpec((tm, 4 * RET_WIDTH), row),
            pl.BlockSpec((tm, ATT_WIDTH), row),
            pl.BlockSpec((tm, IDX_HEADS * IDX_DH), row),
            pl.BlockSpec((tm, LANES), row),
            pl.BlockSpec((tm, ATT_WIDTH), row),
            pl.BlockSpec((tm, ATT_WIDTH), row),
            pl.BlockSpec((tm, ATT_WIDTH), row),
            pl.BlockSpec((n_vt, ATT_HEADS * V_ROWS, v_tile), row3),
            pl.BlockSpec((tm, LANES), row),
        ],
        out_shape=outs,
        compiler_params=_compiler_params(("parallel",)),
        name="proj",
    )(x2d, norm_g, w_in_b, tabs)


def _retention_tables(blk):
    lg = jnp.log1p(-jnp.exp2(-5.0 - jnp.arange(RET_HEADS, dtype=F32)))
    i = jnp.arange(blk, dtype=F32)
    diff = i[:, None] - i[None, :]
    dmat = jnp.where(diff[None] >= 0, jnp.exp(lg[:, None, None] * jnp.maximum(diff, 0.0)[None]), 0.0)
    w_end = jnp.exp(lg[:, None] * (blk - 1.0 - i)[None, :])
    w_q = jnp.exp(lg[:, None] * (i + 1.0)[None, :])
    g_block = jnp.exp(lg * blk)
    lane = lambda a: jnp.broadcast_to(a[:, :, None], (RET_HEADS, blk, LANES))
    return dmat, lane(w_q), lane(w_end), g_block


def _retention_kernel(gblk_ref, q_ref, k_ref, v_ref, g_ref, s0_ref, dmat_ref, wq_ref, wend_ref, gain_ref,
                      y_ref, sfin_ref, s_scr):
    t = pl.program_id(1)

    @pl.when(t == 0)
    def _():
        s_scr[...] = s0_ref[0]

    for hd in range(RET_HEADS):
        cols = slice(hd * RET_DK, (hd + 1) * RET_DK)
        q = q_ref[:, cols]
        k = k_ref[:, cols] * RET_DK ** -0.5
        vb = v_ref[:, cols].astype(BF16)
        state = s_scr[hd]
        scores = lax.dot_general(q.astype(BF16), k.astype(BF16), _NT, preferred_element_type=F32) * dmat_ref[hd]
        o = jnp.dot(scores.astype(BF16), vb, preferred_element_type=F32)
        o = o + jnp.dot((q * wq_ref[hd]).astype(BF16), state.astype(BF16), preferred_element_type=F32)
        k_end_t = (k * wend_ref[hd]).T.astype(BF16)
        s_scr[hd] = gblk_ref[hd] * state + jnp.dot(k_end_t, vb, preferred_element_type=F32)
        mu = jnp.mean(o, axis=-1, keepdims=True)
        var = jnp.mean(jnp.square(o - mu), axis=-1, keepdims=True)
        on = ((o - mu) * lax.rsqrt(var + NORM_EPS)) * gain_ref[:, cols]
        g = g_ref[:, cols]
        y_ref[:, cols] = ((g * jax.nn.sigmoid(g)) * on).astype(y_ref.dtype)

    @pl.when(t == pl.num_programs(1) - 1)
    def _():
        sfin_ref[0] = s_scr[...]


def _retention(ret_in, s0, gn_gain, batch, seq, blk):
    nt = seq // blk
    dmat, w_q, w_end, g_block = _retention_tables(blk)

    def col(c):
        return pl.BlockSpec((blk, RET_WIDTH), lambda b, t: (b * nt + t, c))

    const3 = lambda b, t: (0, 0, 0)
    return pl.pallas_call(
        _retention_kernel,
        grid=(batch, nt),
        in_specs=[
            pl.BlockSpec(memory_space=pltpu.SMEM),
            col(0), col(1), col(2), col(3),
            pl.BlockSpec((1, RET_HEADS, RET_DK, RET_DV), lambda b, t: (b, 0, 0, 0)),
            pl.BlockSpec((RET_HEADS, blk, blk), const3),
            pl.BlockSpec((RET_HEADS, blk, LANES), const3),
            pl.BlockSpec((RET_HEADS, blk, LANES), const3),
            pl.BlockSpec((1, RET_WIDTH), lambda b, t: (0, 0)),
        ],
        out_specs=[
            pl.BlockSpec((blk, RET_WIDTH), lambda b, t: (b * nt + t, 0)),
            pl.BlockSpec((1, RET_HEADS, RET_DK, RET_DV), lambda b, t: (b, 0, 0, 0)),
        ],
        out_shape=(
            jax.ShapeDtypeStruct((batch * seq, RET_WIDTH), BF16),
            jax.ShapeDtypeStruct((batch, RET_HEADS, RET_DK, RET_DV), F32),
        ),
        scratch_shapes=[pltpu.VMEM((RET_HEADS, RET_DK, RET_DV), F32)],
        compiler_params=_compiler_params(("parallel", "arbitrary")),
        name="retention",
    )(g_block, ret_in, ret_in, ret_in, ret_in, s0, dmat, w_q, w_end, gn_gain)


COUNT_ROWS = 32
COUNT_UNROLL = 8
TILE_UNROLL = 4
MASKED = 2.0 * NEG
LOG2_E = 1.4426950408889634


def _ordered_bits_to_float(u):
    bits = jnp.where(u < 0, u ^ jnp.int32(-2 ** 31), ~u)
    return lax.bitcast_convert_type(bits, F32)


def _signed_half(x):
    return (x - 2 ** 15).astype(jnp.int16)


def _attention_kernel(lim_ref, qa_ref, qi_ref, wt_ref, ki_ref, ka_ref, vt_ref, o_ref,
                      s_scr, hi16_scr, lo16_scr, xa_scr, xb_scr, acc_scr, out_scr,
                      *, qb, tk, n_keys, topk, causal, n_valid_q, index_bits):
    n_kt = pl.program_id(1) * (qb // tk) + (qb // tk) if causal else ki_ref.shape[0] // tk
    lim = lim_ref[0]
    lane = lax.broadcasted_iota(jnp.int32, (qb, LANES), 1)
    key_iota = lax.broadcasted_iota(jnp.int32, (tk, qb), 0)

    def head_operand(ref, hd, scale, shift_to_low):
        pair = ref[:, (hd // 2) * LANES:(hd // 2 + 1) * LANES] * scale
        if shift_to_low and hd % 2:
            pair = pltpu.roll(pair, 64, 1)
        keep = (lane < 64) if (shift_to_low or hd % 2 == 0) else (lane >= 64)
        return jnp.where(keep, pair, 0.0).T.astype(BF16)

    qi_heads = [head_operand(qi_ref, hd, 1.0, True) for hd in range(IDX_HEADS)]
    wt = wt_ref[0]

    def score_body(kt, carry, some_inadmissible):
        r0 = pl.multiple_of(kt * tk, tk)
        keys = ki_ref[pl.ds(r0, tk), :]
        acc = jnp.zeros((tk, qb), F32)
        for hd in range(IDX_HEADS):
            d = jnp.dot(keys, qi_heads[hd], preferred_element_type=F32)
            acc = acc + jnp.maximum(d, 0.0) * wt[hd:hd + 1, :]
        s = jnp.where(acc == 0.0, 0.0, acc)
        if some_inadmissible:
            s = jnp.where(key_iota + r0 < lim, s, -jnp.inf)
        s_scr[pl.ds(r0, tk), :] = s
        bits = lax.bitcast_convert_type(s, jnp.int32)
        ordered = bits ^ (lax.shift_right_arithmetic(bits, 31) | jnp.int32(-2 ** 31))
        hi16_scr[pl.ds(r0, tk), :] = _signed_half(lax.shift_right_logical(ordered, 16))
        lo16_scr[pl.ds(r0, tk), :] = _signed_half(ordered & 0xFFFF)
        return carry

    n_all_admissible = jnp.minimum(jnp.min(lim) // tk, n_kt)
    lax.fori_loop(0, n_all_admissible, functools.partial(score_body, some_inadmissible=False), 0)
    lax.fori_loop(n_all_admissible, n_kt, functools.partial(score_body, some_inadmissible=True), 0)

    def count(src_ref, indicator, dtype):
        def one_tile(kt, acc):
            for part in range(tk // COUNT_ROWS):
                r0 = pl.multiple_of(kt * tk + part * COUNT_ROWS, COUNT_ROWS)
                acc = acc + indicator(src_ref[pl.ds(r0, COUNT_ROWS), :], r0)
            return acc
        def tile_group(g, acc):
            for j in range(COUNT_UNROLL):
                acc = one_tile(COUNT_UNROLL * g + j, acc)
            return acc

        acc = lax.fori_loop(0, n_kt // COUNT_UNROLL, tile_group, jnp.zeros((COUNT_ROWS, qb), dtype))
        acc = lax.fori_loop(n_kt - n_kt % COUNT_UNROLL, n_kt, one_tile, acc)
        return jnp.sum(acc.astype(F32), axis=0, keepdims=True)

    n_masked = (n_keys - lim).astype(F32)
    kf = float(topk)
    q_lane = lax.broadcasted_iota(jnp.int32, (1, qb), 1)
    one16 = jnp.ones((), BF16)
    zero16 = jnp.zeros((), BF16)

    def bisect_step(it, carry, half_ref):
        cur, cnt = carry
        cand = cur | lax.shift_left(jnp.int32(1), 31 - it)
        if half_ref is hi16_scr:
            half = _signed_half(lax.shift_right_logical(cand, 16))
        else:
            half = _signed_half(cand & 0xFFFF)
        c = count(half_ref, lambda h, r0: jnp.where(h >= half, one16, zero16), BF16)
        ok = c + jnp.where(_ordered_bits_to_float(cand) <= NEG, n_masked, 0.0) >= kf
        return jnp.where(ok, cand, cur), jnp.where(ok, c, cnt)

    carry = (jnp.zeros((1, qb), jnp.int32), jnp.zeros((1, qb), F32))
    carry = lax.fori_loop(0, 16, lambda it, st: bisect_step(it, st, hi16_scr), carry)

    top = _signed_half(lax.shift_right_logical(carry[0], 16))

    def for_each_tile(body):
        def group(g, c):
            for j in range(TILE_UNROLL):
                body(TILE_UNROLL * g + j)
            return c

        def single(kt, c):
            body(kt)
            return c

        lax.fori_loop(0, n_kt // TILE_UNROLL, group, 0)
        lax.fori_loop(n_kt - n_kt % TILE_UNROLL, n_kt, single, 0)

    def narrow_body(kt):
        r0 = pl.multiple_of(kt * tk, tk)
        hi = hi16_scr[pl.ds(r0, tk), :]
        lo = lo16_scr[pl.ds(r0, tk), :]
        lo16_scr[pl.ds(r0, tk), :] = jnp.where(hi > top, jnp.int16(2 ** 15 - 1),
                                               jnp.where(hi == top, lo, jnp.int16(-2 ** 15)))

    for_each_tile(narrow_body)
    cur, cnt_ge = lax.fori_loop(16, 32, lambda it, st: bisect_step(it, st, lo16_scr), carry)
    theta = _ordered_bits_to_float(cur)

    overflow = jnp.where((cnt_ge > kf) & (q_lane < n_valid_q), 1.0, 0.0)
    all_idx = jnp.full((1, qb), 2 ** index_bits - 1, jnp.int32)

    def tie_cut():
        need = kf - count(s_scr, lambda s, r0: jnp.where(s > theta, 1.0, 0.0), F32)

        def mark_body(kt):
            r0 = pl.multiple_of(kt * tk, tk)
            tied = jnp.where(s_scr[pl.ds(r0, tk), :] == theta, key_iota + r0, 2 ** 15 - 1)
            hi16_scr[pl.ds(r0, tk), :] = tied.astype(jnp.int16)

        for_each_tile(mark_body)

        def body(it, cut):
            cand = cut | lax.shift_left(jnp.int32(1), index_bits - 1 - it)
            cand16 = cand.astype(jnp.int16)
            c = count(hi16_scr, lambda idx, r0: jnp.where(idx < cand16, one16, zero16), BF16)
            return jnp.where(c <= need, cand, cut)

        return lax.fori_loop(0, index_bits, body, jnp.zeros((1, qb), jnp.int32))

    any_overflow = jnp.max(overflow) > 0.0
    idx_cut = lax.cond(any_overflow, tie_cut, lambda: all_idx)

    def bias_body(kt, with_ties):
        r0 = pl.multiple_of(kt * tk, tk)
        s = s_scr[pl.ds(r0, tk), :]
        if with_ties:
            tie = jnp.where(key_iota + r0 < idx_cut, 0.0, MASKED)
            s_scr[pl.ds(r0, tk), :] = jnp.where(s > theta, 0.0, jnp.where(s == theta, tie, MASKED))
        else:
            s_scr[pl.ds(r0, tk), :] = jnp.where(s >= theta, 0.0, MASKED)

    lax.cond(any_overflow,
             lambda: for_each_tile(functools.partial(bias_body, with_ties=True)),
             lambda: for_each_tile(functools.partial(bias_body, with_ties=False)))

    qa_heads = [head_operand(qa_ref, hd, ATT_DH ** -0.5 * LOG2_E, False) for hd in range(ATT_HEADS)]
    acc_scr[...] = jnp.zeros(acc_scr.shape, F32)

    def logits_stage(kt, x_ref):
        r0 = pl.multiple_of(kt * tk, tk)
        tops = []
        for hd in range(ATT_HEADS):
            keys = ka_ref[pl.ds(r0, tk), (hd // 2) * LANES:(hd // 2 + 1) * LANES]
            x = jnp.dot(keys, qa_heads[hd], preferred_element_type=F32) + s_scr[pl.ds(r0, tk), :]
            x_ref[hd] = x.astype(x_ref.dtype)
            tops.append(jnp.max(x, axis=0, keepdims=True))
        return jnp.concatenate(tops, axis=0).astype(x_ref.dtype).astype(F32)

    def accumulate(kt_in, x_ref, tile_max, m_run):
        m_new = jnp.maximum(m_run, tile_max)
        alpha = jnp.exp2(m_run - m_new)
        m_staged = m_new.astype(x_ref.dtype)
        for hd in range(ATT_HEADS):
            p = jnp.exp2(x_ref[hd] - m_staged[hd:hd + 1, :]).astype(BF16)
            rows = slice(hd * V_ROWS, (hd + 1) * V_ROWS)
            pv = jnp.dot(vt_ref[kt_in, rows, :], p, preferred_element_type=F32)
            acc_scr[rows, :] = alpha[hd:hd + 1, :] * acc_scr[rows, :] + pv
        return m_new

    def pair_body(pair, carry, stage_next=True):
        max_a, m_run = carry
        max_b = logits_stage(2 * pair + 1, xb_scr)
        m_run = accumulate(2 * pair, xa_scr, max_a, m_run)
        if stage_next:
            max_a = logits_stage(2 * pair + 2, xa_scr)
        m_run = accumulate(2 * pair + 1, xb_scr, max_b, m_run)
        return max_a, m_run

    n_pairs = (n_kt + 1) // 2
    first_max = logits_stage(0, xa_scr)
    carry = lax.fori_loop(0, n_pairs - 1, pair_body, (first_max, jnp.full((ATT_HEADS, qb), NEG, F32)))
    lax.cond(
        n_kt % 2 == 0,
        lambda c: pair_body(n_pairs - 1, c, stage_next=False)[1],
        lambda c: accumulate(2 * (n_pairs - 1), xa_scr, *c),
        carry)

    for hd in range(ATT_HEADS):
        num = acc_scr[hd * V_ROWS:hd * V_ROWS + ATT_DH, :]
        den = acc_scr[hd * V_ROWS + ATT_DH:hd * V_ROWS + ATT_DH + 1, :]
        out_scr[hd * ATT_DH:(hd + 1) * ATT_DH, :] = num / den
    o_ref[...] = out_scr[...].T.astype(o_ref.dtype)


def _attention(q_a, q_i, w_t, lim, k_i, k_a, v_t, *, qb, tk, n_keys, topk, causal, n_valid_q):
    batch, tq, _ = q_a.shape
    lp = k_i.shape[1]
    nq = tq // qb
    index_bits = int(np.ceil(np.log2(lp + 1)))
    assert lp // COUNT_ROWS <= 256, "bfloat16 partial counts are exact only up to 256"
    kernel = functools.partial(_attention_kernel, qb=qb, tk=tk, n_keys=n_keys, topk=topk, causal=causal,
                               n_valid_q=n_valid_q, index_bits=index_bits)
    return pl.pallas_call(
        kernel,
        grid=(batch, nq),
        in_specs=[
            pl.BlockSpec((1, 1, qb), lambda b, i: (i, 0, 0)),
            pl.BlockSpec((None, qb, ATT_WIDTH), lambda b, i: (b, i, 0)),
            pl.BlockSpec((None, qb, IDX_HEADS * IDX_DH), lambda b, i: (b, i, 0)),
            pl.BlockSpec((1, IDX_HEADS, qb), lambda b, i: (b, 0, i)),
            _resident((None, lp, LANES), lambda b, i: (b, 0, 0)),
            _resident((None, lp, ATT_WIDTH), lambda b, i: (b, 0, 0)),
            _resident((None, lp // tk, ATT_HEADS * V_ROWS, tk), lambda b, i: (b, 0, 0, 0)),
        ],
        out_specs=pl.BlockSpec((None, qb, ATT_WIDTH), lambda b, i: (b, i, 0)),
        out_shape=jax.ShapeDtypeStruct((batch, tq, ATT_WIDTH), BF16),
        scratch_shapes=[
            pltpu.VMEM((lp, qb), F32),
            pltpu.VMEM((lp, qb), jnp.int16),
            pltpu.VMEM((lp, qb), jnp.int16),
            pltpu.VMEM((ATT_HEADS, tk, qb), BF16),
            pltpu.VMEM((ATT_HEADS, tk, qb), BF16),
            pltpu.VMEM((ATT_HEADS * V_ROWS, qb), F32),
            pltpu.VMEM((ATT_WIDTH, qb), F32),
        ],
        compiler_params=_compiler_params(("parallel", "arbitrary")),
        name="attention",
    )(lim, q_a, q_i, w_t, k_i, k_a, v_t)


FF_CHUNK = 512


def _mlp_kernel(x_ref, yr_ref, ya_ref, wo_ref, g2_ref, wup_ref, wdn_ref, gf_ref, y_ref, x_scr, h_scr, a_scr):
    x = x_ref[...] + jnp.dot(yr_ref[...], wo_ref[:RET_WIDTH, :], preferred_element_type=F32)
    x = x + jnp.dot(ya_ref[...], wo_ref[RET_WIDTH:, :], preferred_element_type=F32)
    x_scr[...] = x
    ms = jnp.mean(x * x, axis=-1, keepdims=True)
    h_scr[...] = ((x * lax.rsqrt(ms + NORM_EPS)) * g2_ref[...]).astype(h_scr.dtype)
    for c in range(D_FF // FF_CHUNK):
        cols = slice(c * FF_CHUNK, (c + 1) * FF_CHUNK)
        u = jnp.dot(h_scr[...], wup_ref[:, cols], preferred_element_type=F32)
        a_scr[:, cols] = jnp.square(jnp.maximum(u, 0.0)).astype(a_scr.dtype)
    x = x_scr[...] + jnp.dot(a_scr[...], wdn_ref[...], preferred_element_type=F32)
    ms = jnp.mean(x * x, axis=-1, keepdims=True)
    y_ref[...] = (x * lax.rsqrt(ms + NORM_EPS)) * gf_ref[...]


def _mlp(x2d, y_r, y_a, w_out_b, norm2, w_up_b, w_down_b, norm_final, tm):
    m = x2d.shape[0]
    row = lambda i: (i, 0)
    const = lambda i: (0, 0)
    return pl.pallas_call(
        _mlp_kernel,
        grid=(m // tm,),
        in_specs=[
            pl.BlockSpec((tm, D_MODEL), row),
            pl.BlockSpec((tm, RET_WIDTH), row),
            pl.BlockSpec((tm, ATT_WIDTH), row),
            _resident((D_MODEL, D_MODEL), const),
            _resident((1, D_MODEL), const),
            _resident((D_MODEL, D_FF), const),
            _resident((D_FF, D_MODEL), const),
            _resident((1, D_MODEL), const),
        ],
        out_specs=pl.BlockSpec((tm, D_MODEL), row),
        out_shape=jax.ShapeDtypeStruct((m, D_MODEL), F32),
        scratch_shapes=[pltpu.VMEM((tm, D_MODEL), F32), pltpu.VMEM((tm, D_MODEL), BF16),
                        pltpu.VMEM((tm, D_FF), BF16)],
        compiler_params=_compiler_params(("parallel",)),
        name="mlp",
    )(x2d, y_r, y_a, w_out_b, norm2, w_up_b, w_down_b, norm_final)


def _pick_tile(n, target):
    t = min(n, target)
    while n % t:
        t //= 2
    return t


def _layer(x, pos, past, s0, ret_blk, weights, *, attn_qb, attn_tk):
    norm1, w_in_b, gn_gain, w_out_b, norm2, w_up_b, w_down_b, norm_final = weights
    batch, seq, _ = x.shape
    m = batch * seq
    x2d = x.reshape(m, D_MODEL)
    tm = _pick_tile(m, ROW_TILE)

    tabs = _rope_tables(pos)
    if seq < tm:
        tabs = jnp.tile(tabs, (1, tm // seq, 1))
    ret_in, q_a, q_i, kw, k_a, v_a, ka16, vt16, kw16 = _proj(x2d, norm1, w_in_b, tabs, tm)

    y_r, s_new = _retention(ret_in, s0, gn_gain, batch, seq, ret_blk)

    three = lambda a: a.reshape(batch, seq, -1)
    k_i = three(kw[:, :IDX_DH])
    w_t = jnp.swapaxes(three(kw[:, IDX_DH:IDX_DH + IDX_HEADS]) * IDX_HEADS ** -0.5 * IDX_DH ** -0.5, 1, 2)
    q_a3, q_i3 = three(q_a), three(q_i)
    n_keys = seq if past is None else past[0].shape[1] + seq
    topk = min(TOPK_MAX, n_keys // 4)
    tq = -(-seq // attn_qb) * attn_qb
    lp = -(-n_keys // attn_tk) * attn_tk
    padq = lambda a: jnp.pad(a, ((0, 0), (0, tq - seq), (0, 0)))
    padk = lambda a: jnp.pad(a, ((0, 0), (0, lp - n_keys), (0, 0)))
    if past is None and attn_tk == vt16.shape[2] and lp == n_keys:
        keys_i, keys_a = three(kw16), three(ka16)
        v_t = vt16.reshape(batch, lp // attn_tk, ATT_HEADS * V_ROWS, attn_tk)
        n_valid_q = attn_qb
    else:
        keys_i, keys_a, vals = three(kw16), three(ka16), three(v_a).astype(BF16)
        n_valid_q = attn_qb
        if past is not None:
            past_k, past_v, past_ki = past
            pad_i = jnp.pad(past_ki, ((0, 0), (0, 0), (0, LANES - IDX_DH))).astype(BF16)
            keys_i = jnp.concatenate([pad_i, keys_i], axis=1)
            keys_a = jnp.concatenate([past_k.reshape(batch, -1, ATT_WIDTH).astype(BF16), keys_a], axis=1)
            vals = jnp.concatenate([past_v.reshape(batch, -1, ATT_WIDTH).astype(BF16), vals], axis=1)
            n_valid_q = seq
        keys_i, keys_a = padk(keys_i), padk(keys_a)
        v_t = jnp.transpose(padk(vals).reshape(batch, lp // attn_tk, attn_tk, ATT_HEADS, ATT_DH), (0, 1, 3, 4, 2))
        ones_rows = jnp.zeros(v_t.shape[:3] + (V_ROWS - ATT_DH, attn_tk), BF16).at[:, :, :, 0, :].set(1.0)
        v_t = jnp.concatenate([v_t, ones_rows], axis=3).reshape(batch, lp // attn_tk, ATT_HEADS * V_ROWS, attn_tk)
    lim = jnp.minimum((pos // CHUNK + 1) * CHUNK, n_keys).astype(jnp.int32)
    lim = jnp.pad(lim, (0, tq - seq), constant_values=n_keys).reshape(tq // attn_qb, 1, attn_qb)
    y_a = _attention(
        padq(q_a3), padq(q_i3), jnp.pad(w_t, ((0, 0), (0, 0), (0, tq - seq))), lim, keys_i, keys_a, v_t,
        qb=attn_qb, tk=attn_tk, n_keys=n_keys, topk=topk, causal=past is None, n_valid_q=n_valid_q)
    y_a = y_a[:, :seq].reshape(m, ATT_WIDTH)

    y = _mlp(x2d, y_r, y_a, w_out_b, norm2, w_up_b, w_down_b, norm_final, tm)
    return (y.reshape(batch, seq, D_MODEL),
            k_a.reshape(batch, seq, ATT_HEADS, ATT_DH),
            v_a.reshape(batch, seq, ATT_HEADS, ATT_DH),
            k_i, s_new)


def kernel(x_prompt, x_sample, cache_k_att, cache_v_att, cache_k_idx, state_ret, norm1, w_in, gn_gain, w_out,
           norm2, w_up, w_down, norm_final):
    depth = norm1.shape[0]
    assert depth == 1, "the final norm is fused into the layer's last kernel"
    past_len = cache_k_att.shape[2]
    pos_p = jnp.arange(x_prompt.shape[1], dtype=jnp.int32)
    pos_s = past_len + jnp.arange(x_sample.shape[1], dtype=jnp.int32)
    l = 0
    w_in_b = jnp.pad(w_in[l], ((0, 0), (0, IN_WIDTH_PADDED - IN_WIDTH))).astype(BF16)
    weights = (norm1[l][None], w_in_b, gn_gain[l][None], w_out[l].astype(BF16), norm2[l][None],
               w_up[l].astype(BF16), w_down[l].astype(BF16), norm_final[None])
    s0_p = jnp.zeros((x_prompt.shape[0], RET_HEADS, RET_DK, RET_DV), F32)
    yp, kap, vap, kip, sp = _layer(x_prompt, pos_p, None, s0_p, PROMPT_RETENTION_BLOCK, weights,
                                   attn_qb=PROMPT_QUERY_BLOCK, attn_tk=KEY_TILE)
    past = (cache_k_att[l], cache_v_att[l], cache_k_idx[l])
    ys, kas, vas, kis, ss = _layer(x_sample, pos_s, past, state_ret[l].astype(F32), x_sample.shape[1], weights,
                                   attn_qb=SAMPLE_QUERY_BLOCK, attn_tk=KEY_TILE)
    return (yp, ys, kap[None], vap[None], kip[None], sp[None], kas[None], vas[None], kis[None], ss[None])
```

```python
import functools

import jax
import jax.numpy as jnp
import numpy as np
from jax import lax
from jax.experimental import pallas as pl
from jax.experimental.pallas import tpu as pltpu

D_MODEL = 1024
CHUNK = 64
ROPE_THETA = 10000.0
NORM_EPS = 1e-6
NEG = -1e30
RET_WIDTH = 512
ATT_WIDTH = 512
RET_HEADS = 4
RET_DK = 128
RET_DV = 128
ATT_HEADS = 8
ATT_DH = 64
IDX_HEADS = 8
IDX_DH = 64
TOPK_MAX = 256
D_FF = 4 * D_MODEL
IN_WIDTH = 4 * RET_WIDTH + 3 * ATT_WIDTH + IDX_HEADS * IDX_DH + IDX_DH + IDX_HEADS

LANES = 128
IN_WIDTH_PADDED = ((IN_WIDTH + LANES - 1) // LANES) * LANES
VMEM_LIMIT_BYTES = 56 * 1024 * 1024

ROW_TILE = 512
KEY_TILE = 256
PROMPT_QUERY_BLOCK = 256
SAMPLE_QUERY_BLOCK = 128
SAMPLE_KEY_TILE = 512
PROMPT_RETENTION_BLOCK = 256

F32 = jnp.float32
BF16 = jnp.bfloat16

_NT = (((1,), (1,)), ((), ()))


def _compiler_params(semantics):
    return pltpu.CompilerParams(dimension_semantics=semantics, vmem_limit_bytes=VMEM_LIMIT_BYTES)


def _resident(block_shape, index_map, prefetch_next=False):
    if prefetch_next:
        return pl.BlockSpec(block_shape, index_map)
    return pl.BlockSpec(block_shape, index_map, pipeline_mode=pl.Buffered(1))


def _rope_tables(pos):
    posf = pos.astype(F32)[:, None]

    def angles(d):
        half = d // 2
        inv = ROPE_THETA ** (-jnp.arange(half, dtype=F32) * 2.0 / d)
        ang = posf * inv[None, :]
        return jnp.cos(ang), jnp.sin(ang)

    c128, s128 = angles(128)
    zero64 = jnp.zeros_like(s128)
    cos128 = jnp.concatenate([c128, c128], axis=1)
    sin128 = jnp.concatenate([-s128, s128], axis=1)
    c64, s64 = angles(64)
    zero32 = jnp.zeros_like(s64)
    cos64 = jnp.concatenate([c64, c64, c64, c64], axis=1)
    sin64_up = jnp.concatenate([zero32, s64, zero32, s64], axis=1)
    sin64_dn = jnp.concatenate([-s64, zero32, -s64, zero32], axis=1)
    one64 = jnp.ones_like(zero64)
    cosk = jnp.concatenate([c64, c64, one64], axis=1)
    sink_up = jnp.concatenate([zero32, s64, zero64], axis=1)
    sink_dn = jnp.concatenate([-s64, zero32, zero64], axis=1)
    return jnp.stack([cos128, sin128, cos64, sin64_up, sin64_dn, cosk, sink_up, sink_dn])


V_TILE = KEY_TILE
V_ROWS = ATT_DH + 16


def _proj_kernel(x_ref, g_ref, w_ref, tab_ref, ret_ref, qa_ref, qi_ref, kw_ref, ka_ref, va_ref,
                 ka16_ref, vt16_ref, kw16_ref):
    x = x_ref[...]
    ms = jnp.mean(x * x, axis=-1, keepdims=True)
    h = ((x * lax.rsqrt(ms + NORM_EPS)) * g_ref[...]).astype(BF16)

    def mm(c0, n):
        return jnp.dot(h, w_ref[:, c0:c0 + n], preferred_element_type=F32)

    def rope128(z):
        return z * tab_ref[0] + pltpu.roll(z, 64, 1) * tab_ref[1]

    def rope64(z):
        return z * tab_ref[2] + pltpu.roll(z, 32, 1) * tab_ref[3] + pltpu.roll(z, 96, 1) * tab_ref[4]

    def ropek(z):
        return z * tab_ref[5] + pltpu.roll(z, 32, 1) * tab_ref[6] + pltpu.roll(z, 96, 1) * tab_ref[7]

    def emit(out_ref, out_c0, w_c0, rope):
        z = mm(w_c0, 512)
        for t in range(4):
            zt = z[:, t * LANES:(t + 1) * LANES]
            if rope is not None:
                zt = rope(zt)
            out_ref[:, out_c0 + t * LANES:out_c0 + (t + 1) * LANES] = zt.astype(out_ref.dtype)

    emit(ret_ref, 0, 0, rope128)
    emit(ret_ref, 512, 512, rope128)
    emit(ret_ref, 1024, 1024, None)
    emit(ret_ref, 1536, 1536, None)
    emit(qa_ref, 0, 2048, rope64)
    emit(qi_ref, 0, 3584, rope64)
    k_a = mm(2560, 512)
    k_a = jnp.concatenate([rope64(k_a[:, t * LANES:(t + 1) * LANES]) for t in range(4)], axis=1)
    ka_ref[...] = k_a
    ka16_ref[...] = k_a.astype(ka16_ref.dtype)
    v_a = mm(3072, 512)
    va_ref[...] = v_a
    v_tile = vt16_ref.shape[2]
    ones_rows = jnp.where(lax.broadcasted_iota(jnp.int32, (V_ROWS - ATT_DH, v_tile), 0) == 0, 1.0, 0.0)
    for part in range(vt16_ref.shape[0]):
        v_t = v_a[part * v_tile:(part + 1) * v_tile, :].T.astype(vt16_ref.dtype)
        for hd in range(ATT_HEADS):
            vt16_ref[part, hd * V_ROWS:hd * V_ROWS + ATT_DH, :] = v_t[hd * ATT_DH:(hd + 1) * ATT_DH, :]
            vt16_ref[part, hd * V_ROWS + ATT_DH:(hd + 1) * V_ROWS, :] = ones_rows.astype(vt16_ref.dtype)
    kw = ropek(mm(4096, LANES))
    kw_ref[...] = kw
    kw16_ref[...] = kw.astype(kw16_ref.dtype)


def _proj(x2d, norm_g, w_in_b, tabs, tm):
    m = x2d.shape[0]
    n_tab = tabs.shape[1] // tm
    row = lambda i: (i, 0)
    row3 = lambda i: (i, 0, 0)
    assert tm % V_TILE == 0 or m == tm, (tm, m)
    n_vt = max(tm // V_TILE, 1)
    v_tile = min(V_TILE, tm)
    outs = (
        jax.ShapeDtypeStruct((m, 4 * RET_WIDTH), F32),
        jax.ShapeDtypeStruct((m, ATT_WIDTH), F32),
        jax.ShapeDtypeStruct((m, IDX_HEADS * IDX_DH), F32),
        jax.ShapeDtypeStruct((m, LANES), F32),
        jax.ShapeDtypeStruct((m, ATT_WIDTH), F32),
        jax.ShapeDtypeStruct((m, ATT_WIDTH), F32),
        jax.ShapeDtypeStruct((m, ATT_WIDTH), BF16),
        jax.ShapeDtypeStruct((m // v_tile, ATT_HEADS * V_ROWS, v_tile), BF16),
        jax.ShapeDtypeStruct((m, LANES), BF16),
    )
    return pl.pallas_call(
        _proj_kernel,
        grid=(m // tm,),
        in_specs=[
            pl.BlockSpec((tm, D_MODEL), row),
            _resident((1, D_MODEL), lambda i: (0, 0)),
            _resident((D_MODEL, IN_WIDTH_PADDED), lambda i: (0, 0)),
            pl.BlockSpec((8, tm, LANES), lambda i: (0, i % n_tab, 0)),
        ],
        out_specs=[
            pl.BlockSpec((tm, 4 * RET_WIDTH), row),
            pl.BlockSpec((tm, ATT_WIDTH), row),
            pl.BlockSpec((tm, IDX_HEADS * IDX_DH), row),
            pl.BlockSpec((tm, LANES), row),
            pl.BlockSpec((tm, ATT_WIDTH), row),
            pl.BlockSpec((tm, ATT_WIDTH), row),
            pl.BlockSpec((tm, ATT_WIDTH), row),
            pl.BlockSpec((n_vt, ATT_HEADS * V_ROWS, v_tile), row3),
            pl.BlockSpec((tm, LANES), row),
        ],
        out_shape=outs,
        compiler_params=_compiler_params(("parallel",)),
        name="proj",
    )(x2d, norm_g, w_in_b, tabs)


def _retention_tables(blk):
    lg = jnp.log1p(-jnp.exp2(-5.0 - jnp.arange(RET_HEADS, dtype=F32)))
    i = jnp.arange(blk, dtype=F32)
    diff = i[:, None] - i[None, :]
    dmat = jnp.where(diff[None] >= 0, jnp.exp(lg[:, None, None] * jnp.maximum(diff, 0.0)[None]), 0.0)
    w_end = jnp.exp(lg[:, None] * (blk - 1.0 - i)[None, :])
    w_q = jnp.exp(lg[:, None] * (i + 1.0)[None, :])
    g_block = jnp.exp(lg * blk)
    lane = lambda a: jnp.broadcast_to(a[:, :, None], (RET_HEADS, blk, LANES))
    return dmat, lane(w_q), lane(w_end), g_block


def _retention_kernel(gblk_ref, q_ref, k_ref, v_ref, g_ref, s0_ref, dmat_ref, wq_ref, wend_ref, gain_ref,
                      y_ref, sfin_ref, s_scr):
    t = pl.program_id(1)

    @pl.when(t == 0)
    def _():
        s_scr[...] = s0_ref[0]

    for hd in range(RET_HEADS):
        cols = slice(hd * RET_DK, (hd + 1) * RET_DK)
        q = q_ref[:, cols]
        k = k_ref[:, cols] * RET_DK ** -0.5
        vb = v_ref[:, cols].astype(BF16)
        state = s_scr[hd]
        scores = lax.dot_general(q.astype(BF16), k.astype(BF16), _NT, preferred_element_type=F32) * dmat_ref[hd]
        o = jnp.dot(scores.astype(BF16), vb, preferred_element_type=F32)
        o = o + jnp.dot((q * wq_ref[hd]).astype(BF16), state.astype(BF16), preferred_element_type=F32)
        k_end_t = (k * wend_ref[hd]).T.astype(BF16)
        s_scr[hd] = gblk_ref[hd] * state + jnp.dot(k_end_t, vb, preferred_element_type=F32)
        mu = jnp.mean(o, axis=-1, keepdims=True)
        var = jnp.mean(jnp.square(o - mu), axis=-1, keepdims=True)
        on = ((o - mu) * lax.rsqrt(var + NORM_EPS)) * gain_ref[:, cols]
        g = g_ref[:, cols]
        y_ref[:, cols] = ((g * jax.nn.sigmoid(g)) * on).astype(y_ref.dtype)

    @pl.when(t == pl.num_programs(1) - 1)
    def _():
        sfin_ref[0] = s_scr[...]


def _retention(ret_in, s0, gn_gain, batch, seq, blk):
    nt = seq // blk
    dmat, w_q, w_end, g_block = _retention_tables(blk)

    def col(c):
        return pl.BlockSpec((blk, RET_WIDTH), lambda b, t: (b * nt + t, c))

    const3 = lambda b, t: (0, 0, 0)
    return pl.pallas_call(
        _retention_kernel,
        grid=(batch, nt),
        in_specs=[
            pl.BlockSpec(memory_space=pltpu.SMEM),
            col(0), col(1), col(2), col(3),
            pl.BlockSpec((1, RET_HEADS, RET_DK, RET_DV), lambda b, t: (b, 0, 0, 0)),
            pl.BlockSpec((RET_HEADS, blk, blk), const3),
            pl.BlockSpec((RET_HEADS, blk, LANES), const3),
            pl.BlockSpec((RET_HEADS, blk, LANES), const3),
            pl.BlockSpec((1, RET_WIDTH), lambda b, t: (0, 0)),
        ],
        out_specs=[
            pl.BlockSpec((blk, RET_WIDTH), lambda b, t: (b * nt + t, 0)),
            pl.BlockSpec((1, RET_HEADS, RET_DK, RET_DV), lambda b, t: (b, 0, 0, 0)),
        ],
        out_shape=(
            jax.ShapeDtypeStruct((batch * seq, RET_WIDTH), BF16),
            jax.ShapeDtypeStruct((batch, RET_HEADS, RET_DK, RET_DV), F32),
        ),
        scratch_shapes=[pltpu.VMEM((RET_HEADS, RET_DK, RET_DV), F32)],
        compiler_params=_compiler_params(("parallel", "arbitrary")),
        name="retention",
    )(g_block, ret_in, ret_in, ret_in, ret_in, s0, dmat, w_q, w_end, gn_gain)


COUNT_ROWS = 32
COUNT_UNROLL = 8
TILE_UNROLL = 4
MASKED = 2.0 * NEG
LOG2_E = 1.4426950408889634


def _ordered_bits_to_float(u):
    bits = jnp.where(u < 0, u ^ jnp.int32(-2 ** 31), ~u)
    return lax.bitcast_convert_type(bits, F32)


def _signed_half(x):
    return (x - 2 ** 15).astype(jnp.int16)


def _attention_kernel(lim_ref, qa_ref, qi_ref, wt_ref, ki_ref, ka_ref, vt_ref, o_ref,
                      s_scr, hi16_scr, lo16_scr, xa_scr, xb_scr, acc_scr, out_scr,
                      *, qb, tk, n_keys, topk, causal, n_valid_q, index_bits):
    n_kt = pl.program_id(1) * (qb // tk) + (qb // tk) if causal else ki_ref.shape[0] // tk
    lim = lim_ref[0]
    lane = lax.broadcasted_iota(jnp.int32, (qb, LANES), 1)
    key_iota = lax.broadcasted_iota(jnp.int32, (tk, qb), 0)

    def head_operand(ref, hd, scale, shift_to_low):
        pair = ref[:, (hd // 2) * LANES:(hd // 2 + 1) * LANES] * scale
        if shift_to_low and hd % 2:
            pair = pltpu.roll(pair, 64, 1)
        keep = (lane < 64) if (shift_to_low or hd % 2 == 0) else (lane >= 64)
        return jnp.where(keep, pair, 0.0).T.astype(BF16)

    qi_heads = [head_operand(qi_ref, hd, 1.0, True) for hd in range(IDX_HEADS)]
    wt = wt_ref[0]

    def score_body(kt, carry, some_inadmissible):
        r0 = pl.multiple_of(kt * tk, tk)
        keys = ki_ref[pl.ds(r0, tk), :]
        acc = jnp.zeros((tk, qb), F32)
        for hd in range(IDX_HEADS):
            d = jnp.dot(keys, qi_heads[hd], preferred_element_type=F32)
            acc = acc + jnp.maximum(d, 0.0) * wt[hd:hd + 1, :]
        s = jnp.where(acc == 0.0, 0.0, acc)
        if some_inadmissible:
            s = jnp.where(key_iota + r0 < lim, s, -jnp.inf)
        s_scr[pl.ds(r0, tk), :] = s
        bits = lax.bitcast_convert_type(s, jnp.int32)
        ordered = bits ^ (lax.shift_right_arithmetic(bits, 31) | jnp.int32(-2 ** 31))
        hi16_scr[pl.ds(r0, tk), :] = _signed_half(lax.shift_right_logical(ordered, 16))
        lo16_scr[pl.ds(r0, tk), :] = _signed_half(ordered & 0xFFFF)
        return carry

    n_all_admissible = jnp.minimum(jnp.min(lim) // tk, n_kt)
    lax.fori_loop(0, n_all_admissible, functools.partial(score_body, some_inadmissible=False), 0)
    lax.fori_loop(n_all_admissible, n_kt, functools.partial(score_body, some_inadmissible=True), 0)

    def count(src_ref, indicator, dtype):
        def one_tile(kt, acc):
            for part in range(tk // COUNT_ROWS):
                r0 = pl.multiple_of(kt * tk + part * COUNT_ROWS, COUNT_ROWS)
                acc = acc + indicator(src_ref[pl.ds(r0, COUNT_ROWS), :], r0)
            return acc
        def tile_group(g, acc):
            for j in range(COUNT_UNROLL):
                acc = one_tile(COUNT_UNROLL * g + j, acc)
            return acc

        acc = lax.fori_loop(0, n_kt // COUNT_UNROLL, tile_group, jnp.zeros((COUNT_ROWS, qb), dtype))
        acc = lax.fori_loop(n_kt - n_kt % COUNT_UNROLL, n_kt, one_tile, acc)
        return jnp.sum(acc.astype(F32), axis=0, keepdims=True)

    n_masked = (n_keys - lim).astype(F32)
    kf = float(topk)
    q_lane = lax.broadcasted_iota(jnp.int32, (1, qb), 1)
    one16 = jnp.ones((), BF16)
    zero16 = jnp.zeros((), BF16)

    def bisect_step(it, carry, half_ref):
        cur, cnt = carry
        cand = cur | lax.shift_left(jnp.int32(1), 31 - it)
        if half_ref is hi16_scr:
            half = _signed_half(lax.shift_right_logical(cand, 16))
        else:
            half = _signed_half(cand & 0xFFFF)
        c = count(half_ref, lambda h, r0: jnp.where(h >= half, one16, zero16), BF16)
        ok = c + jnp.where(_ordered_bits_to_float(cand) <= NEG, n_masked, 0.0) >= kf
        return jnp.where(ok, cand, cur), jnp.where(ok, c, cnt)

    carry = (jnp.zeros((1, qb), jnp.int32), jnp.zeros((1, qb), F32))
    carry = lax.fori_loop(0, 16, lambda it, st: bisect_step(it, st, hi16_scr), carry)

    top = _signed_half(lax.shift_right_logical(carry[0], 16))

    def for_each_tile(body):
        def group(g, c):
            for j in range(TILE_UNROLL):
                body(TILE_UNROLL * g + j)
            return c

        def single(kt, c):
            body(kt)
            return c

        lax.fori_loop(0, n_kt // TILE_UNROLL, group, 0)
        lax.fori_loop(n_kt - n_kt % TILE_UNROLL, n_kt, single, 0)

    def narrow_body(kt):
        r0 = pl.multiple_of(kt * tk, tk)
        hi = hi16_scr[pl.ds(r0, tk), :]
        lo = lo16_scr[pl.ds(r0, tk), :]
        lo16_scr[pl.ds(r0, tk), :] = jnp.where(hi > top, jnp.int16(2 ** 15 - 1),
                                               jnp.where(hi == top, lo, jnp.int16(-2 ** 15)))

    for_each_tile(narrow_body)
    cur, cnt_ge = lax.fori_loop(16, 32, lambda it, st: bisect_step(it, st, lo16_scr), carry)
    theta = _ordered_bits_to_float(cur)

    overflow = jnp.where((cnt_ge > kf) & (q_lane < n_valid_q), 1.0, 0.0)
    all_idx = jnp.full((1, qb), 2 ** index_bits - 1, jnp.int32)

    def tie_cut():
        need = kf - count(s_scr, lambda s, r0: jnp.where(s > theta, 1.0, 0.0), F32)

        def mark_body(kt):
            r0 = pl.multiple_of(kt * tk, tk)
            tied = jnp.where(s_scr[pl.ds(r0, tk), :] == theta, key_iota + r0, 2 ** 15 - 1)
            hi16_scr[pl.ds(r0, tk), :] = tied.astype(jnp.int16)

        for_each_tile(mark_body)

        def body(it, cut):
            cand = cut | lax.shift_left(jnp.int32(1), index_bits - 1 - it)
            cand16 = cand.astype(jnp.int16)
            c = count(hi16_scr, lambda idx, r0: jnp.where(idx < cand16, one16, zero16), BF16)
            return jnp.where(c <= need, cand, cut)

        return lax.fori_loop(0, index_bits, body, jnp.zeros((1, qb), jnp.int32))

    any_overflow = jnp.max(overflow) > 0.0
    idx_cut = lax.cond(any_overflow, tie_cut, lambda: all_idx)

    def bias_body(kt, with_ties):
        r0 = pl.multiple_of(kt * tk, tk)
        s = s_scr[pl.ds(r0, tk), :]
        if with_ties:
            tie = jnp.where(key_iota + r0 < idx_cut, 0.0, MASKED)
            s_scr[pl.ds(r0, tk), :] = jnp.where(s > theta, 0.0, jnp.where(s == theta, tie, MASKED))
        else:
            s_scr[pl.ds(r0, tk), :] = jnp.where(s >= theta, 0.0, MASKED)

    lax.cond(any_overflow,
             lambda: for_each_tile(functools.partial(bias_body, with_ties=True)),
             lambda: for_each_tile(functools.partial(bias_body, with_ties=False)))

    qa_heads = [head_operand(qa_ref, hd, ATT_DH ** -0.5 * LOG2_E, False) for hd in range(ATT_HEADS)]
    acc_scr[...] = jnp.zeros(acc_scr.shape, F32)

    def logits_stage(kt, x_ref):
        r0 = pl.multiple_of(kt * tk, tk)
        tops = []
        for hd in range(ATT_HEADS):
            keys = ka_ref[pl.ds(r0, tk), (hd // 2) * LANES:(hd // 2 + 1) * LANES]
            x = jnp.dot(keys, qa_heads[hd], preferred_element_type=F32) + s_scr[pl.ds(r0, tk), :]
            x_ref[hd] = x.astype(x_ref.dtype)
            tops.append(jnp.max(x, axis=0, keepdims=True))
        return jnp.concatenate(tops, axis=0).astype(x_ref.dtype).astype(F32)

    def accumulate(kt_in, x_ref, tile_max, m_run):
        m_new = jnp.maximum(m_run, tile_max)
        alpha = jnp.exp2(m_run - m_new)
        m_staged = m_new.astype(x_ref.dtype)
        for hd in range(ATT_HEADS):
            p = jnp.exp2(x_ref[hd] - m_staged[hd:hd + 1, :]).astype(BF16)
            rows = slice(hd * V_ROWS, (hd + 1) * V_ROWS)
            pv = jnp.dot(vt_ref[kt_in, rows, :], p, preferred_element_type=F32)
            acc_scr[rows, :] = alpha[hd:hd + 1, :] * acc_scr[rows, :] + pv
        return m_new

    def pair_body(pair, carry, stage_next=True):
        max_a, m_run = carry
        max_b = logits_stage(2 * pair + 1, xb_scr)
        m_run = accumulate(2 * pair, xa_scr, max_a, m_run)
        if stage_next:
            max_a = logits_stage(2 * pair + 2, xa_scr)
        m_run = accumulate(2 * pair + 1, xb_scr, max_b, m_run)
        return max_a, m_run

    n_pairs = (n_kt + 1) // 2
    first_max = logits_stage(0, xa_scr)
    carry = lax.fori_loop(0, n_pairs - 1, pair_body, (first_max, jnp.full((ATT_HEADS, qb), NEG, F32)))
    lax.cond(
        n_kt % 2 == 0,
        lambda c: pair_body(n_pairs - 1, c, stage_next=False)[1],
        lambda c: accumulate(2 * (n_pairs - 1), xa_scr, *c),
        carry)

    for hd in range(ATT_HEADS):
        num = acc_scr[hd * V_ROWS:hd * V_ROWS + ATT_DH, :]
        den = acc_scr[hd * V_ROWS + ATT_DH:hd * V_ROWS + ATT_DH + 1, :]
        out_scr[hd * ATT_DH:(hd + 1) * ATT_DH, :] = num / den
    o_ref[...] = out_scr[...].T.astype(o_ref.dtype)


def _attention(q_a, q_i, w_t, lim, k_i, k_a, v_t, *, qb, tk, n_keys, topk, causal, n_valid_q):
    batch, tq, _ = q_a.shape
    lp = k_i.shape[1]
    nq = tq // qb
    index_bits = int(np.ceil(np.log2(lp + 1)))
    assert lp // COUNT_ROWS <= 256, "bfloat16 partial counts are exact only up to 256"
    resident_bytes = 2 * lp * (LANES + ATT_WIDTH + ATT_HEADS * V_ROWS)
    prefetch = 2 * resident_bytes <= VMEM_LIMIT_BYTES // 2
    kernel = functools.partial(_attention_kernel, qb=qb, tk=tk, n_keys=n_keys, topk=topk, causal=causal,
                               n_valid_q=n_valid_q, index_bits=index_bits)
    return pl.pallas_call(
        kernel,
        grid=(batch, nq),
        in_specs=[
            pl.BlockSpec((1, 1, qb), lambda b, i: (i, 0, 0)),
            pl.BlockSpec((None, qb, ATT_WIDTH), lambda b, i: (b, i, 0)),
            pl.BlockSpec((None, qb, IDX_HEADS * IDX_DH), lambda b, i: (b, i, 0)),
            pl.BlockSpec((1, IDX_HEADS, qb), lambda b, i: (b, 0, i)),
            _resident((None, lp, LANES), lambda b, i: (b, 0, 0), prefetch),
            _resident((None, lp, ATT_WIDTH), lambda b, i: (b, 0, 0), prefetch),
            _resident((None, lp // tk, ATT_HEADS * V_ROWS, tk), lambda b, i: (b, 0, 0, 0), prefetch),
        ],
        out_specs=pl.BlockSpec((None, qb, ATT_WIDTH), lambda b, i: (b, i, 0)),
        out_shape=jax.ShapeDtypeStruct((batch, tq, ATT_WIDTH), BF16),
        scratch_shapes=[
            pltpu.VMEM((lp, qb), F32),
            pltpu.VMEM((lp, qb), jnp.int16),
            pltpu.VMEM((lp, qb), jnp.int16),
            pltpu.VMEM((ATT_HEADS, tk, qb), BF16),
            pltpu.VMEM((ATT_HEADS, tk, qb), BF16),
            pltpu.VMEM((ATT_HEADS * V_ROWS, qb), F32),
            pltpu.VMEM((ATT_WIDTH, qb), F32),
        ],
        compiler_params=_compiler_params(("parallel", "arbitrary")),
        name="attention",
    )(lim, q_a, q_i, w_t, k_i, k_a, v_t)


FF_CHUNK = 512


def _mlp_kernel(x_ref, yr_ref, ya_ref, wo_ref, g2_ref, wup_ref, wdn_ref, gf_ref, y_ref, x_scr, h_scr, a_scr):
    x = x_ref[...] + jnp.dot(yr_ref[...], wo_ref[:RET_WIDTH, :], preferred_element_type=F32)
    x = x + jnp.dot(ya_ref[...], wo_ref[RET_WIDTH:, :], preferred_element_type=F32)
    x_scr[...] = x
    ms = jnp.mean(x * x, axis=-1, keepdims=True)
    h_scr[...] = ((x * lax.rsqrt(ms + NORM_EPS)) * g2_ref[...]).astype(h_scr.dtype)
    for c in range(D_FF // FF_CHUNK):
        cols = slice(c * FF_CHUNK, (c + 1) * FF_CHUNK)
        u = jnp.dot(h_scr[...], wup_ref[:, cols], preferred_element_type=F32)
        a_scr[:, cols] = jnp.square(jnp.maximum(u, 0.0)).astype(a_scr.dtype)
    x = x_scr[...] + jnp.dot(a_scr[...], wdn_ref[...], preferred_element_type=F32)
    ms = jnp.mean(x * x, axis=-1, keepdims=True)
    y_ref[...] = (x * lax.rsqrt(ms + NORM_EPS)) * gf_ref[...]


def _mlp(x2d, y_r, y_a, w_out_b, norm2, w_up_b, w_down_b, norm_final, tm):
    m = x2d.shape[0]
    row = lambda i: (i, 0)
    const = lambda i: (0, 0)
    return pl.pallas_call(
        _mlp_kernel,
        grid=(m // tm,),
        in_specs=[
            pl.BlockSpec((tm, D_MODEL), row),
            pl.BlockSpec((tm, RET_WIDTH), row),
            pl.BlockSpec((tm, ATT_WIDTH), row),
            _resident((D_MODEL, D_MODEL), const),
            _resident((1, D_MODEL), const),
            _resident((D_MODEL, D_FF), const),
            _resident((D_FF, D_MODEL), const),
            _resident((1, D_MODEL), const),
        ],
        out_specs=pl.BlockSpec((tm, D_MODEL), row),
        out_shape=jax.ShapeDtypeStruct((m, D_MODEL), F32),
        scratch_shapes=[pltpu.VMEM((tm, D_MODEL), F32), pltpu.VMEM((tm, D_MODEL), BF16),
                        pltpu.VMEM((tm, D_FF), BF16)],
        compiler_params=_compiler_params(("parallel",)),
        name="mlp",
    )(x2d, y_r, y_a, w_out_b, norm2, w_up_b, w_down_b, norm_final)


def _pick_tile(n, target):
    t = min(n, target)
    while n % t:
        t //= 2
    return t


def _layer(x, pos, past, s0, ret_blk, weights, *, attn_qb, attn_tk):
    norm1, w_in_b, gn_gain, w_out_b, norm2, w_up_b, w_down_b, norm_final = weights
    batch, seq, _ = x.shape
    m = batch * seq
    x2d = x.reshape(m, D_MODEL)
    tm = _pick_tile(m, ROW_TILE)

    tabs = _rope_tables(pos)
    if seq < tm:
        tabs = jnp.tile(tabs, (1, tm // seq, 1))
    ret_in, q_a, q_i, kw, k_a, v_a, ka16, vt16, kw16 = _proj(x2d, norm1, w_in_b, tabs, tm)

    y_r, s_new = _retention(ret_in, s0, gn_gain, batch, seq, ret_blk)

    three = lambda a: a.reshape(batch, seq, -1)
    k_i = three(kw[:, :IDX_DH])
    w_t = jnp.swapaxes(three(kw[:, IDX_DH:IDX_DH + IDX_HEADS]) * IDX_HEADS ** -0.5 * IDX_DH ** -0.5, 1, 2)
    q_a3, q_i3 = three(q_a), three(q_i)
    n_keys = seq if past is None else past[0].shape[1] + seq
    topk = min(TOPK_MAX, n_keys // 4)
    tq = -(-seq // attn_qb) * attn_qb
    lp = -(-n_keys // attn_tk) * attn_tk
    padq = lambda a: jnp.pad(a, ((0, 0), (0, tq - seq), (0, 0)))
    padk = lambda a: jnp.pad(a, ((0, 0), (0, lp - n_keys), (0, 0)))
    if past is None and attn_tk == vt16.shape[2] and lp == n_keys:
        keys_i, keys_a = three(kw16), three(ka16)
        v_t = vt16.reshape(batch, lp // attn_tk, ATT_HEADS * V_ROWS, attn_tk)
        n_valid_q = attn_qb
    else:
        keys_i, keys_a, vals = three(kw16), three(ka16), three(v_a).astype(BF16)
        n_valid_q = attn_qb
        if past is not None:
            past_k, past_v, past_ki = past
            pad_i = jnp.pad(past_ki, ((0, 0), (0, 0), (0, LANES - IDX_DH))).astype(BF16)
            keys_i = jnp.concatenate([pad_i, keys_i], axis=1)
            keys_a = jnp.concatenate([past_k.reshape(batch, -1, ATT_WIDTH).astype(BF16), keys_a], axis=1)
            vals = jnp.concatenate([past_v.reshape(batch, -1, ATT_WIDTH).astype(BF16), vals], axis=1)
            n_valid_q = seq
        keys_i, keys_a = padk(keys_i), padk(keys_a)
        v_t = jnp.transpose(padk(vals).reshape(batch, lp // attn_tk, attn_tk, ATT_HEADS, ATT_DH), (0, 1, 3, 4, 2))
        ones_rows = jnp.zeros(v_t.shape[:3] + (V_ROWS - ATT_DH, attn_tk), BF16).at[:, :, :, 0, :].set(1.0)
        v_t = jnp.concatenate([v_t, ones_rows], axis=3).reshape(batch, lp // attn_tk, ATT_HEADS * V_ROWS, attn_tk)
    lim = jnp.minimum((pos // CHUNK + 1) * CHUNK, n_keys).astype(jnp.int32)
    lim = jnp.pad(lim, (0, tq - seq), constant_values=n_keys).reshape(tq // attn_qb, 1, attn_qb)
    y_a = _attention(
        padq(q_a3), padq(q_i3), jnp.pad(w_t, ((0, 0), (0, 0), (0, tq - seq))), lim, keys_i, keys_a, v_t,
        qb=attn_qb, tk=attn_tk, n_keys=n_keys, topk=topk, causal=past is None, n_valid_q=n_valid_q)
    y_a = y_a[:, :seq].reshape(m, ATT_WIDTH)

    y = _mlp(x2d, y_r, y_a, w_out_b, norm2, w_up_b, w_down_b, norm_final, tm)
    return (y.reshape(batch, seq, D_MODEL),
            k_a.reshape(batch, seq, ATT_HEADS, ATT_DH),
            v_a.reshape(batch, seq, ATT_HEADS, ATT_DH),
            k_i, s_new)


def kernel(x_prompt, x_sample, cache_k_att, cache_v_att, cache_k_idx, state_ret, norm1, w_in, gn_gain, w_out,
           norm2, w_up, w_down, norm_final):
    depth = norm1.shape[0]
    assert depth == 1, "the final norm is fused into the layer's last kernel"
    past_len = cache_k_att.shape[2]
    pos_p = jnp.arange(x_prompt.shape[1], dtype=jnp.int32)
    pos_s = past_len + jnp.arange(x_sample.shape[1], dtype=jnp.int32)
    l = 0
    w_in_b = jnp.pad(w_in[l], ((0, 0), (0, IN_WIDTH_PADDED - IN_WIDTH))).astype(BF16)
    weights = (norm1[l][None], w_in_b, gn_gain[l][None], w_out[l].astype(BF16), norm2[l][None],
               w_up[l].astype(BF16), w_down[l].astype(BF16), norm_final[None])
    s0_p = jnp.zeros((x_prompt.shape[0], RET_HEADS, RET_DK, RET_DV), F32)
    yp, kap, vap, kip, sp = _layer(x_prompt, pos_p, None, s0_p, PROMPT_RETENTION_BLOCK, weights,
                                   attn_qb=PROMPT_QUERY_BLOCK, attn_tk=KEY_TILE)
    past = (cache_k_att[l], cache_v_att[l], cache_k_idx[l])
    ys, kas, vas, kis, ss = _layer(x_sample, pos_s, past, state_ret[l].astype(F32), x_sample.shape[1], weights,
                                   attn_qb=SAMPLE_QUERY_BLOCK, attn_tk=SAMPLE_KEY_TILE)
    return (yp, ys, kap[None], vap[None], kip[None], sp[None], kas[None], vas[None], kis[None], ss[None])
```
